```python
import math
import jax, jax.numpy as jnp
from jax import lax
import numpy as np

D_MODEL = 4096
BATCH = 1
SEQ = 8192
DEPTH = 1
DEC_BATCH = 32
DEC_SEQ = 64
PAST_LEN = 2048

CHUNK = 64
HEAD_DIM = 128
D_MIX = D_MODEL
D_DIFF = D_MIX // 2
D_HGRN = D_MIX - D_DIFF
N_DIFF_HEADS = D_DIFF // HEAD_DIM
N_HGRN_HEADS = D_HGRN // HEAD_DIM
DIFF_QK = HEAD_DIM // 2
DIFF_V = HEAD_DIM
HGRN_DK = HEAD_DIM
HGRN_DV = HEAD_DIM
IN_COLS = 3 * D_DIFF + 4 * D_HGRN
N_MEM = 256
MEM_HEADS = 4
MEM_HEAD_DIM = D_MODEL // MEM_HEADS
D_FF = -(-8 * D_MODEL // (3 * 256)) * 256
REL_BUCKETS = 32
REL_MAX_DIST = 128
QBLOCK = 128
EPS = 1e-6

kernel_name = "hymba_diffattn_hgrn2_streaming_step"

F32 = jnp.float32


def _rms(x, g):
    xf = x.astype(F32)
    y = xf * lax.rsqrt(jnp.mean(xf * xf, axis=-1, keepdims=True) + EPS) * g.astype(F32)
    return y.astype(x.dtype)


def _rel_bucket(rel):
    nb = REL_BUCKETS // 2
    max_exact = nb // 2
    ret = jnp.where(rel > 0, nb, 0)
    n = jnp.abs(rel)
    nf = jnp.maximum(n, 1).astype(F32)
    large = max_exact + (jnp.log(nf / max_exact) / math.log(REL_MAX_DIST / max_exact)
                         * (nb - max_exact)).astype(jnp.int32)
    large = jnp.minimum(large, nb - 1)
    return ret + jnp.where(n < max_exact, n, large)


def _rel_bias(table, qpos, kpos):
    bucket = _rel_bucket(kpos[None, :] - qpos[:, None])
    return jnp.transpose(table.astype(F32)[bucket], (2, 0, 1))


def _chunk_mask(qpos, kpos):
    return (kpos[None, :] // CHUNK) <= (qpos[:, None] // CHUNK)


def _diff_lambda(lp, lam_init):
    lp = lp.astype(F32)
    return jnp.exp(jnp.sum(lp[0] * lp[1])) - jnp.exp(jnp.sum(lp[2] * lp[3])) + lam_init


def _diff_attention(q, k, v, bias, mask, lam):
    scale = DIFF_QK ** -0.5

    def probs(qi, ki):
        s = jnp.einsum('bqhd,bkhd->bhqk', qi, ki).astype(F32) * scale + bias
        s = jnp.where(mask, s, -jnp.inf)
        return jax.nn.softmax(s, axis=-1)

    p = probs(q[..., :DIFF_QK], k[..., :DIFF_QK]) - lam * probs(q[..., DIFF_QK:], k[..., DIFF_QK:])
    return jnp.einsum('bhqk,bkhd->bqhd', p.astype(v.dtype), v)


def _split_in(u):
    B, T, _ = u.shape
    dq = u[..., :D_DIFF].reshape(B, T, N_DIFF_HEADS, 2 * DIFF_QK)
    dk = u[..., D_DIFF:2 * D_DIFF].reshape(B, T, N_DIFF_HEADS, 2 * DIFF_QK)
    dv = u[..., 2 * D_DIFF:3 * D_DIFF].reshape(B, T, N_DIFF_HEADS, DIFF_V)
    o = 3 * D_DIFF
    hq, hf, hi, hg = [u[..., o + j * D_HGRN:o + (j + 1) * D_HGRN] for j in range(4)]
    return dq, dk, dv, hq, hf, hi, hg


def _hgrn_feats(hq, hf, hi, lb):
    B, T, _ = hq.shape
    shp = (B, T, N_HGRN_HEADS, HGRN_DK)
    q = jax.nn.silu(hq.astype(F32)).reshape(shp)
    z = hf.astype(F32)
    logf = jnp.log(lb + (1.0 - lb) * jax.nn.sigmoid(z)).reshape(shp)
    k = ((1.0 - lb) * jax.nn.sigmoid(-z)).reshape(shp)
    v = hi.astype(F32).reshape(B, T, N_HGRN_HEADS, HGRN_DV)
    return q, k, v, logf


def _hgrn2_chunk(S0, q, k, v, logf):
    L = q.shape[1]
    b = jnp.cumsum(logf, axis=1)
    qd = q * jnp.exp(b)
    kd = k * jnp.exp(-b)
    A = jnp.einsum('bthk,bshk->bhts', qd, kd)
    A = jnp.where(jnp.tril(jnp.ones((L, L), dtype=bool)), A, 0.0)
    o = jnp.einsum('bhts,bshv->bthv', A, v) + jnp.einsum('bthk,bhkv->bthv', qd, S0)
    bL = b[:, -1]
    kL = k * jnp.exp(bL[:, None] - b)
    S = jnp.exp(bL)[..., None] * S0 + jnp.einsum('bshk,bshv->bhkv', kL, v)
    return o, S


def _mixer_out(o_diff, o_hgrn, hg, diff_g, hgrn_g, lam_init, w_out, dtype):
    B, T = o_diff.shape[:2]
    od = _rms(o_diff, diff_g).astype(F32) * (1.0 - lam_init)
    oh = _rms(o_hgrn, hgrn_g) * jax.nn.silu(hg.astype(F32)).reshape(B, T, N_HGRN_HEADS, HGRN_DV)
    cat = jnp.concatenate([od.reshape(B, T, D_DIFF), oh.reshape(B, T, D_HGRN)], axis=-1).astype(dtype)
    return cat @ w_out


def _mem_kv(mem, g, wk, wv):
    B = mem.shape[0]
    m = _rms(mem, g)
    k = (m @ wk).reshape(B, N_MEM, MEM_HEADS, MEM_HEAD_DIM)
    v = (m @ wv).reshape(B, N_MEM, MEM_HEADS, MEM_HEAD_DIM)
    return k, v


def _cross_attn(h, g, wq, wo, mk, mv):
    B, T, _ = h.shape
    q = (_rms(h, g) @ wq).reshape(B, T, MEM_HEADS, MEM_HEAD_DIM)
    s = jnp.einsum('bqhd,bkhd->bhqk', q, mk).astype(F32) * (MEM_HEAD_DIM ** -0.5)
    p = jax.nn.softmax(s, axis=-1)
    o = jnp.einsum('bhqk,bkhd->bqhd', p.astype(mv.dtype), mv).reshape(B, T, D_MODEL)
    return o @ wo


def _ffn(h, g, wg, wu, wd):
    n = _rms(h, g)
    return (jax.nn.silu(n @ wg) * (n @ wu)) @ wd


def setup_inputs(seed: int = 0) -> dict:
    key = jax.random.key(seed)
    ks = iter(jax.random.split(key, 32))

    def nrm(shape, scale):
        return jax.random.normal(next(ks), shape, F32) * scale

    def gain(shape):
        return 1.0 + nrm(shape, 0.02)

    return {
        "x_prompt": nrm((BATCH, SEQ, D_MODEL), 1.0),
        "x_sample": nrm((DEC_BATCH, DEC_SEQ, D_MODEL), 1.0),
        "mem_prompt": nrm((BATCH, N_MEM, D_MODEL), 1.0),
        "cache_diff_k": nrm((DEPTH, DEC_BATCH, PAST_LEN, N_DIFF_HEADS, HEAD_DIM), 1.0),
        "cache_diff_v": nrm((DEPTH, DEC_BATCH, PAST_LEN, N_DIFF_HEADS, HEAD_DIM), 1.0),
        "state_hgrn": nrm((DEPTH, DEC_BATCH, N_HGRN_HEADS, HGRN_DK, HGRN_DV), 0.5),
        "cache_mem_k": nrm((DEPTH, DEC_BATCH, N_MEM, MEM_HEADS, MEM_HEAD_DIM), 1.0),
        "cache_mem_v": nrm((DEPTH, DEC_BATCH, N_MEM, MEM_HEADS, MEM_HEAD_DIM), 1.0),
        "norm_mix": gain((DEPTH, D_MODEL)),
        "w_in": nrm((DEPTH, D_MODEL, IN_COLS), D_MODEL ** -0.5),
        "diff_lambda": nrm((DEPTH, 4, DIFF_QK), 0.1),
        "diff_subln": gain((DEPTH, DIFF_V)),
        "rel_bias": nrm((REL_BUCKETS, N_DIFF_HEADS), 0.5),
        "hgrn_lb_logits": nrm((DEPTH + 1, D_HGRN), 0.1),
        "hgrn_norm": gain((DEPTH, HGRN_DV)),
        "w_out": nrm((DEPTH, D_MIX, D_MODEL), D_MIX ** -0.5),
        "norm_cross": gain((DEPTH, D_MODEL)),
        "norm_mem": gain((DEPTH, D_MODEL)),
        "w_mem_q": nrm((DEPTH, D_MODEL, D_MODEL), D_MODEL ** -0.5),
        "w_mem_k": nrm((DEPTH, D_MODEL, D_MODEL), D_MODEL ** -0.5),
        "w_mem_v": nrm((DEPTH, D_MODEL, D_MODEL), D_MODEL ** -0.5),
        "w_mem_o": nrm((DEPTH, D_MODEL, D_MODEL), D_MODEL ** -0.5),
        "norm_ffn": gain((DEPTH, D_MODEL)),
        "w_ffn_gate": nrm((DEPTH, D_MODEL, D_FF), D_MODEL ** -0.5),
        "w_ffn_up": nrm((DEPTH, D_MODEL, D_FF), D_MODEL ** -0.5),
        "w_ffn_down": nrm((DEPTH, D_FF, D_MODEL), D_FF ** -0.5),
        "norm_final": gain((D_MODEL,)),
    }


def reference(x_prompt, x_sample, mem_prompt, cache_diff_k, cache_diff_v, state_hgrn, cache_mem_k,
              cache_mem_v, norm_mix, w_in, diff_lambda, diff_subln, rel_bias, hgrn_lb_logits, hgrn_norm,
              w_out, norm_cross, norm_mem, w_mem_q, w_mem_k, w_mem_v, w_mem_o, norm_ffn, w_ffn_gate,
              w_ffn_up, w_ffn_down, norm_final):
    lb_all = jnp.cumsum(jax.nn.softmax(hgrn_lb_logits.astype(F32), axis=0), axis=0)

    hp = x_prompt
    Bp, Tp, _ = hp.shape
    n_qb = Tp // QBLOCK
    n_ch = Tp // CHUNK
    kpos_p = jnp.arange(Tp)
    pk_list, pv_list, ps_list, pmk_list, pmv_list = [], [], [], [], []
    for l in range(DEPTH):
        lam_init = 0.8 - 0.6 * math.exp(-0.3 * l)
        lam = _diff_lambda(diff_lambda[l], lam_init)
        lb = lb_all[l]
        u = _rms(hp, norm_mix[l]) @ w_in[l]
        dq, dk, dv, hq, hf, hi, hg = _split_in(u)

        def q_block(i, dq=dq, dk=dk, dv=dv, lam=lam):
            qs = i * QBLOCK
            qb = lax.dynamic_slice_in_dim(dq, qs, QBLOCK, axis=1)
            qpos = qs + jnp.arange(QBLOCK)
            bias = _rel_bias(rel_bias, qpos, kpos_p)
            return _diff_attention(qb, dk, dv, bias, _chunk_mask(qpos, kpos_p), lam)

        ob = lax.map(q_block, jnp.arange(n_qb))
        o_diff = jnp.transpose(ob, (1, 0, 2, 3, 4)).reshape(Bp, Tp, N_DIFF_HEADS, DIFF_V)

        q, k, v, logf = _hgrn_feats(hq, hf, hi, lb)

        def to_chunks(a):
            return jnp.transpose(a.reshape(Bp, n_ch, CHUNK, a.shape[2], a.shape[3]), (1, 0, 2, 3, 4))

        def step(S, xs):
            o_c, S_new = _hgrn2_chunk(S, *xs)
            return S_new, o_c

        S0 = jnp.zeros((Bp, N_HGRN_HEADS, HGRN_DK, HGRN_DV), F32)
        S_fin, oc = lax.scan(step, S0, (to_chunks(q), to_chunks(k), to_chunks(v), to_chunks(logf)))
        o_hgrn = jnp.transpose(oc, (1, 0, 2, 3, 4)).reshape(Bp, Tp, N_HGRN_HEADS, HGRN_DV)

        hp = hp + _mixer_out(o_diff, o_hgrn, hg, diff_subln[l], hgrn_norm[l], lam_init, w_out[l], hp.dtype)
        mk, mv = _mem_kv(mem_prompt, norm_mem[l], w_mem_k[l], w_mem_v[l])
        hp = hp + _cross_attn(hp, norm_cross[l], w_mem_q[l], w_mem_o[l], mk, mv)
        hp = hp + _ffn(hp, norm_ffn[l], w_ffn_gate[l], w_ffn_up[l], w_ffn_down[l])
        pk_list.append(dk)
        pv_list.append(dv)
        ps_list.append(S_fin.astype(x_prompt.dtype))
        pmk_list.append(mk)
        pmv_list.append(mv)
    y_prompt = _rms(hp, norm_final)

    hs = x_sample
    Bs, Ts, _ = hs.shape
    past = cache_diff_k.shape[2]
    qpos_s = past + jnp.arange(Ts)
    kpos_s = jnp.arange(past + Ts)
    bias_s = _rel_bias(rel_bias, qpos_s, kpos_s)
    mask_s = _chunk_mask(qpos_s, kpos_s)
    sk_list, sv_list, ss_list = [], [], []
    for l in range(DEPTH):
        lam_init = 0.8 - 0.6 * math.exp(-0.3 * l)
        lam = _diff_lambda(diff_lambda[l], lam_init)
        lb = lb_all[l]
        u = _rms(hs, norm_mix[l]) @ w_in[l]
        dq, dk, dv, hq, hf, hi, hg = _split_in(u)
        k_all = jnp.concatenate([cache_diff_k[l].astype(dk.dtype), dk], axis=1)
        v_all = jnp.concatenate([cache_diff_v[l].astype(dv.dtype), dv], axis=1)
        o_diff = _diff_attention(dq, k_all, v_all, bias_s, mask_s, lam)

        q, k, v, logf = _hgrn_feats(hq, hf, hi, lb)
        o_hgrn, S_new = _hgrn2_chunk(state_hgrn[l].astype(F32), q, k, v, logf)

        hs = hs + _mixer_out(o_diff, o_hgrn, hg, diff_subln[l], hgrn_norm[l], lam_init, w_out[l], hs.dtype)
        hs = hs + _cross_attn(hs, norm_cross[l], w_mem_q[l], w_mem_o[l],
                              cache_mem_k[l].astype(hs.dtype), cache_mem_v[l].astype(hs.dtype))
        hs = hs + _ffn(hs, norm_ffn[l], w_ffn_gate[l], w_ffn_up[l], w_ffn_down[l])
        sk_list.append(dk)
        sv_list.append(dv)
        ss_list.append(S_new.astype(state_hgrn.dtype))
    y_sample = _rms(hs, norm_final)

    return (y_prompt, y_sample,
            jnp.stack(pk_list), jnp.stack(pv_list), jnp.stack(ps_list),
            jnp.stack(pmk_list), jnp.stack(pmv_list),
            jnp.stack(sk_list), jnp.stack(sv_list), jnp.stack(ss_list))
```

```python
import functools
import math

import jax
import jax.numpy as jnp
from jax import lax
from jax.experimental import pallas as pl
from jax.experimental.pallas import tpu as pltpu

F32 = jnp.float32
BF16 = jnp.bfloat16

CHUNK = 64
HEAD_DIM = 128
DIFF_QK = HEAD_DIM // 2
N_MEM = 256
MEM_HEADS = 4
REL_BUCKETS = 32
REL_MAX_DIST = 128
EPS = 1e-6
LAM_INIT_L0 = 0.8 - 0.6 * math.exp(-0.3 * 0)
MASK_VALUE = -1e30

V7X_LANES = 128
V7X_SCOPED_VMEM_LIMIT_BYTES = 60000 * 1024
FF_PAD_MULTIPLE = 1024

ATTN_TILE = 512
SAMPLE_HEADS_PER_STEP = 4
SAMPLE_NEAR = 128


def _cparams(*sem):
    return pltpu.CompilerParams(dimension_semantics=sem,
                                vmem_limit_bytes=V7X_SCOPED_VMEM_LIMIT_BYTES)


def _rmsnorm_kernel(x_ref, g_ref, o_ref):
    x = x_ref[...]
    ms = jnp.mean(x * x, axis=-1, keepdims=True)
    o_ref[...] = (x * lax.rsqrt(ms + EPS) * g_ref[...]).astype(o_ref.dtype)


def _rmsnorm(x, g, out_dtype, rows=256):
    m, d = x.shape
    rows = min(rows, m)
    return pl.pallas_call(
        _rmsnorm_kernel,
        grid=(m // rows,),
        in_specs=[pl.BlockSpec((rows, d), lambda i: (i, 0)),
                  pl.BlockSpec((1, d), lambda i: (0, 0))],
        out_specs=pl.BlockSpec((rows, d), lambda i: (i, 0)),
        out_shape=jax.ShapeDtypeStruct((m, d), out_dtype),
        compiler_params=_cparams("parallel"),
        name="rmsnorm",
    )(x, g.reshape(1, d))


def _mm_kernel(*refs, n_pairs, has_res, n_out, nk, scale):
    xs = refs[0:2 * n_pairs:2]
    ws = refs[1:2 * n_pairs:2]
    pos = 2 * n_pairs
    res_ref = refs[pos] if has_res else None
    pos += int(has_res)
    outs = refs[pos:pos + n_out]
    acc_ref = refs[pos + n_out] if nk > 1 else None

    acc = None
    for x_ref, w_ref in zip(xs, ws):
        d = jnp.dot(x_ref[...], w_ref[...], preferred_element_type=F32)
        acc = d if acc is None else acc + d

    def finish(total):
        if scale != 1.0:
            total = total * scale
        if has_res:
            total = total + res_ref[...]
        for o_ref in outs:
            o_ref[...] = total.astype(o_ref.dtype)

    if nk == 1:
        finish(acc)
    else:
        k = pl.program_id(2)

        @pl.when(k == 0)
        def _():
            acc_ref[...] = acc

        @pl.when(jnp.logical_and(k > 0, k < nk - 1))
        def _():
            acc_ref[...] += acc

        @pl.when(k == nk - 1)
        def _():
            finish(acc_ref[...] + acc)


def _matmul(pairs, out_dtypes, *, residual=None, scale=1.0, bm=1024, bn=1024, tk=None, name="matmul"):
    m = pairs[0][0].shape[0]
    n = pairs[0][1].shape[1]
    kdim = pairs[0][0].shape[1]
    bm = min(bm, m)
    bn = min(bn, n)
    tk = kdim if tk is None else tk
    nk = kdim // tk
    assert m % bm == 0 and n % bn == 0 and kdim % tk == 0
    assert nk == 1 or len(pairs) == 1
    in_specs, args = [], []
    for x, w in pairs:
        kx = x.shape[1]
        bk = kx if nk == 1 else tk
        in_specs.append(pl.BlockSpec((bm, bk), lambda i, j, k: (i, k)))
        in_specs.append(pl.BlockSpec((bk, bn), lambda i, j, k: (k, j)))
        args += [x, w]
    if residual is not None:
        in_specs.append(pl.BlockSpec((bm, bn), lambda i, j, k: (i, j)))
        args.append(residual)
    out_specs = [pl.BlockSpec((bm, bn), lambda i, j, k: (i, j)) for _ in out_dtypes]
    out_shape = [jax.ShapeDtypeStruct((m, n), dt) for dt in out_dtypes]
    scratch = [pltpu.VMEM((bm, bn), F32)] if nk > 1 else []
    kern = functools.partial(_mm_kernel, n_pairs=len(pairs), has_res=residual is not None,
                             n_out=len(out_dtypes), nk=nk, scale=scale)
    outs = pl.pallas_call(
        kern,
        grid=(m // bm, n // bn, nk),
        in_specs=in_specs,
        out_specs=out_specs,
        out_shape=out_shape,
        scratch_shapes=scratch,
        compiler_params=_cparams("parallel", "parallel", "arbitrary"),
        name=name,
    )(*args)
    return outs


def _gateup_kernel(x_ref, wg_ref, wu_ref, o_ref):
    x = x_ref[...]
    g = jnp.dot(x, wg_ref[...], preferred_element_type=F32)
    u = jnp.dot(x, wu_ref[...], preferred_element_type=F32)
    o_ref[...] = (g * jax.nn.sigmoid(g) * u).astype(o_ref.dtype)


def _gateup(x, wg, wu, bm=1024, bn=512):
    m, kdim = x.shape
    n = wg.shape[1]
    return pl.pallas_call(
        _gateup_kernel,
        grid=(m // bm, n // bn),
        in_specs=[pl.BlockSpec((bm, kdim), lambda i, j: (i, 0)),
                  pl.BlockSpec((kdim, bn), lambda i, j: (0, j)),
                  pl.BlockSpec((kdim, bn), lambda i, j: (0, j))],
        out_specs=pl.BlockSpec((bm, bn), lambda i, j: (i, j)),
        out_shape=jax.ShapeDtypeStruct((m, n), BF16),
        compiler_params=_cparams("parallel", "parallel"),
        name="ffn_gate_up",
    )(x, wg, wu)


def _rel_bucket(rel):
    nb = REL_BUCKETS // 2
    max_exact = nb // 2
    ret = jnp.where(rel > 0, nb, 0)
    n = jnp.abs(rel)
    nf = jnp.maximum(n, 1).astype(F32)
    large = max_exact + (jnp.log(nf / max_exact) / math.log(REL_MAX_DIST / max_exact)
                         * (nb - max_exact)).astype(jnp.int32)
    large = jnp.minimum(large, nb - 1)
    return ret + jnp.where(n < max_exact, n, large)


def _bias_tile_kernel(far_ref, table_ref, bucket_ref, mask_ref, o_ref):
    h = pl.program_id(0)
    bucket = bucket_ref[...]
    acc = jnp.zeros(bucket.shape, F32)
    for b in range(REL_BUCKETS):
        acc = jnp.where(bucket == b, table_ref[b, h], acc)
    o_ref[0] = acc - table_ref[far_ref[0], h] + mask_ref[...]


def _bias_tiles(table, bucket, maskadd, far_bucket, n_heads):
    r, c = bucket.shape
    return pl.pallas_call(
        _bias_tile_kernel,
        grid=(n_heads,),
        in_specs=[pl.BlockSpec(memory_space=pltpu.SMEM),
                  pl.BlockSpec(memory_space=pltpu.SMEM),
                  pl.BlockSpec((r, c), lambda h: (0, 0)),
                  pl.BlockSpec((r, c), lambda h: (0, 0))],
        out_specs=pl.BlockSpec((1, r, c), lambda h: (h, 0, 0)),
        out_shape=jax.ShapeDtypeStruct((n_heads, r, c), F32),
        compiler_params=_cparams("arbitrary"),
        name="rel_bias_tiles",
    )(far_bucket.reshape(1).astype(jnp.int32), table.astype(F32), bucket.astype(jnp.int32), maskadd)


def _diff_lambda(lp):
    a = jnp.sum(lp[0:1, :] * lp[1:2, :], axis=-1, keepdims=True)
    b = jnp.sum(lp[2:3, :] * lp[3:4, :], axis=-1, keepdims=True)
    return jnp.exp(a) - jnp.exp(b) + LAM_INIT_L0


def _stack_maps(q):
    lane = lax.broadcasted_iota(jnp.int32, q.shape, 1)
    zero = jnp.zeros_like(q)
    return jnp.concatenate([jnp.where(lane < DIFF_QK, q, zero),
                            jnp.where(lane >= DIFF_QK, q, zero)], axis=0)


def _diff_epilogue(acc, l, lam, g, tq):
    o = acc * (1.0 / l)
    o = o[:tq] - lam * o[tq:]
    ms = jnp.mean(o * o, axis=-1, keepdims=True)
    return o * lax.rsqrt(ms + EPS) * g * (1.0 - LAM_INIT_L0)


def _attn_prompt_kernel(lam_ref, g_ref, q_ref, k_ref, v_ref, bias_ref, o_ref, m_ref, l_ref, acc_ref, *, tile):
    i = pl.program_id(1)
    qs = _stack_maps(q_ref[...])

    m_ref[...] = jnp.full(m_ref.shape, MASK_VALUE, F32)
    l_ref[...] = jnp.zeros(l_ref.shape, F32)
    acc_ref[...] = jnp.zeros(acc_ref.shape, F32)

    def update(j, bias):
        start = pl.multiple_of(j * tile, tile)
        kt = k_ref[pl.ds(start, tile), :]
        vt = v_ref[pl.ds(start, tile), :]
        s = lax.dot_general(qs, kt, (((1,), (1,)), ((), ())), preferred_element_type=F32)
        if bias is not None:
            s = (s.reshape(2, tile, tile) + bias[None]).reshape(2 * tile, tile)
        m_prev = m_ref[...]
        m_new = jnp.maximum(m_prev, jnp.max(s, axis=-1, keepdims=True))
        alpha = jnp.exp(m_prev - m_new)
        p = jnp.exp(s - m_new)
        l_ref[...] = alpha * l_ref[...] + jnp.sum(p, axis=-1, keepdims=True)
        acc_ref[...] = alpha * acc_ref[...] + jnp.dot(p.astype(BF16), vt, preferred_element_type=F32)
        m_ref[...] = m_new

    def far_body(j, carry):
        update(j, None)
        return carry

    lax.fori_loop(0, jnp.maximum(i - 1, 0), far_body, 0)

    @pl.when(i >= 1)
    def _():
        update(i - 1, bias_ref[0, :, :tile])

    update(i, bias_ref[0, :, tile:])

    y = _diff_epilogue(acc_ref[...], l_ref[...], _diff_lambda(lam_ref[...]), g_ref[...], tile)
    o_ref[...] = y.astype(o_ref.dtype)


def _attn_prompt(lam_p, g, q, k, v, bias, n_heads, tile=ATTN_TILE):
    t = q.shape[0]
    kern = functools.partial(_attn_prompt_kernel, tile=tile)
    return pl.pallas_call(
        kern,
        grid=(n_heads, t // tile),
        in_specs=[pl.BlockSpec((4, DIFF_QK), lambda h, i: (0, 0)),
                  pl.BlockSpec((1, HEAD_DIM), lambda h, i: (0, 0)),
                  pl.BlockSpec((tile, HEAD_DIM), lambda h, i: (i, h)),
                  pl.BlockSpec((t, HEAD_DIM), lambda h, i: (0, h)),
                  pl.BlockSpec((t, HEAD_DIM), lambda h, i: (0, h)),
                  pl.BlockSpec((1, tile, 2 * tile), lambda h, i: (h, 0, 0))],
        out_specs=pl.BlockSpec((tile, HEAD_DIM), lambda h, i: (i, h)),
        out_shape=jax.ShapeDtypeStruct((t, n_heads * HEAD_DIM), BF16),
        scratch_shapes=[pltpu.VMEM((2 * tile, 1), F32),
                        pltpu.VMEM((2 * tile, 1), F32),
                        pltpu.VMEM((2 * tile, HEAD_DIM), F32)],
        compiler_params=_cparams("parallel", "arbitrary"),
        name="diff_attn_prompt",
    )(lam_p, g, q, k, v, bias)


def _attn_sample_kernel(lam_ref, g_ref, q_ref, kn_ref, vn_ref, kc_ref, vc_ref, bias_ref, o_ref, *,
                        heads, past, ts):
    lam = _diff_lambda(lam_ref[...])
    g = g_ref[...]
    far = past - SAMPLE_NEAR
    nt = (((1,), (1,)), ((), ()))
    for h in range(heads):
        cols = slice(h * HEAD_DIM, (h + 1) * HEAD_DIM)
        qs = _stack_maps(q_ref[0, :, cols])
        kc = kc_ref[0, :, cols].astype(BF16)
        vc = vc_ref[0, :, cols].astype(BF16)
        kn = kn_ref[0, :, cols]
        vn = vn_ref[0, :, cols]
        bias = bias_ref[h]
        s_far = lax.dot_general(qs, kc[:far], nt, preferred_element_type=F32)
        s_near = lax.dot_general(qs, kc[far:], nt, preferred_element_type=F32)
        s_near = (s_near.reshape(2, ts, SAMPLE_NEAR) + bias[None, :, :SAMPLE_NEAR]).reshape(2 * ts, SAMPLE_NEAR)
        s_new = lax.dot_general(qs, kn, nt, preferred_element_type=F32)
        s_new = (s_new.reshape(2, ts, ts) + bias[None, :, SAMPLE_NEAR:]).reshape(2 * ts, ts)
        m = jnp.maximum(jnp.maximum(jnp.max(s_far, axis=-1, keepdims=True),
                                    jnp.max(s_near, axis=-1, keepdims=True)),
                        jnp.max(s_new, axis=-1, keepdims=True))
        p_far = jnp.exp(s_far - m)
        p_near = jnp.exp(s_near - m)
        p_new = jnp.exp(s_new - m)
        l = (jnp.sum(p_far, axis=-1, keepdims=True) + jnp.sum(p_near, axis=-1, keepdims=True)
             + jnp.sum(p_new, axis=-1, keepdims=True))
        acc = (jnp.dot(p_far.astype(BF16), vc[:far], preferred_element_type=F32)
               + jnp.dot(p_near.astype(BF16), vc[far:], preferred_element_type=F32)
               + jnp.dot(p_new.astype(BF16), vn, preferred_element_type=F32))
        o_ref[0, :, cols] = _diff_epilogue(acc, l, lam, g, ts).astype(o_ref.dtype)


def _attn_sample(lam_p, g, q, kn, vn, kc, vc, bias, n_heads):
    nb, ts, _ = q.shape
    past = kc.shape[1]
    hs = SAMPLE_HEADS_PER_STEP
    w = hs * HEAD_DIM
    kern = functools.partial(_attn_sample_kernel, heads=hs, past=past, ts=ts)
    blk = lambda rows: pl.BlockSpec((1, rows, w), lambda b, hg: (b, 0, hg))
    return pl.pallas_call(
        kern,
        grid=(nb, n_heads // hs),
        in_specs=[pl.BlockSpec((4, DIFF_QK), lambda b, hg: (0, 0)),
                  pl.BlockSpec((1, HEAD_DIM), lambda b, hg: (0, 0)),
                  blk(ts), blk(ts), blk(ts), blk(past), blk(past),
                  pl.BlockSpec((hs, ts, SAMPLE_NEAR + ts), lambda b, hg: (hg, 0, 0))],
        out_specs=blk(ts),
        out_shape=jax.ShapeDtypeStruct((nb, ts, n_heads * HEAD_DIM), BF16),
        compiler_params=_cparams("parallel", "arbitrary"),
        name="diff_attn_sample",
    )(lam_p, g, q, kn, vn, kc, vc, bias)


def _split3(x):
    hi = x.astype(BF16)
    r1 = x - hi.astype(F32)
    mid = r1.astype(BF16)
    lo = (r1 - mid.astype(F32)).astype(BF16)
    return hi, mid, lo


def _hgrn_kernel(lbl_ref, g_ref, hq_ref, hf_ref, hi_ref, hg_ref, s0_ref, o_ref, s_out_ref, st_ref, *, n_heads):
    c = pl.program_id(1)
    nc = pl.num_programs(1)
    L = hq_ref.shape[1]

    @pl.when(c == 0)
    def _():
        for h in range(n_heads):
            st_ref[h] = s0_ref[0, h].T

    lg = lbl_ref[...]
    e = jnp.exp(lg - jnp.max(lg, axis=0, keepdims=True))
    lb = e[0:1, :] / jnp.sum(e, axis=0, keepdims=True)

    z = hf_ref[0]
    sig = jax.nn.sigmoid(z)
    logf = jnp.log(lb + (1.0 - lb) * sig)
    kk = (1.0 - lb) * jax.nn.sigmoid(-z)
    hq = hq_ref[0]
    qq = hq * jax.nn.sigmoid(hq)

    row = lax.broadcasted_iota(jnp.int32, (L, L), 0)
    col = lax.broadcasted_iota(jnp.int32, (L, L), 1)
    causal = col <= row
    tri = jnp.where(causal, 1.0, 0.0).astype(BF16)
    b = None
    for part in _split3(logf):
        d = jnp.dot(tri, part, preferred_element_type=F32)
        b = d if b is None else b + d
    b_last = b[L - 1:L, :]
    qd = (qq * jnp.exp(b)).astype(BF16)
    kd = (kk * jnp.exp(-b)).astype(BF16)
    k_last = (kk * jnp.exp(b_last - b)).astype(BF16)
    decay = jnp.exp(b_last)
    v = hi_ref[0].astype(BF16)
    hg = hg_ref[0]
    gate = hg * jax.nn.sigmoid(hg)
    g = g_ref[...]

    nt = (((1,), (1,)), ((), ()))
    tn = (((0,), (0,)), ((), ()))
    for h in range(n_heads):
        cols = slice(h * HEAD_DIM, (h + 1) * HEAD_DIM)
        st = st_ref[h]
        a = lax.dot_general(qd[:, cols], kd[:, cols], nt, preferred_element_type=F32)
        a = jnp.where(causal, a, 0.0)
        o = (jnp.dot(a.astype(BF16), v[:, cols], preferred_element_type=F32)
             + lax.dot_general(qd[:, cols], st.astype(BF16), nt, preferred_element_type=F32))
        st_new = decay[:, cols] * st + lax.dot_general(v[:, cols], k_last[:, cols], tn,
                                                       preferred_element_type=F32)
        st_ref[h] = st_new
        ms = jnp.mean(o * o, axis=-1, keepdims=True)
        o_ref[0, :, cols] = (o * lax.rsqrt(ms + EPS) * g * gate[:, cols]).astype(o_ref.dtype)

    @pl.when(c == nc - 1)
    def _():
        for h in range(n_heads):
            s_out_ref[0, h] = st_ref[h].T


def _hgrn(lb_logits, g, hpart, s0, n_heads):
    nb, t, _ = hpart.shape
    d = n_heads * HEAD_DIM
    nc = t // CHUNK
    kern = functools.partial(_hgrn_kernel, n_heads=n_heads)
    part = lambda p: pl.BlockSpec((1, CHUNK, d), lambda b, c: (b, c, p))
    state = pl.BlockSpec((1, n_heads, HEAD_DIM, HEAD_DIM), lambda b, c: (b, 0, 0, 0))
    return pl.pallas_call(
        kern,
        grid=(nb, nc),
        in_specs=[pl.BlockSpec(lb_logits.shape, lambda b, c: (0, 0)),
                  pl.BlockSpec((1, HEAD_DIM), lambda b, c: (0, 0)),
                  part(0), part(1), part(2), part(3), state],
        out_specs=[pl.BlockSpec((1, CHUNK, d), lambda b, c: (b, c, 0)), state],
        out_shape=[jax.ShapeDtypeStruct((nb, t, d), BF16),
                   jax.ShapeDtypeStruct(s0.shape, F32)],
        scratch_shapes=[pltpu.VMEM((n_heads, HEAD_DIM, HEAD_DIM), F32)],
        compiler_params=_cparams("parallel", "arbitrary"),
        name="hgrn2_chunks",
    )(lb_logits, g, hpart, hpart, hpart, hpart, s0)


def _cross_kernel(q_ref, k_ref, v_ref, o_ref, *, heads, hd):
    nt = (((1,), (1,)), ((), ()))
    for h in range(heads):
        cols = slice(h * hd, (h + 1) * hd)
        q = q_ref[0, :, cols]
        k = k_ref[0, :, cols].astype(BF16)
        v = v_ref[0, :, cols].astype(BF16)
        s = lax.dot_general(q, k, nt, preferred_element_type=F32)
        p = jnp.exp(s - jnp.max(s, axis=-1, keepdims=True))
        l = jnp.sum(p, axis=-1, keepdims=True)
        o = jnp.dot(p.astype(BF16), v, preferred_element_type=F32) * (1.0 / l)
        o_ref[0, :, cols] = o.astype(o_ref.dtype)


def _cross_attn(q, mk, mv, tq):
    nb, t, d = q.shape
    kern = functools.partial(_cross_kernel, heads=MEM_HEADS, hd=d // MEM_HEADS)
    return pl.pallas_call(
        kern,
        grid=(nb, t // tq),
        in_specs=[pl.BlockSpec((1, tq, d), lambda b, i: (b, i, 0)),
                  pl.BlockSpec((1, N_MEM, d), lambda b, i: (b, 0, 0)),
                  pl.BlockSpec((1, N_MEM, d), lambda b, i: (b, 0, 0))],
        out_specs=pl.BlockSpec((1, tq, d), lambda b, i: (b, i, 0)),
        out_shape=jax.ShapeDtypeStruct((nb, t, d), BF16),
        compiler_params=_cparams("parallel", "parallel"),
        name="mem_cross_attn",
    )(q, mk, mv)


def _trunk(x, weights, mixer_fn, mem_kv_fn):
    nb, t, d = x.shape
    m = nb * t
    x2 = x.reshape(m, d)
    n = _rmsnorm(x2, weights["norm_mix"], BF16)
    n_dh = weights["w_dq"].shape[1]
    (dq,) = _matmul([(n, weights["w_dq"])], [BF16], scale=DIFF_QK ** -0.5, name="proj_dq")
    dk32, dk16 = _matmul([(n, weights["w_dk"])], [F32, BF16], name="proj_dk")
    dv32, dv16 = _matmul([(n, weights["w_dv"])], [F32, BF16], name="proj_dv")
    (hpart,) = _matmul([(n, weights["w_h"])], [F32], name="proj_hgrn")

    cat_diff, cat_hgrn, s_new = mixer_fn(dq, dk16, dv16, hpart)

    (h1,) = _matmul([(cat_diff, weights["w_out_diff"]), (cat_hgrn, weights["w_out_hgrn"])], [F32],
                    residual=x2, name="proj_out")

    n2 = _rmsnorm(h1, weights["norm_cross"], BF16)
    (cq,) = _matmul([(n2, weights["w_mem_q"])], [BF16], scale=(d // MEM_HEADS) ** -0.5, name="proj_mem_q")
    mk, mv = mem_kv_fn()
    co = _cross_attn(cq.reshape(nb, t, d), mk, mv, tq=min(t, 512))
    (h2,) = _matmul([(co.reshape(m, d), weights["w_mem_o"])], [F32], residual=h1, name="proj_mem_o")

    n3 = _rmsnorm(h2, weights["norm_ffn"], BF16)
    act = _gateup(n3, weights["w_gate"], weights["w_up"])
    (h3,) = _matmul([(act, weights["w_down"])], [F32], residual=h2, tk=act.shape[1] // 4, name="ffn_down")
    y = _rmsnorm(h3, weights["norm_final"], F32)
    return y.reshape(nb, t, d), dk32, dv32, s_new, n_dh


def kernel(x_prompt, x_sample, mem_prompt, cache_diff_k, cache_diff_v, state_hgrn, cache_mem_k, cache_mem_v, norm_mix, w_in, diff_lambda, diff_subln, rel_bias, hgrn_lb_logits, hgrn_norm, w_out, norm_cross, norm_mem, w_mem_q, w_mem_k, w_mem_v, w_mem_o, norm_ffn, w_ffn_gate, w_ffn_up, w_ffn_down, norm_final):
    bp, tp, d = x_prompt.shape
    bs, ts, _ = x_sample.shape
    depth = w_in.shape[0]
    assert depth == 1 and bp == 1
    past = cache_diff_k.shape[2]
    n_dh = cache_diff_k.shape[3]
    d_diff = n_dh * HEAD_DIM
    n_hh = state_hgrn.shape[2]
    d_hgrn = n_hh * HEAD_DIM
    d_ff = w_ffn_gate.shape[2]
    d_ffp = -(-d_ff // FF_PAD_MULTIPLE) * FF_PAD_MULTIPLE
    assert past % CHUNK == 0 and ts == CHUNK and tp % ATTN_TILE == 0

    wi = w_in[0]
    pad_c = ((0, 0), (0, d_ffp - d_ff))
    weights = {
        "norm_mix": norm_mix[0], "norm_cross": norm_cross[0], "norm_ffn": norm_ffn[0],
        "norm_final": norm_final,
        "w_dq": wi[:, :d_diff].astype(BF16),
        "w_dk": wi[:, d_diff:2 * d_diff].astype(BF16),
        "w_dv": wi[:, 2 * d_diff:3 * d_diff].astype(BF16),
        "w_h": wi[:, 3 * d_diff:].astype(BF16),
        "w_out_diff": w_out[0, :d_diff].astype(BF16),
        "w_out_hgrn": w_out[0, d_diff:].astype(BF16),
        "w_mem_q": w_mem_q[0].astype(BF16),
        "w_mem_o": w_mem_o[0].astype(BF16),
        "w_gate": jnp.pad(w_ffn_gate[0].astype(BF16), pad_c),
        "w_up": jnp.pad(w_ffn_up[0].astype(BF16), pad_c),
        "w_down": jnp.pad(w_ffn_down[0].astype(BF16), ((0, d_ffp - d_ff), (0, 0))),
    }
    lam_p = diff_lambda[0].astype(F32)
    subln = diff_subln[0].reshape(1, HEAD_DIM)
    hnorm = hgrn_norm[0].reshape(1, HEAD_DIM)

    tile = ATTN_TILE
    r = jnp.arange(tile)[:, None]
    c = jnp.arange(2 * tile)[None, :]
    rel_p = (c - tile) - r
    vis_p = jnp.logical_or(c < tile, (c - tile) // CHUNK <= r // CHUNK)
    bias_p = _bias_tiles(rel_bias, _rel_bucket(rel_p), jnp.where(vis_p, 0.0, MASK_VALUE).astype(F32),
                         _rel_bucket(jnp.int32(-2 * tile)), n_dh)
    qpos = past + jnp.arange(ts)[:, None]
    kpos = (past - SAMPLE_NEAR) + jnp.arange(SAMPLE_NEAR + ts)[None, :]
    vis_s = kpos // CHUNK <= qpos // CHUNK
    bias_s = _bias_tiles(rel_bias, _rel_bucket(kpos - qpos), jnp.where(vis_s, 0.0, MASK_VALUE).astype(F32),
                         _rel_bucket(jnp.int32(-(SAMPLE_NEAR + 1))), n_dh)

    def mixer_prompt(dq, dk16, dv16, hpart):
        cat_diff = _attn_prompt(lam_p, subln, dq, dk16, dv16, bias_p, n_dh)
        s0 = jnp.zeros((bp, n_hh, HEAD_DIM, HEAD_DIM), F32)
        cat_hgrn, s_new = _hgrn(hgrn_lb_logits, hnorm, hpart.reshape(bp, tp, 4 * d_hgrn), s0, n_hh)
        return cat_diff, cat_hgrn.reshape(bp * tp, d_hgrn), s_new

    mem_out = {}

    def mem_kv_prompt():
        mn = _rmsnorm(mem_prompt.reshape(bp * N_MEM, d), norm_mem[0], BF16)
        (mk,) = _matmul([(mn, w_mem_k[0].astype(BF16))], [F32], name="proj_mem_k")
        (mv,) = _matmul([(mn, w_mem_v[0].astype(BF16))], [F32], name="proj_mem_v")
        mem_out["k"], mem_out["v"] = mk, mv
        return mk.reshape(bp, N_MEM, d), mv.reshape(bp, N_MEM, d)

    y_p, dk_p, dv_p, s_p, _ = _trunk(x_prompt, weights, mixer_prompt, mem_kv_prompt)

    def mixer_sample(dq, dk16, dv16, hpart):
        shp = (bs, ts, d_diff)
        cat_diff = _attn_sample(lam_p, subln, dq.reshape(shp), dk16.reshape(shp), dv16.reshape(shp),
                                cache_diff_k[0].reshape(bs, past, d_diff),
                                cache_diff_v[0].reshape(bs, past, d_diff), bias_s, n_dh)
        cat_hgrn, s_new = _hgrn(hgrn_lb_logits, hnorm, hpart.reshape(bs, ts, 4 * d_hgrn), state_hgrn[0], n_hh)
        return cat_diff.reshape(bs * ts, d_diff), cat_hgrn.reshape(bs * ts, d_hgrn), s_new

    def mem_kv_sample():
        return cache_mem_k[0].reshape(bs, N_MEM, d), cache_mem_v[0].reshape(bs, N_MEM, d)

    y_s, dk_s, dv_s, s_s, _ = _trunk(x_sample, weights, mixer_sample, mem_kv_sample)

    mhd = d // MEM_HEADS
    return (y_p, y_s,
            dk_p.reshape(depth, bp, tp, n_dh, HEAD_DIM), dv_p.reshape(depth, bp, tp, n_dh, HEAD_DIM),
            s_p.reshape(depth, bp, n_hh, HEAD_DIM, HEAD_DIM),
            mem_out["k"].reshape(depth, bp, N_MEM, MEM_HEADS, mhd),
            mem_out["v"].reshape(depth, bp, N_MEM, MEM_HEADS, mhd),
            dk_s.reshape(depth, bs, ts, n_dh, HEAD_DIM), dv_s.reshape(depth, bs, ts, n_dh, HEAD_DIM),
            s_s.reshape(depth, bs, n_hh, HEAD_DIM, HEAD_DIM))
```

```python
import functools
import math

import jax
import jax.numpy as jnp
from jax import lax
from jax.experimental import pallas as pl
from jax.experimental.pallas import tpu as pltpu

F32 = jnp.float32
BF16 = jnp.bfloat16

CHUNK = 64
HEAD_DIM = 128
DIFF_QK = HEAD_DIM // 2
N_MEM = 256
MEM_HEADS = 4
REL_BUCKETS = 32
REL_MAX_DIST = 128
EPS = 1e-6
LAM_INIT_L0 = 0.8 - 0.6 * math.exp(-0.3 * 0)
MASK_VALUE = -1e30
LOG2E = math.log2(math.e)

V7X_LANES = 128
V7X_SCOPED_VMEM_LIMIT_BYTES = 60000 * 1024
FF_PAD_MULTIPLE = 1024

ATTN_TILE = 512
SAMPLE_CACHE_PARTS = 2
SAMPLE_HEADS_PER_ITER = 4
SAMPLE_NEAR = 128


def _cparams(*sem):
    return pltpu.CompilerParams(dimension_semantics=sem,
                                vmem_limit_bytes=V7X_SCOPED_VMEM_LIMIT_BYTES)


def _rmsnorm_kernel(x_ref, g_ref, o_ref):
    x = x_ref[...]
    ms = jnp.mean(x * x, axis=-1, keepdims=True)
    o_ref[...] = (x * lax.rsqrt(ms + EPS) * g_ref[...]).astype(o_ref.dtype)


def _rmsnorm(x, g, out_dtype, rows=256):
    m, d = x.shape
    rows = min(rows, m)
    return pl.pallas_call(
        _rmsnorm_kernel,
        grid=(m // rows,),
        in_specs=[pl.BlockSpec((rows, d), lambda i: (i, 0)),
                  pl.BlockSpec((1, d), lambda i: (0, 0))],
        out_specs=pl.BlockSpec((rows, d), lambda i: (i, 0)),
        out_shape=jax.ShapeDtypeStruct((m, d), out_dtype),
        compiler_params=_cparams("parallel"),
        name="rmsnorm",
    )(x, g.reshape(1, d))


def _mm_kernel(*refs, n_pairs, has_res, n_out, nk, scale):
    xs = refs[0:2 * n_pairs:2]
    ws = refs[1:2 * n_pairs:2]
    pos = 2 * n_pairs
    res_ref = refs[pos] if has_res else None
    pos += int(has_res)
    outs = refs[pos:pos + n_out]
    acc_ref = refs[pos + n_out] if nk > 1 else None

    acc = None
    for x_ref, w_ref in zip(xs, ws):
        d = jnp.dot(x_ref[...], w_ref[...], preferred_element_type=F32)
        acc = d if acc is None else acc + d

    def finish(total):
        if scale != 1.0:
            total = total * scale
        if has_res:
            total = total + res_ref[...]
        for o_ref in outs:
            o_ref[...] = total.astype(o_ref.dtype)

    if nk == 1:
        finish(acc)
    else:
        k = pl.program_id(2)

        @pl.when(k == 0)
        def _():
            acc_ref[...] = acc

        @pl.when(jnp.logical_and(k > 0, k < nk - 1))
        def _():
            acc_ref[...] += acc

        @pl.when(k == nk - 1)
        def _():
            finish(acc_ref[...] + acc)


def _matmul(pairs, out_dtypes, *, residual=None, scale=1.0, bm=1024, bn=1024, tk=None, name="matmul"):
    m = pairs[0][0].shape[0]
    n = pairs[0][1].shape[1]
    kdim = pairs[0][0].shape[1]
    bm = min(bm, m)
    bn = min(bn, n)
    tk = kdim if tk is None else tk
    nk = kdim // tk
    assert m % bm == 0 and n % bn == 0 and kdim % tk == 0
    assert nk == 1 or len(pairs) == 1
    in_specs, args = [], []
    for x, w in pairs:
        kx = x.shape[1]
        bk = kx if nk == 1 else tk
        in_specs.append(pl.BlockSpec((bm, bk), lambda i, j, k: (i, k)))
        in_specs.append(pl.BlockSpec((bk, bn), lambda i, j, k: (k, j)))
        args += [x, w]
    if residual is not None:
        in_specs.append(pl.BlockSpec((bm, bn), lambda i, j, k: (i, j)))
        args.append(residual)
    out_specs = [pl.BlockSpec((bm, bn), lambda i, j, k: (i, j)) for _ in out_dtypes]
    out_shape = [jax.ShapeDtypeStruct((m, n), dt) for dt in out_dtypes]
    scratch = [pltpu.VMEM((bm, bn), F32)] if nk > 1 else []
    kern = functools.partial(_mm_kernel, n_pairs=len(pairs), has_res=residual is not None,
                             n_out=len(out_dtypes), nk=nk, scale=scale)
    outs = pl.pallas_call(
        kern,
        grid=(m // bm, n // bn, nk),
        in_specs=in_specs,
        out_specs=out_specs,
        out_shape=out_shape,
        scratch_shapes=scratch,
        compiler_params=_cparams("parallel", "parallel", "arbitrary"),
        name=name,
    )(*args)
    return outs


def _gateup_kernel(x_ref, wg_ref, wu_ref, o_ref):
    x = x_ref[...]
    g = jnp.dot(x, wg_ref[...], preferred_element_type=F32)
    u = jnp.dot(x, wu_ref[...], preferred_element_type=F32)
    o_ref[...] = (g * jax.nn.sigmoid(g) * u).astype(o_ref.dtype)


def _gateup(x, wg, wu, bm=1024, bn=512):
    m, kdim = x.shape
    n = wg.shape[1]
    return pl.pallas_call(
        _gateup_kernel,
        grid=(m // bm, n // bn),
        in_specs=[pl.BlockSpec((bm, kdim), lambda i, j: (i, 0)),
                  pl.BlockSpec((kdim, bn), lambda i, j: (0, j)),
                  pl.BlockSpec((kdim, bn), lambda i, j: (0, j))],
        out_specs=pl.BlockSpec((bm, bn), lambda i, j: (i, j)),
        out_shape=jax.ShapeDtypeStruct((m, n), BF16),
        compiler_params=_cparams("parallel", "parallel"),
        name="ffn_gate_up",
    )(x, wg, wu)


def _rel_bucket(rel):
    nb = REL_BUCKETS // 2
    max_exact = nb // 2
    ret = jnp.where(rel > 0, nb, 0)
    n = jnp.abs(rel)
    nf = jnp.maximum(n, 1).astype(F32)
    large = max_exact + (jnp.log(nf / max_exact) / math.log(REL_MAX_DIST / max_exact)
                         * (nb - max_exact)).astype(jnp.int32)
    large = jnp.minimum(large, nb - 1)
    return ret + jnp.where(n < max_exact, n, large)


def _bias_tile_kernel(far_ref, table_ref, bucket_ref, mask_ref, o_ref):
    h = pl.program_id(0)
    bucket = bucket_ref[...]
    acc = jnp.zeros(bucket.shape, F32)
    for b in range(REL_BUCKETS):
        acc = jnp.where(bucket == b, table_ref[b, h], acc)
    o_ref[0] = (acc - table_ref[far_ref[0], h]) * LOG2E + mask_ref[...]


def _bias_tiles(table, bucket, maskadd, far_bucket, n_heads):
    r, c = bucket.shape
    return pl.pallas_call(
        _bias_tile_kernel,
        grid=(n_heads,),
        in_specs=[pl.BlockSpec(memory_space=pltpu.SMEM),
                  pl.BlockSpec(memory_space=pltpu.SMEM),
                  pl.BlockSpec((r, c), lambda h: (0, 0)),
                  pl.BlockSpec((r, c), lambda h: (0, 0))],
        out_specs=pl.BlockSpec((1, r, c), lambda h: (h, 0, 0)),
        out_shape=jax.ShapeDtypeStruct((n_heads, r, c), F32),
        compiler_params=_cparams("arbitrary"),
        name="rel_bias_tiles",
    )(far_bucket.reshape(1).astype(jnp.int32), table.astype(F32), bucket.astype(jnp.int32), maskadd)


def _diff_lambda(lp):
    a = jnp.sum(lp[0:1, :] * lp[1:2, :], axis=-1, keepdims=True)
    b = jnp.sum(lp[2:3, :] * lp[3:4, :], axis=-1, keepdims=True)
    return jnp.exp(a) - jnp.exp(b) + LAM_INIT_L0


def _stack_maps(q):
    lane = lax.broadcasted_iota(jnp.int32, q.shape, 1)
    zero = jnp.zeros_like(q)
    return jnp.concatenate([jnp.where(lane < DIFF_QK, q, zero),
                            jnp.where(lane >= DIFF_QK, q, zero)], axis=0)


def _diff_epilogue(acc, l, lam, g, tq):
    o = acc * (1.0 / l)
    o = o[:tq] - lam * o[tq:]
    ms = jnp.mean(o * o, axis=-1, keepdims=True)
    return o * lax.rsqrt(ms + EPS) * g * (1.0 - LAM_INIT_L0)


def _attn_prompt_kernel(lam_ref, g_ref, q_ref, k_ref, v_ref, bias_ref, o_ref, qs_ref, s_ref, m_ref, acc_ref, *,
                        tile):
    i = pl.program_id(1)
    n_lane_tiles = tile // V7X_LANES
    ones = jnp.ones((tile, V7X_LANES), BF16)
    nt = (((1,), (1,)), ((), ()))

    qs_ref[...] = _stack_maps(q_ref[...])
    m_ref[...] = jnp.full(m_ref.shape, MASK_VALUE, F32)
    acc_ref[...] = jnp.zeros(acc_ref.shape, F32)

    def key_rows(d):
        return pl.ds(pl.multiple_of((i - d) * tile, tile), tile)

    def scores(d, slot):
        s_ref[slot] = lax.dot_general(qs_ref[...], k_ref[key_rows(d), :], nt, preferred_element_type=F32)

    def absorb(d, slot, bias):
        vt = jnp.concatenate([v_ref[key_rows(d), :], ones], axis=1)
        s = s_ref[slot]
        if bias is not None:
            s = (s.reshape(2, tile, tile) + bias[None]).reshape(2 * tile, tile)
        cols = [s[:, c * V7X_LANES:(c + 1) * V7X_LANES] for c in range(n_lane_tiles)]
        mx = cols[0]
        for sc in cols[1:]:
            mx = jnp.maximum(mx, sc)
        m_prev = m_ref[...]
        m_new = jnp.maximum(m_prev, jnp.max(mx, axis=-1, keepdims=True))
        alpha = jnp.exp2(m_prev - m_new)
        p = jnp.concatenate([jnp.exp2(sc - m_new).astype(BF16) for sc in cols], axis=1)
        pv = jnp.dot(p, vt, preferred_element_type=F32)
        acc_ref[...] = jnp.concatenate([alpha, alpha], axis=1) * acc_ref[...] + pv
        m_ref[...] = m_new

    i_even = lax.rem(i, 2) == 0

    @pl.when(i_even)
    def _():
        scores(i, 0)

    @pl.when(jnp.logical_not(i_even))
    def _():
        scores(i, 1)

    @pl.when(jnp.logical_and(i_even, i >= 2))
    def _():
        scores(i - 1, 1)
        absorb(i, 0, None)

    d_odd = i - 1 + lax.rem(i, 2)

    def pair_body(u, carry):
        d = d_odd - 2 * u
        scores(d - 1, 0)
        absorb(d, 1, None)
        scores(d - 2, 1)
        absorb(d - 1, 0, None)
        return carry

    lax.fori_loop(0, jnp.where(d_odd >= 3, (d_odd - 1) // 2, 0), pair_body, 0)

    @pl.when(i >= 1)
    def _():
        scores(0, 0)
        absorb(1, 1, bias_ref[0, :, :tile])

    absorb(0, 0, bias_ref[0, :, tile:])

    acc = acc_ref[...]
    y = _diff_epilogue(acc[:, :HEAD_DIM], acc[:, HEAD_DIM:], _diff_lambda(lam_ref[...]), g_ref[...], tile)
    o_ref[...] = y.astype(o_ref.dtype)


def _attn_prompt(lam_p, g, q, k, v, bias, n_heads, tile=ATTN_TILE):
    t = q.shape[0]
    kern = functools.partial(_attn_prompt_kernel, tile=tile)
    return pl.pallas_call(
        kern,
        grid=(n_heads, t // tile),
        in_specs=[pl.BlockSpec((4, DIFF_QK), lambda h, i: (0, 0)),
                  pl.BlockSpec((1, HEAD_DIM), lambda h, i: (0, 0)),
                  pl.BlockSpec((tile, HEAD_DIM), lambda h, i: (i, h)),
                  pl.BlockSpec((t, HEAD_DIM), lambda h, i: (0, h)),
                  pl.BlockSpec((t, HEAD_DIM), lambda h, i: (0, h)),
                  pl.BlockSpec((1, tile, 2 * tile), lambda h, i: (h, 0, 0))],
        out_specs=pl.BlockSpec((tile, HEAD_DIM), lambda h, i: (i, h)),
        out_shape=jax.ShapeDtypeStruct((t, n_heads * HEAD_DIM), BF16),
        scratch_shapes=[pltpu.VMEM((2 * tile, HEAD_DIM), BF16),
                        pltpu.VMEM((2, 2 * tile, tile), F32),
                        pltpu.VMEM((2 * tile, V7X_LANES), F32),
                        pltpu.VMEM((2 * tile, 2 * HEAD_DIM), F32)],
        compiler_params=_cparams("parallel", "arbitrary"),
        name="diff_attn_prompt",
    )(lam_p, g, q, k, v, bias)


def _attn_sample_kernel(lam_ref, g_ref, q_ref, kn_ref, vn_ref, kc_ref, vc_ref, bias_ref, o_ref, m_ref, acc_ref, *,
                        n_heads, keys, ts):
    part = pl.program_id(1)
    is_last = part == pl.num_programs(1) - 1
    lam = _diff_lambda(lam_ref[...])
    g = g_ref[...]
    n_lane_tiles = keys // V7X_LANES
    ones_c = jnp.ones((keys, V7X_LANES), BF16)
    ones_n = jnp.ones((ts, V7X_LANES), BF16)
    nt = (((1,), (1,)), ((), ()))

    @pl.when(part == 0)
    def _():
        m_ref[...] = jnp.full(m_ref.shape, MASK_VALUE, F32)
        acc_ref[...] = jnp.zeros(acc_ref.shape, F32)

    def one_head(h):
        col = pl.ds(pl.multiple_of(h * HEAD_DIM, HEAD_DIM), HEAD_DIM)
        rows = pl.ds(h, keys, stride=n_heads)
        qs = _stack_maps(q_ref[0, :, col])
        kc = kc_ref[0, rows, :].astype(BF16)
        vc = jnp.concatenate([vc_ref[0, rows, :].astype(BF16), ones_c], axis=1)
        vn = jnp.concatenate([vn_ref[0, :, col], ones_n], axis=1)
        bias = bias_ref[h]
        near_bias = jnp.where(is_last, bias[:, :SAMPLE_NEAR], 0.0)
        new_bias = jnp.where(is_last, bias[:, SAMPLE_NEAR:], MASK_VALUE)

        s = lax.dot_general(qs, kc, nt, preferred_element_type=F32)
        cols = [s[:, c * V7X_LANES:(c + 1) * V7X_LANES] for c in range(n_lane_tiles)]
        cols[-1] = (cols[-1].reshape(2, ts, SAMPLE_NEAR) + near_bias[None]).reshape(2 * ts, SAMPLE_NEAR)
        s_new = lax.dot_general(qs, kn_ref[0, :, col], nt, preferred_element_type=F32)
        s_new = (s_new.reshape(2, ts, ts) + new_bias[None]).reshape(2 * ts, ts)

        mx = cols[0]
        for sc in cols[1:]:
            mx = jnp.maximum(mx, sc)
        m_prev = m_ref[h]
        m_new = jnp.maximum(m_prev, jnp.maximum(jnp.max(mx, axis=-1, keepdims=True),
                                                jnp.max(s_new, axis=-1, keepdims=True)))
        alpha = jnp.exp2(m_prev - m_new)
        p = jnp.concatenate([jnp.exp2(sc - m_new).astype(BF16) for sc in cols], axis=1)
        p_new = jnp.exp2(s_new - m_new[:, :ts]).astype(BF16)
        pv = (jnp.dot(p, vc, preferred_element_type=F32)
              + jnp.dot(p_new, vn, preferred_element_type=F32))
        acc = jnp.concatenate([alpha, alpha], axis=1) * acc_ref[h] + pv
        acc_ref[h] = acc
        m_ref[h] = m_new

    def head_group(hg, carry):
        heads = [hg * SAMPLE_HEADS_PER_ITER + u for u in range(SAMPLE_HEADS_PER_ITER)]
        for h in heads:
            one_head(h)

        @pl.when(is_last)
        def _():
            for h in heads:
                acc = acc_ref[h]
                y = _diff_epilogue(acc[:, :HEAD_DIM], acc[:, HEAD_DIM:], lam, g, ts)
                o_ref[0, :, pl.ds(pl.multiple_of(h * HEAD_DIM, HEAD_DIM), HEAD_DIM)] = y.astype(o_ref.dtype)

        return carry

    lax.fori_loop(0, n_heads // SAMPLE_HEADS_PER_ITER, head_group, 0)


def _attn_sample(lam_p, g, q, kn, vn, kc, vc, bias, n_heads):
    nb, ts, d = q.shape
    past = kc.shape[1] // n_heads
    parts = SAMPLE_CACHE_PARTS
    keys = past // parts
    assert past % parts == 0 and keys % V7X_LANES == 0 and SAMPLE_NEAR == V7X_LANES
    kern = functools.partial(_attn_sample_kernel, n_heads=n_heads, keys=keys, ts=ts)
    new = pl.BlockSpec((1, ts, d), lambda b, c: (b, 0, 0))
    cache = pl.BlockSpec((1, keys * n_heads, HEAD_DIM), lambda b, c: (b, c, 0))
    return pl.pallas_call(
        kern,
        grid=(nb, parts),
        in_specs=[pl.BlockSpec((4, DIFF_QK), lambda b, c: (0, 0)),
                  pl.BlockSpec((1, HEAD_DIM), lambda b, c: (0, 0)),
                  new, new, new, cache, cache,
                  pl.BlockSpec((n_heads, ts, SAMPLE_NEAR + ts), lambda b, c: (0, 0, 0))],
        out_specs=new,
        out_shape=jax.ShapeDtypeStruct((nb, ts, d), BF16),
        scratch_shapes=[pltpu.VMEM((n_heads, 2 * ts, V7X_LANES), F32),
                        pltpu.VMEM((n_heads, 2 * ts, 2 * HEAD_DIM), F32)],
        compiler_params=_cparams("parallel", "arbitrary"),
        name="diff_attn_sample",
    )(lam_p, g, q, kn, vn, kc, vc, bias)


def _split3(x):
    hi = x.astype(BF16)
    r1 = x - hi.astype(F32)
    mid = r1.astype(BF16)
    lo = (r1 - mid.astype(F32)).astype(BF16)
    return hi, mid, lo


def _hgrn_kernel(lbl_ref, g_ref, hq_ref, hf_ref, hi_ref, hg_ref, s0_ref, o_ref, s_out_ref, st_ref, *, n_heads):
    c = pl.program_id(1)
    nc = pl.num_programs(1)
    L = hq_ref.shape[1]

    @pl.when(c == 0)
    def _():
        for h in range(n_heads):
            st_ref[h] = s0_ref[0, h].T

    lg = lbl_ref[...]
    e = jnp.exp(lg - jnp.max(lg, axis=0, keepdims=True))
    lb = e[0:1, :] / jnp.sum(e, axis=0, keepdims=True)

    z = hf_ref[0]
    sig = jax.nn.sigmoid(z)
    logf = jnp.log(lb + (1.0 - lb) * sig)
    kk = (1.0 - lb) * jax.nn.sigmoid(-z)
    hq = hq_ref[0]
    qq = hq * jax.nn.sigmoid(hq)

    row = lax.broadcasted_iota(jnp.int32, (L, L), 0)
    col = lax.broadcasted_iota(jnp.int32, (L, L), 1)
    causal = col <= row
    tri = jnp.where(causal, 1.0, 0.0).astype(BF16)
    b = None
    for part in _split3(logf):
        d = jnp.dot(tri, part, preferred_element_type=F32)
        b = d if b is None else b + d
    b_last = b[L - 1:L, :]
    qd = (qq * jnp.exp(b)).astype(BF16)
    kd = (kk * jnp.exp(-b)).astype(BF16)
    k_last = (kk * jnp.exp(b_last - b)).astype(BF16)
    decay = jnp.exp(b_last)
    v = hi_ref[0].astype(BF16)
    hg = hg_ref[0]
    gate = hg * jax.nn.sigmoid(hg)
    g = g_ref[...]

    nt = (((1,), (1,)), ((), ()))
    tn = (((0,), (0,)), ((), ()))
    for h in range(n_heads):
        cols = slice(h * HEAD_DIM, (h + 1) * HEAD_DIM)
        st = st_ref[h]
        a = lax.dot_general(qd[:, cols], kd[:, cols], nt, preferred_element_type=F32)
        a = jnp.where(causal, a, 0.0)
        o = (jnp.dot(a.astype(BF16), v[:, cols], preferred_element_type=F32)
             + lax.dot_general(qd[:, cols], st.astype(BF16), nt, preferred_element_type=F32))
        st_new = decay[:, cols] * st + lax.dot_general(v[:, cols], k_last[:, cols], tn,
                                                       preferred_element_type=F32)
        st_ref[h] = st_new
        ms = jnp.mean(o * o, axis=-1, keepdims=True)
        o_ref[0, :, cols] = (o * lax.rsqrt(ms + EPS) * g * gate[:, cols]).astype(o_ref.dtype)

    @pl.when(c == nc - 1)
    def _():
        for h in range(n_heads):
            s_out_ref[0, h] = st_ref[h].T


def _hgrn(lb_logits, g, hpart, s0, n_heads):
    nb, t, _ = hpart.shape
    d = n_heads * HEAD_DIM
    nc = t // CHUNK
    kern = functools.partial(_hgrn_kernel, n_heads=n_heads)
    part = lambda p: pl.BlockSpec((1, CHUNK, d), lambda b, c: (b, c, p))
    state = pl.BlockSpec((1, n_heads, HEAD_DIM, HEAD_DIM), lambda b, c: (b, 0, 0, 0))
    return pl.pallas_call(
        kern,
        grid=(nb, nc),
        in_specs=[pl.BlockSpec(lb_logits.shape, lambda b, c: (0, 0)),
                  pl.BlockSpec((1, HEAD_DIM), lambda b, c: (0, 0)),
                  part(0), part(1), part(2), part(3), state],
        out_specs=[pl.BlockSpec((1, CHUNK, d), lambda b, c: (b, c, 0)), state],
        out_shape=[jax.ShapeDtypeStruct((nb, t, d), BF16),
                   jax.ShapeDtypeStruct(s0.shape, F32)],
        scratch_shapes=[pltpu.VMEM((n_heads, HEAD_DIM, HEAD_DIM), F32)],
        compiler_params=_cparams("parallel", "arbitrary"),
        name="hgrn2_chunks",
    )(lb_logits, g, hpart, hpart, hpart, hpart, s0)


def _cross_kernel(q_ref, k_ref, v_ref, o_ref, *, heads, hd):
    nt = (((1,), (1,)), ((), ()))
    for h in range(heads):
        cols = slice(h * hd, (h + 1) * hd)
        q = q_ref[0, :, cols]
        k = k_ref[0, :, cols].astype(BF16)
        v = v_ref[0, :, cols].astype(BF16)
        s = lax.dot_general(q, k, nt, preferred_element_type=F32)
        p = jnp.exp(s - jnp.max(s, axis=-1, keepdims=True))
        l = jnp.sum(p, axis=-1, keepdims=True)
        o = jnp.dot(p.astype(BF16), v, preferred_element_type=F32) * (1.0 / l)
        o_ref[0, :, cols] = o.astype(o_ref.dtype)


def _cross_attn(q, mk, mv, tq):
    nb, t, d = q.shape
    kern = functools.partial(_cross_kernel, heads=MEM_HEADS, hd=d // MEM_HEADS)
    return pl.pallas_call(
        kern,
        grid=(nb, t // tq),
        in_specs=[pl.BlockSpec((1, tq, d), lambda b, i: (b, i, 0)),
                  pl.BlockSpec((1, N_MEM, d), lambda b, i: (b, 0, 0)),
                  pl.BlockSpec((1, N_MEM, d), lambda b, i: (b, 0, 0))],
        out_specs=pl.BlockSpec((1, tq, d), lambda b, i: (b, i, 0)),
        out_shape=jax.ShapeDtypeStruct((nb, t, d), BF16),
        compiler_params=_cparams("parallel", "parallel"),
        name="mem_cross_attn",
    )(q, mk, mv)


def _trunk(x, weights, mixer_fn, mem_kv_fn):
    nb, t, d = x.shape
    m = nb * t
    x2 = x.reshape(m, d)
    n = _rmsnorm(x2, weights["norm_mix"], BF16)
    n_dh = weights["w_dq"].shape[1]
    (dq,) = _matmul([(n, weights["w_dq"])], [BF16], scale=DIFF_QK ** -0.5 * LOG2E, name="proj_dq")
    dk32, dk16 = _matmul([(n, weights["w_dk"])], [F32, BF16], name="proj_dk")
    dv32, dv16 = _matmul([(n, weights["w_dv"])], [F32, BF16], name="proj_dv")
    (hpart,) = _matmul([(n, weights["w_h"])], [F32], name="proj_hgrn")

    cat_diff, cat_hgrn, s_new = mixer_fn(dq, dk16, dv16, hpart)

    (h1,) = _matmul([(cat_diff, weights["w_out_diff"]), (cat_hgrn, weights["w_out_hgrn"])], [F32],
                    residual=x2, name="proj_out")

    n2 = _rmsnorm(h1, weights["norm_cross"], BF16)
    (cq,) = _matmul([(n2, weights["w_mem_q"])], [BF16], scale=(d // MEM_HEADS) ** -0.5, name="proj_mem_q")
    mk, mv = mem_kv_fn()
    co = _cross_attn(cq.reshape(nb, t, d), mk, mv, tq=min(t, 512))
    (h2,) = _matmul([(co.reshape(m, d), weights["w_mem_o"])], [F32], residual=h1, name="proj_mem_o")

    n3 = _rmsnorm(h2, weights["norm_ffn"], BF16)
    act = _gateup(n3, weights["w_gate"], weights["w_up"])
    (h3,) = _matmul([(act, weights["w_down"])], [F32], residual=h2, tk=act.shape[1] // 4, name="ffn_down")
    y = _rmsnorm(h3, weights["norm_final"], F32)
    return y.reshape(nb, t, d), dk32, dv32, s_new, n_dh


def kernel(x_prompt, x_sample, mem_prompt, cache_diff_k, cache_diff_v, state_hgrn, cache_mem_k, cache_mem_v, norm_mix, w_in, diff_lambda, diff_subln, rel_bias, hgrn_lb_logits, hgrn_norm, w_out, norm_cross, norm_mem, w_mem_q, w_mem_k, w_mem_v, w_mem_o, norm_ffn, w_ffn_gate, w_ffn_up, w_ffn_down, norm_final):
    bp, tp, d = x_prompt.shape
    bs, ts, _ = x_sample.shape
    depth = w_in.shape[0]
    assert depth == 1 and bp == 1
    past = cache_diff_k.shape[2]
    n_dh = cache_diff_k.shape[3]
    d_diff = n_dh * HEAD_DIM
    n_hh = state_hgrn.shape[2]
    d_hgrn = n_hh * HEAD_DIM
    d_ff = w_ffn_gate.shape[2]
    d_ffp = -(-d_ff // FF_PAD_MULTIPLE) * FF_PAD_MULTIPLE
    assert past % CHUNK == 0 and ts == CHUNK and tp % ATTN_TILE == 0

    wi = w_in[0]
    pad_c = ((0, 0), (0, d_ffp - d_ff))
    weights = {
        "norm_mix": norm_mix[0], "norm_cross": norm_cross[0], "norm_ffn": norm_ffn[0],
        "norm_final": norm_final,
        "w_dq": wi[:, :d_diff].astype(BF16),
        "w_dk": wi[:, d_diff:2 * d_diff].astype(BF16),
        "w_dv": wi[:, 2 * d_diff:3 * d_diff].astype(BF16),
        "w_h": wi[:, 3 * d_diff:].astype(BF16),
        "w_out_diff": w_out[0, :d_diff].astype(BF16),
        "w_out_hgrn": w_out[0, d_diff:].astype(BF16),
        "w_mem_q": w_mem_q[0].astype(BF16),
        "w_mem_o": w_mem_o[0].astype(BF16),
        "w_gate": jnp.pad(w_ffn_gate[0].astype(BF16), pad_c),
        "w_up": jnp.pad(w_ffn_up[0].astype(BF16), pad_c),
        "w_down": jnp.pad(w_ffn_down[0].astype(BF16), ((0, d_ffp - d_ff), (0, 0))),
    }
    lam_p = diff_lambda[0].astype(F32)
    subln = diff_subln[0].reshape(1, HEAD_DIM)
    hnorm = hgrn_norm[0].reshape(1, HEAD_DIM)

    tile = ATTN_TILE
    r = jnp.arange(tile)[:, None]
    c = jnp.arange(2 * tile)[None, :]
    rel_p = (c - tile) - r
    vis_p = jnp.logical_or(c < tile, (c - tile) // CHUNK <= r // CHUNK)
    bias_p = _bias_tiles(rel_bias, _rel_bucket(rel_p), jnp.where(vis_p, 0.0, MASK_VALUE).astype(F32),
                         _rel_bucket(jnp.int32(-2 * tile)), n_dh)
    qpos = past + jnp.arange(ts)[:, None]
    kpos = (past - SAMPLE_NEAR) + jnp.arange(SAMPLE_NEAR + ts)[None, :]
    vis_s = kpos // CHUNK <= qpos // CHUNK
    bias_s = _bias_tiles(rel_bias, _rel_bucket(kpos - qpos), jnp.where(vis_s, 0.0, MASK_VALUE).astype(F32),
                         _rel_bucket(jnp.int32(-(SAMPLE_NEAR + 1))), n_dh)

    def mixer_prompt(dq, dk16, dv16, hpart):
        cat_diff = _attn_prompt(lam_p, subln, dq, dk16, dv16, bias_p, n_dh)
        s0 = jnp.zeros((bp, n_hh, HEAD_DIM, HEAD_DIM), F32)
        cat_hgrn, s_new = _hgrn(hgrn_lb_logits, hnorm, hpart.reshape(bp, tp, 4 * d_hgrn), s0, n_hh)
        return cat_diff, cat_hgrn.reshape(bp * tp, d_hgrn), s_new

    mem_out = {}

    def mem_kv_prompt():
        mn = _rmsnorm(mem_prompt.reshape(bp * N_MEM, d), norm_mem[0], BF16)
        (mk,) = _matmul([(mn, w_mem_k[0].astype(BF16))], [F32], name="proj_mem_k")
        (mv,) = _matmul([(mn, w_mem_v[0].astype(BF16))], [F32], name="proj_mem_v")
        mem_out["k"], mem_out["v"] = mk, mv
        return mk.reshape(bp, N_MEM, d), mv.reshape(bp, N_MEM, d)

    y_p, dk_p, dv_p, s_p, _ = _trunk(x_prompt, weights, mixer_prompt, mem_kv_prompt)

    def mixer_sample(dq, dk16, dv16, hpart):
        shp = (bs, ts, d_diff)
        cat_diff = _attn_sample(lam_p, subln, dq.reshape(shp), dk16.reshape(shp), dv16.reshape(shp),
                                cache_diff_k[0].reshape(bs, past * n_dh, HEAD_DIM),
                                cache_diff_v[0].reshape(bs, past * n_dh, HEAD_DIM), bias_s, n_dh)
        cat_hgrn, s_new = _hgrn(hgrn_lb_logits, hnorm, hpart.reshape(bs, ts, 4 * d_hgrn), state_hgrn[0], n_hh)
        return cat_diff.reshape(bs * ts, d_diff), cat_hgrn.reshape(bs * ts, d_hgrn), s_new

    def mem_kv_sample():
        return cache_mem_k[0].reshape(bs, N_MEM, d), cache_mem_v[0].reshape(bs, N_MEM, d)

    y_s, dk_s, dv_s, s_s, _ = _trunk(x_sample, weights, mixer_sample, mem_kv_sample)

    mhd = d // MEM_HEADS
    return (y_p, y_s,
            dk_p.reshape(depth, bp, tp, n_dh, HEAD_DIM), dv_p.reshape(depth, bp, tp, n_dh, HEAD_DIM),
            s_p.reshape(depth, bp, n_hh, HEAD_DIM, HEAD_DIM),
            mem_out["k"].reshape(depth, bp, N_MEM, MEM_HEADS, mhd),
            mem_out["v"].reshape(depth, bp, N_MEM, MEM_HEADS, mhd),
            dk_s.reshape(depth, bs, ts, n_dh, HEAD_DIM), dv_s.reshape(depth, bs, ts, n_dh, HEAD_DIM),
            s_s.reshape(depth, bs, n_hh, HEAD_DIM, HEAD_DIM))
```

```python
import functools
import math

import jax
import jax.numpy as jnp
from jax import lax
from jax.experimental import pallas as pl
from jax.experimental.pallas import tpu as pltpu

F32 = jnp.float32
BF16 = jnp.bfloat16

CHUNK = 64
HEAD_DIM = 128
DIFF_QK = HEAD_DIM // 2
N_MEM = 256
MEM_HEADS = 4
REL_BUCKETS = 32
REL_MAX_DIST = 128
EPS = 1e-6
LAM_INIT_L0 = 0.8 - 0.6 * math.exp(-0.3 * 0)
MASK_VALUE = -1e30
LOG2E = math.log2(math.e)

V7X_LANES = 128
V7X_SCOPED_VMEM_LIMIT_BYTES = 60000 * 1024
ATTN_TILE = 512
SAMPLE_CACHE_PARTS = 2
SAMPLE_HEADS_PER_ITER = 4
SAMPLE_NEAR = 128


def _cparams(*sem):
    return pltpu.CompilerParams(dimension_semantics=sem,
                                vmem_limit_bytes=V7X_SCOPED_VMEM_LIMIT_BYTES)


def _rmsnorm_kernel(x_ref, g_ref, o_ref):
    x = x_ref[...]
    ms = jnp.mean(x * x, axis=-1, keepdims=True)
    o_ref[...] = (x * lax.rsqrt(ms + EPS) * g_ref[...]).astype(o_ref.dtype)


def _rmsnorm(x, g, out_dtype, rows=256):
    m, d = x.shape
    rows = min(rows, m)
    return pl.pallas_call(
        _rmsnorm_kernel,
        grid=(m // rows,),
        in_specs=[pl.BlockSpec((rows, d), lambda i: (i, 0)),
                  pl.BlockSpec((1, d), lambda i: (0, 0))],
        out_specs=pl.BlockSpec((rows, d), lambda i: (i, 0)),
        out_shape=jax.ShapeDtypeStruct((m, d), out_dtype),
        compiler_params=_cparams("parallel"),
        name="rmsnorm",
    )(x, g.reshape(1, d))


def _mm_kernel(*refs, n_pairs, has_res, scale):
    xs = refs[0:2 * n_pairs:2]
    ws = refs[1:2 * n_pairs:2]
    pos = 2 * n_pairs
    res_ref = refs[pos] if has_res else None
    outs = refs[pos + int(has_res):]

    total = None
    for x_ref, w_ref in zip(xs, ws):
        d = jnp.dot(x_ref[...], w_ref[...].astype(BF16), preferred_element_type=F32)
        total = d if total is None else total + d
    if scale != 1.0:
        total = total * scale
    if has_res:
        total = total + res_ref[...]
    for o_ref in outs:
        o_ref[...] = total.astype(o_ref.dtype)


def _matmul(pairs, n, out_dtypes, *, residual=None, scale=1.0, bm=1024, bn=512, name="matmul"):
    m = pairs[0][0].shape[0]
    bm = min(bm, m)
    bn = min(bn, n)
    assert m % bm == 0 and n % bn == 0
    in_specs, args = [], []
    for x, w, r0, c0 in pairs:
        kx = x.shape[1]
        assert r0 % kx == 0 and c0 % bn == 0
        in_specs.append(pl.BlockSpec((bm, kx), lambda j, i: (i, 0)))
        in_specs.append(pl.BlockSpec((kx, bn), lambda j, i, rb=r0 // kx, cb=c0 // bn: (rb, cb + j)))
        args += [x, w]
    if residual is not None:
        in_specs.append(pl.BlockSpec((bm, bn), lambda j, i: (i, j)))
        args.append(residual)
    kern = functools.partial(_mm_kernel, n_pairs=len(pairs), has_res=residual is not None, scale=scale)
    return pl.pallas_call(
        kern,
        grid=(n // bn, m // bm),
        in_specs=in_specs,
        out_specs=[pl.BlockSpec((bm, bn), lambda j, i: (i, j)) for _ in out_dtypes],
        out_shape=[jax.ShapeDtypeStruct((m, n), dt) for dt in out_dtypes],
        compiler_params=_cparams("parallel", "parallel"),
        name=name,
    )(*args)


def _gateup_kernel(x_ref, wg_ref, wu_ref, o_ref):
    x = x_ref[...]
    g = jnp.dot(x, wg_ref[...].astype(BF16), preferred_element_type=F32)
    u = jnp.dot(x, wu_ref[...].astype(BF16), preferred_element_type=F32)
    o_ref[...] = (g * jax.nn.sigmoid(g) * u).astype(o_ref.dtype)


def _gateup(x, wg, wu, bm=2048, bn=256):
    m, kdim = x.shape
    n = wg.shape[1]
    bm = min(bm, m)
    assert m % bm == 0 and n % bn == 0
    return pl.pallas_call(
        _gateup_kernel,
        grid=(n // bn, m // bm),
        in_specs=[pl.BlockSpec((bm, kdim), lambda j, i: (i, 0)),
                  pl.BlockSpec((kdim, bn), lambda j, i: (0, j)),
                  pl.BlockSpec((kdim, bn), lambda j, i: (0, j))],
        out_specs=pl.BlockSpec((bm, bn), lambda j, i: (i, j)),
        out_shape=jax.ShapeDtypeStruct((m, n), BF16),
        compiler_params=_cparams("parallel", "parallel"),
        name="ffn_gate_up",
    )(x, wg, wu)


def _rel_bucket(rel):
    nb = REL_BUCKETS // 2
    max_exact = nb // 2
    ret = jnp.where(rel > 0, nb, 0)
    n = jnp.abs(rel)
    nf = jnp.maximum(n, 1).astype(F32)
    large = max_exact + (jnp.log(nf / max_exact) / math.log(REL_MAX_DIST / max_exact)
                         * (nb - max_exact)).astype(jnp.int32)
    large = jnp.minimum(large, nb - 1)
    return ret + jnp.where(n < max_exact, n, large)


def _bias_tile_kernel(far_ref, table_ref, bucket_ref, mask_ref, o_ref):
    h = pl.program_id(0)
    bucket = bucket_ref[...]
    acc = jnp.zeros(bucket.shape, F32)
    for b in range(REL_BUCKETS):
        acc = jnp.where(bucket == b, table_ref[b, h], acc)
    o_ref[0] = (acc - table_ref[far_ref[0], h]) * LOG2E + mask_ref[...]


def _bias_tiles(table, bucket, maskadd, far_bucket, n_heads):
    r, c = bucket.shape
    return pl.pallas_call(
        _bias_tile_kernel,
        grid=(n_heads,),
        in_specs=[pl.BlockSpec(memory_space=pltpu.SMEM),
                  pl.BlockSpec(memory_space=pltpu.SMEM),
                  pl.BlockSpec((r, c), lambda h: (0, 0)),
                  pl.BlockSpec((r, c), lambda h: (0, 0))],
        out_specs=pl.BlockSpec((1, r, c), lambda h: (h, 0, 0)),
        out_shape=jax.ShapeDtypeStruct((n_heads, r, c), F32),
        compiler_params=_cparams("arbitrary"),
        name="rel_bias_tiles",
    )(far_bucket.reshape(1).astype(jnp.int32), table.astype(F32), bucket.astype(jnp.int32), maskadd)


def _diff_lambda(lp):
    a = jnp.sum(lp[0:1, :] * lp[1:2, :], axis=-1, keepdims=True)
    b = jnp.sum(lp[2:3, :] * lp[3:4, :], axis=-1, keepdims=True)
    return jnp.exp(a) - jnp.exp(b) + LAM_INIT_L0


def _stack_maps(q):
    lane = lax.broadcasted_iota(jnp.int32, q.shape, 1)
    zero = jnp.zeros_like(q)
    return jnp.concatenate([jnp.where(lane < DIFF_QK, q, zero),
                            jnp.where(lane >= DIFF_QK, q, zero)], axis=0)


def _diff_epilogue(acc, l, lam, g, tq):
    o = acc * (1.0 / l)
    o = o[:tq] - lam * o[tq:]
    ms = jnp.mean(o * o, axis=-1, keepdims=True)
    return o * lax.rsqrt(ms + EPS) * g * (1.0 - LAM_INIT_L0)


def _attn_prompt_kernel(lam_ref, g_ref, q_ref, k_ref, v_ref, bias_ref, o_ref, qs_ref, s_ref, m_ref, acc_ref, *,
                        tile):
    i = pl.program_id(1)
    n_lane_tiles = tile // V7X_LANES
    ones = jnp.ones((tile, V7X_LANES), BF16)
    nt = (((1,), (1,)), ((), ()))

    qs_ref[...] = _stack_maps(q_ref[...])
    m_ref[...] = jnp.full(m_ref.shape, MASK_VALUE, F32)
    acc_ref[...] = jnp.zeros(acc_ref.shape, F32)

    def key_rows(d):
        return pl.ds(pl.multiple_of((i - d) * tile, tile), tile)

    def scores(d, slot):
        s_ref[slot] = lax.dot_general(qs_ref[...], k_ref[key_rows(d), :], nt, preferred_element_type=F32)

    def absorb(d, slot, bias):
        vt = jnp.concatenate([v_ref[key_rows(d), :], ones], axis=1)
        s = s_ref[slot]
        if bias is not None:
            s = (s.reshape(2, tile, tile) + bias[None]).reshape(2 * tile, tile)
        cols = [s[:, c * V7X_LANES:(c + 1) * V7X_LANES] for c in range(n_lane_tiles)]
        mx = cols[0]
        for sc in cols[1:]:
            mx = jnp.maximum(mx, sc)
        m_prev = m_ref[...]
        m_new = jnp.maximum(m_prev, jnp.max(mx, axis=-1, keepdims=True))
        alpha = jnp.exp2(m_prev - m_new)
        p = jnp.concatenate([jnp.exp2(sc - m_new).astype(BF16) for sc in cols], axis=1)
        pv = jnp.dot(p, vt, preferred_element_type=F32)
        acc_ref[...] = jnp.concatenate([alpha, alpha], axis=1) * acc_ref[...] + pv
        m_ref[...] = m_new

    i_even = lax.rem(i, 2) == 0

    @pl.when(i_even)
    def _():
        scores(i, 0)

    @pl.when(jnp.logical_not(i_even))
    def _():
        scores(i, 1)

    @pl.when(jnp.logical_and(i_even, i >= 2))
    def _():
        scores(i - 1, 1)
        absorb(i, 0, None)

    d_odd = i - 1 + lax.rem(i, 2)

    def pair_body(u, carry):
        d = d_odd - 2 * u
        scores(d - 1, 0)
        absorb(d, 1, None)
        scores(d - 2, 1)
        absorb(d - 1, 0, None)
        return carry

    lax.fori_loop(0, jnp.where(d_odd >= 3, (d_odd - 1) // 2, 0), pair_body, 0)

    @pl.when(i >= 1)
    def _():
        scores(0, 0)
        absorb(1, 1, bias_ref[0, :, :tile])

    absorb(0, 0, bias_ref[0, :, tile:])

    acc = acc_ref[...]
    y = _diff_epilogue(acc[:, :HEAD_DIM], acc[:, HEAD_DIM:], _diff_lambda(lam_ref[...]), g_ref[...], tile)
    o_ref[...] = y.astype(o_ref.dtype)


def _attn_prompt(lam_p, g, q, k, v, bias, n_heads, tile=ATTN_TILE):
    t = q.shape[0]
    kern = functools.partial(_attn_prompt_kernel, tile=tile)
    return pl.pallas_call(
        kern,
        grid=(n_heads, t // tile),
        in_specs=[pl.BlockSpec((4, DIFF_QK), lambda h, i: (0, 0)),
                  pl.BlockSpec((1, HEAD_DIM), lambda h, i: (0, 0)),
                  pl.BlockSpec((tile, HEAD_DIM), lambda h, i: (i, h)),
                  pl.BlockSpec((t, HEAD_DIM), lambda h, i: (0, h)),
                  pl.BlockSpec((t, HEAD_DIM), lambda h, i: (0, h)),
                  pl.BlockSpec((1, tile, 2 * tile), lambda h, i: (h, 0, 0))],
        out_specs=pl.BlockSpec((tile, HEAD_DIM), lambda h, i: (i, h)),
        out_shape=jax.ShapeDtypeStruct((t, n_heads * HEAD_DIM), BF16),
        scratch_shapes=[pltpu.VMEM((2 * tile, HEAD_DIM), BF16),
                        pltpu.VMEM((2, 2 * tile, tile), F32),
                        pltpu.VMEM((2 * tile, V7X_LANES), F32),
                        pltpu.VMEM((2 * tile, 2 * HEAD_DIM), F32)],
        compiler_params=_cparams("parallel", "arbitrary"),
        name="diff_attn_prompt",
    )(lam_p, g, q, k, v, bias)


def _attn_sample_kernel(lam_ref, g_ref, q_ref, kn_ref, vn_ref, kc_ref, vc_ref, bias_ref, o_ref, m_ref, acc_ref, *,
                        n_heads, keys, ts):
    part = pl.program_id(1)
    is_last = part == pl.num_programs(1) - 1
    lam = _diff_lambda(lam_ref[...])
    g = g_ref[...]
    n_lane_tiles = keys // V7X_LANES
    ones_c = jnp.ones((keys, V7X_LANES), BF16)
    ones_n = jnp.ones((ts, V7X_LANES), BF16)
    nt = (((1,), (1,)), ((), ()))

    @pl.when(part == 0)
    def _():
        m_ref[...] = jnp.full(m_ref.shape, MASK_VALUE, F32)
        acc_ref[...] = jnp.zeros(acc_ref.shape, F32)

    def one_head(h):
        col = pl.ds(pl.multiple_of(h * HEAD_DIM, HEAD_DIM), HEAD_DIM)
        rows = pl.ds(h, keys, stride=n_heads)
        qs = _stack_maps(q_ref[0, :, col])
        kc = kc_ref[0, rows, :].astype(BF16)
        vc = jnp.concatenate([vc_ref[0, rows, :].astype(BF16), ones_c], axis=1)
        vn = jnp.concatenate([vn_ref[0, :, col], ones_n], axis=1)
        bias = bias_ref[h]
        near_bias = jnp.where(is_last, bias[:, :SAMPLE_NEAR], 0.0)
        new_bias = jnp.where(is_last, bias[:, SAMPLE_NEAR:], MASK_VALUE)

        s = lax.dot_general(qs, kc, nt, preferred_element_type=F32)
        cols = [s[:, c * V7X_LANES:(c + 1) * V7X_LANES] for c in range(n_lane_tiles)]
        cols[-1] = (cols[-1].reshape(2, ts, SAMPLE_NEAR) + near_bias[None]).reshape(2 * ts, SAMPLE_NEAR)
        s_new = lax.dot_general(qs, kn_ref[0, :, col], nt, preferred_element_type=F32)
        s_new = (s_new.reshape(2, ts, ts) + new_bias[None]).reshape(2 * ts, ts)

        mx = cols[0]
        for sc in cols[1:]:
            mx = jnp.maximum(mx, sc)
        m_prev = m_ref[h]
        m_new = jnp.maximum(m_prev, jnp.maximum(jnp.max(mx, axis=-1, keepdims=True),
                                                jnp.max(s_new, axis=-1, keepdims=True)))
        alpha = jnp.exp2(m_prev - m_new)
        p = jnp.concatenate([jnp.exp2(sc - m_new).astype(BF16) for sc in cols], axis=1)
        p_new = jnp.exp2(s_new - m_new[:, :ts]).astype(BF16)
        pv = (jnp.dot(p, vc, preferred_element_type=F32)
              + jnp.dot(p_new, vn, preferred_element_type=F32))
        acc = jnp.concatenate([alpha, alpha], axis=1) * acc_ref[h] + pv
        acc_ref[h] = acc
        m_ref[h] = m_new

    def head_group(hg, carry):
        heads = [hg * SAMPLE_HEADS_PER_ITER + u for u in range(SAMPLE_HEADS_PER_ITER)]
        for h in heads:
            one_head(h)

        @pl.when(is_last)
        def _():
            for h in heads:
                acc = acc_ref[h]
                y = _diff_epilogue(acc[:, :HEAD_DIM], acc[:, HEAD_DIM:], lam, g, ts)
                o_ref[0, :, pl.ds(pl.multiple_of(h * HEAD_DIM, HEAD_DIM), HEAD_DIM)] = y.astype(o_ref.dtype)

        return carry

    lax.fori_loop(0, n_heads // SAMPLE_HEADS_PER_ITER, head_group, 0)


def _attn_sample(lam_p, g, q, kn, vn, kc, vc, bias, n_heads):
    nb, ts, d = q.shape
    past = kc.shape[1] // n_heads
    parts = SAMPLE_CACHE_PARTS
    keys = past // parts
    assert past % parts == 0 and keys % V7X_LANES == 0 and SAMPLE_NEAR == V7X_LANES
    kern = functools.partial(_attn_sample_kernel, n_heads=n_heads, keys=keys, ts=ts)
    new = pl.BlockSpec((1, ts, d), lambda b, c: (b, 0, 0))
    cache = pl.BlockSpec((1, keys * n_heads, HEAD_DIM), lambda b, c: (b, c, 0))
    return pl.pallas_call(
        kern,
        grid=(nb, parts),
        in_specs=[pl.BlockSpec((4, DIFF_QK), lambda b, c: (0, 0)),
                  pl.BlockSpec((1, HEAD_DIM), lambda b, c: (0, 0)),
                  new, new, new, cache, cache,
                  pl.BlockSpec((n_heads, ts, SAMPLE_NEAR + ts), lambda b, c: (0, 0, 0))],
        out_specs=new,
        out_shape=jax.ShapeDtypeStruct((nb, ts, d), BF16),
        scratch_shapes=[pltpu.VMEM((n_heads, 2 * ts, V7X_LANES), F32),
                        pltpu.VMEM((n_heads, 2 * ts, 2 * HEAD_DIM), F32)],
        compiler_params=_cparams("parallel", "arbitrary"),
        name="diff_attn_sample",
    )(lam_p, g, q, kn, vn, kc, vc, bias)


def _split3(x):
    hi = x.astype(BF16)
    r1 = x - hi.astype(F32)
    mid = r1.astype(BF16)
    lo = (r1 - mid.astype(F32)).astype(BF16)
    return hi, mid, lo


def _hgrn_kernel(lbl_ref, g_ref, hq_ref, hf_ref, hi_ref, hg_ref, s0_ref, o_ref, s_out_ref, st_ref, *, n_heads):
    c = pl.program_id(1)
    nc = pl.num_programs(1)
    L = hq_ref.shape[1]

    @pl.when(c == 0)
    def _():
        for h in range(n_heads):
            st_ref[h] = s0_ref[0, h].T

    lg = lbl_ref[...]
    e = jnp.exp(lg - jnp.max(lg, axis=0, keepdims=True))
    lb = e[0:1, :] / jnp.sum(e, axis=0, keepdims=True)

    z = hf_ref[0]
    sig = jax.nn.sigmoid(z)
    logf = jnp.log(lb + (1.0 - lb) * sig)
    kk = (1.0 - lb) * jax.nn.sigmoid(-z)
    hq = hq_ref[0]
    qq = hq * jax.nn.sigmoid(hq)

    row = lax.broadcasted_iota(jnp.int32, (L, L), 0)
    col = lax.broadcasted_iota(jnp.int32, (L, L), 1)
    causal = col <= row
    tri = jnp.where(causal, 1.0, 0.0).astype(BF16)
    b = None
    for part in _split3(logf):
        d = jnp.dot(tri, part, preferred_element_type=F32)
        b = d if b is None else b + d
    b_last = b[L - 1:L, :]
    qd = (qq * jnp.exp(b)).astype(BF16)
    kd = (kk * jnp.exp(-b)).astype(BF16)
    k_last = (kk * jnp.exp(b_last - b)).astype(BF16)
    decay = jnp.exp(b_last)
    v = hi_ref[0].astype(BF16)
    hg = hg_ref[0]
    gate = hg * jax.nn.sigmoid(hg)
    g = g_ref[...]

    nt = (((1,), (1,)), ((), ()))
    tn = (((0,), (0,)), ((), ()))
    for h in range(n_heads):
        cols = slice(h * HEAD_DIM, (h + 1) * HEAD_DIM)
        st = st_ref[h]
        a = lax.dot_general(qd[:, cols], kd[:, cols], nt, preferred_element_type=F32)
        a = jnp.where(causal, a, 0.0)
        o = (jnp.dot(a.astype(BF16), v[:, cols], preferred_element_type=F32)
             + lax.dot_general(qd[:, cols], st.astype(BF16), nt, preferred_element_type=F32))
        st_new = decay[:, cols] * st + lax.dot_general(v[:, cols], k_last[:, cols], tn,
                                                       preferred_element_type=F32)
        st_ref[h] = st_new
        ms = jnp.mean(o * o, axis=-1, keepdims=True)
        o_ref[0, :, cols] = (o * lax.rsqrt(ms + EPS) * g * gate[:, cols]).astype(o_ref.dtype)

    @pl.when(c == nc - 1)
    def _():
        for h in range(n_heads):
            s_out_ref[0, h] = st_ref[h].T


def _hgrn(lb_logits, g, hpart, s0, n_heads):
    nb, t, _ = hpart.shape
    d = n_heads * HEAD_DIM
    nc = t // CHUNK
    kern = functools.partial(_hgrn_kernel, n_heads=n_heads)
    part = lambda p: pl.BlockSpec((1, CHUNK, d), lambda b, c: (b, c, p))
    state = pl.BlockSpec((1, n_heads, HEAD_DIM, HEAD_DIM), lambda b, c: (b, 0, 0, 0))
    return pl.pallas_call(
        kern,
        grid=(nb, nc),
        in_specs=[pl.BlockSpec(lb_logits.shape, lambda b, c: (0, 0)),
                  pl.BlockSpec((1, HEAD_DIM), lambda b, c: (0, 0)),
                  part(0), part(1), part(2), part(3), state],
        out_specs=[pl.BlockSpec((1, CHUNK, d), lambda b, c: (b, c, 0)), state],
        out_shape=[jax.ShapeDtypeStruct((nb, t, d), BF16),
                   jax.ShapeDtypeStruct(s0.shape, F32)],
        scratch_shapes=[pltpu.VMEM((n_heads, HEAD_DIM, HEAD_DIM), F32)],
        compiler_params=_cparams("parallel", "arbitrary"),
        name="hgrn2_chunks",
    )(lb_logits, g, hpart, hpart, hpart, hpart, s0)


def _cross_kernel(q_ref, k_ref, v_ref, o_ref, *, heads, hd, split_heads):
    nt = (((1,), (1,)), ((), ()))
    if split_heads:
        tq = q_ref.shape[1]
        n_mem = k_ref.shape[1]
        xk = k_ref[0].reshape(n_mem * heads, hd).astype(BF16)
        xv = v_ref[0].reshape(n_mem * heads, hd).astype(BF16)
        q = jnp.concatenate([q_ref[0, :, h * hd:(h + 1) * hd] for h in range(heads)], axis=0)
        s = lax.dot_general(q, xk, nt, preferred_element_type=F32)
        row_head = lax.broadcasted_iota(jnp.int32, s.shape, 0) // tq
        col_head = lax.rem(lax.broadcasted_iota(jnp.int32, s.shape, 1), heads)
        s = jnp.where(row_head == col_head, s, MASK_VALUE)
        p = jnp.exp(s - jnp.max(s, axis=-1, keepdims=True))
        l = jnp.sum(p, axis=-1, keepdims=True)
        o = jnp.dot(p.astype(BF16), xv, preferred_element_type=F32) * (1.0 / l)
        for h in range(heads):
            o_ref[0, :, h * hd:(h + 1) * hd] = o[h * tq:(h + 1) * tq].astype(o_ref.dtype)
        return
    for h in range(heads):
        cols = slice(h * hd, (h + 1) * hd)
        q = q_ref[0, :, cols]
        k = k_ref[0, :, cols].astype(BF16)
        v = v_ref[0, :, cols].astype(BF16)
        s = lax.dot_general(q, k, nt, preferred_element_type=F32)
        p = jnp.exp(s - jnp.max(s, axis=-1, keepdims=True))
        l = jnp.sum(p, axis=-1, keepdims=True)
        o = jnp.dot(p.astype(BF16), v, preferred_element_type=F32) * (1.0 / l)
        o_ref[0, :, cols] = o.astype(o_ref.dtype)


def _cross_attn(q, mk, mv, tq):
    nb, t, d = q.shape
    split_heads = mk.ndim == 4
    kern = functools.partial(_cross_kernel, heads=MEM_HEADS, hd=d // MEM_HEADS, split_heads=split_heads)
    if split_heads:
        mem_spec = pl.BlockSpec((1,) + mk.shape[1:], lambda b, i: (b, 0, 0, 0))
    else:
        mem_spec = pl.BlockSpec((1, N_MEM, d), lambda b, i: (b, 0, 0))
    return pl.pallas_call(
        kern,
        grid=(nb, t // tq),
        in_specs=[pl.BlockSpec((1, tq, d), lambda b, i: (b, i, 0)),
                  mem_spec,
                  mem_spec],
        out_specs=pl.BlockSpec((1, tq, d), lambda b, i: (b, i, 0)),
        out_shape=jax.ShapeDtypeStruct((nb, t, d), BF16),
        compiler_params=_cparams("parallel", "parallel"),
        name="mem_cross_attn",
    )(q, mk, mv)


def _trunk(x, weights, mixer_fn, mem_kv_fn):
    nb, t, d = x.shape
    m = nb * t
    x2 = x.reshape(m, d)
    n = _rmsnorm(x2, weights["norm_mix"], BF16)
    w_in, d_diff = weights["w_in"], weights["d_diff"]
    (dq,) = _matmul([(n, w_in, 0, 0)], d_diff, [BF16], scale=DIFF_QK ** -0.5 * LOG2E, name="proj_dq")
    dk32, dk16 = _matmul([(n, w_in, 0, d_diff)], d_diff, [F32, BF16], name="proj_dk")
    dv32, dv16 = _matmul([(n, w_in, 0, 2 * d_diff)], d_diff, [F32, BF16], name="proj_dv")
    (hpart,) = _matmul([(n, w_in, 0, 3 * d_diff)], w_in.shape[1] - 3 * d_diff, [F32], name="proj_hgrn")

    cat_diff, cat_hgrn, s_new = mixer_fn(dq, dk16, dv16, hpart)

    w_out = weights["w_out"]
    (h1,) = _matmul([(cat_diff, w_out, 0, 0), (cat_hgrn, w_out, d_diff, 0)], d, [F32],
                    residual=x2, name="proj_out")

    n2 = _rmsnorm(h1, weights["norm_cross"], BF16)
    (cq,) = _matmul([(n2, weights["w_mem_q"], 0, 0)], d, [BF16], scale=(d // MEM_HEADS) ** -0.5,
                    name="proj_mem_q")
    mk, mv = mem_kv_fn()
    co = _cross_attn(cq.reshape(nb, t, d), mk, mv, tq=min(t, 512))
    (h2,) = _matmul([(co.reshape(m, d), weights["w_mem_o"], 0, 0)], d, [F32], residual=h1, name="proj_mem_o")

    n3 = _rmsnorm(h2, weights["norm_ffn"], BF16)
    act = _gateup(n3, weights["w_gate"], weights["w_up"])
    (h3,) = _matmul([(act, weights["w_down"], 0, 0)], d, [F32], residual=h2, bm=512, name="ffn_down")
    y = _rmsnorm(h3, weights["norm_final"], F32)
    return y.reshape(nb, t, d), dk32, dv32, s_new


def kernel(x_prompt, x_sample, mem_prompt, cache_diff_k, cache_diff_v, state_hgrn, cache_mem_k, cache_mem_v, norm_mix, w_in, diff_lambda, diff_subln, rel_bias, hgrn_lb_logits, hgrn_norm, w_out, norm_cross, norm_mem, w_mem_q, w_mem_k, w_mem_v, w_mem_o, norm_ffn, w_ffn_gate, w_ffn_up, w_ffn_down, norm_final):
    bp, tp, d = x_prompt.shape
    bs, ts, _ = x_sample.shape
    depth = w_in.shape[0]
    assert depth == 1 and bp == 1
    past = cache_diff_k.shape[2]
    n_dh = cache_diff_k.shape[3]
    d_diff = n_dh * HEAD_DIM
    n_hh = state_hgrn.shape[2]
    d_hgrn = n_hh * HEAD_DIM
    assert past % CHUNK == 0 and ts == CHUNK and tp % ATTN_TILE == 0

    weights = {
        "norm_mix": norm_mix[0], "norm_cross": norm_cross[0], "norm_ffn": norm_ffn[0],
        "norm_final": norm_final, "d_diff": d_diff,
        "w_in": w_in[0], "w_out": w_out[0], "w_mem_q": w_mem_q[0], "w_mem_o": w_mem_o[0],
        "w_gate": w_ffn_gate[0], "w_up": w_ffn_up[0],
        "w_down": w_ffn_down[0].astype(BF16),
    }
    lam_p = diff_lambda[0].astype(F32)
    subln = diff_subln[0].reshape(1, HEAD_DIM)
    hnorm = hgrn_norm[0].reshape(1, HEAD_DIM)

    tile = ATTN_TILE
    r = jnp.arange(tile)[:, None]
    c = jnp.arange(2 * tile)[None, :]
    rel_p = (c - tile) - r
    vis_p = jnp.logical_or(c < tile, (c - tile) // CHUNK <= r // CHUNK)
    bias_p = _bias_tiles(rel_bias, _rel_bucket(rel_p), jnp.where(vis_p, 0.0, MASK_VALUE).astype(F32),
                         _rel_bucket(jnp.int32(-2 * tile)), n_dh)
    qpos = past + jnp.arange(ts)[:, None]
    kpos = (past - SAMPLE_NEAR) + jnp.arange(SAMPLE_NEAR + ts)[None, :]
    vis_s = kpos // CHUNK <= qpos // CHUNK
    bias_s = _bias_tiles(rel_bias, _rel_bucket(kpos - qpos), jnp.where(vis_s, 0.0, MASK_VALUE).astype(F32),
                         _rel_bucket(jnp.int32(-(SAMPLE_NEAR + 1))), n_dh)

    def mixer_prompt(dq, dk16, dv16, hpart):
        cat_diff = _attn_prompt(lam_p, subln, dq, dk16, dv16, bias_p, n_dh)
        s0 = jnp.zeros((bp, n_hh, HEAD_DIM, HEAD_DIM), F32)
        cat_hgrn, s_new = _hgrn(hgrn_lb_logits, hnorm, hpart.reshape(bp, tp, 4 * d_hgrn), s0, n_hh)
        return cat_diff, cat_hgrn.reshape(bp * tp, d_hgrn), s_new

    mem_out = {}

    def mem_kv_prompt():
        mn = _rmsnorm(mem_prompt.reshape(bp * N_MEM, d), norm_mem[0], BF16)
        (mk,) = _matmul([(mn, w_mem_k[0], 0, 0)], d, [F32], name="proj_mem_k")
        (mv,) = _matmul([(mn, w_mem_v[0], 0, 0)], d, [F32], name="proj_mem_v")
        mem_out["k"], mem_out["v"] = mk, mv
        return mk.reshape(bp, N_MEM, d), mv.reshape(bp, N_MEM, d)

    y_p, dk_p, dv_p, s_p = _trunk(x_prompt, weights, mixer_prompt, mem_kv_prompt)

    def mixer_sample(dq, dk16, dv16, hpart):
        shp = (bs, ts, d_diff)
        cat_diff = _attn_sample(lam_p, subln, dq.reshape(shp), dk16.reshape(shp), dv16.reshape(shp),
                                cache_diff_k[0].reshape(bs, past * n_dh, HEAD_DIM),
                                cache_diff_v[0].reshape(bs, past * n_dh, HEAD_DIM), bias_s, n_dh)
        cat_hgrn, s_new = _hgrn(hgrn_lb_logits, hnorm, hpart.reshape(bs, ts, 4 * d_hgrn), state_hgrn[0], n_hh)
        return cat_diff.reshape(bs * ts, d_diff), cat_hgrn.reshape(bs * ts, d_hgrn), s_new

    def mem_kv_sample():
        return cache_mem_k[0], cache_mem_v[0]

    y_s, dk_s, dv_s, s_s = _trunk(x_sample, weights, mixer_sample, mem_kv_sample)

    mhd = d // MEM_HEADS
    return (y_p, y_s,
            dk_p.reshape(depth, bp, tp, n_dh, HEAD_DIM), dv_p.reshape(depth, bp, tp, n_dh, HEAD_DIM),
            s_p.reshape(depth, bp, n_hh, HEAD_DIM, HEAD_DIM),
            mem_out["k"].reshape(depth, bp, N_MEM, MEM_HEADS, mhd),
            mem_out["v"].reshape(depth, bp, N_MEM, MEM_HEADS, mhd),
            dk_s.reshape(depth, bs, ts, n_dh, HEAD_DIM), dv_s.reshape(depth, bs, ts, n_dh, HEAD_DIM),
            s_s.reshape(depth, bs, n_hh, HEAD_DIM, HEAD_DIM))
```

```python
import functools
import math

import jax
import jax.numpy as jnp
from jax import lax
from jax.experimental import pallas as pl
from jax.experimental.pallas import tpu as pltpu

F32 = jnp.float32
BF16 = jnp.bfloat16

CHUNK = 64
HEAD_DIM = 128
DIFF_QK = HEAD_DIM // 2
N_MEM = 256
MEM_HEADS = 4
REL_BUCKETS = 32
REL_MAX_DIST = 128
EPS = 1e-6
LAM_INIT_L0 = 0.8 - 0.6 * math.exp(-0.3 * 0)
MASK_VALUE = -1e30
LOG2E = math.log2(math.e)

V7X_LANES = 128
V7X_SCOPED_VMEM_LIMIT_BYTES = 60000 * 1024
MM_BLOCK_F32_WEIGHTS = (2048, 256)
MM_BLOCK_BF16_WEIGHTS = (1024, 1024)
FFN_GATE_UP_BLOCK = (2048, 256)
FFN_DOWN_BLOCK = (512, 512)

ATTN_TILE = 512
SAMPLE_CACHE_PARTS = 2
SAMPLE_HEADS_PER_ITER = 8
HGRN_CHUNKS_PER_STEP = 4
SAMPLE_NEAR = 128


def _cparams(*sem):
    return pltpu.CompilerParams(dimension_semantics=sem,
                                vmem_limit_bytes=V7X_SCOPED_VMEM_LIMIT_BYTES)


def _rmsnorm_kernel(x_ref, g_ref, o_ref):
    x = x_ref[...]
    ms = jnp.mean(x * x, axis=-1, keepdims=True)
    o_ref[...] = (x * lax.rsqrt(ms + EPS) * g_ref[...]).astype(o_ref.dtype)


def _rmsnorm(x, g, out_dtype, rows=256):
    m, d = x.shape
    rows = min(rows, m)
    return pl.pallas_call(
        _rmsnorm_kernel,
        grid=(m // rows,),
        in_specs=[pl.BlockSpec((rows, d), lambda i: (i, 0)),
                  pl.BlockSpec((1, d), lambda i: (0, 0))],
        out_specs=pl.BlockSpec((rows, d), lambda i: (i, 0)),
        out_shape=jax.ShapeDtypeStruct((m, d), out_dtype),
        compiler_params=_cparams("parallel"),
        name="rmsnorm",
    )(x, g.reshape(1, d))


def _mm_kernel(*refs, n_pairs, has_res, n_out, scale):
    xs = refs[0:2 * n_pairs:2]
    ws = refs[1:2 * n_pairs:2]
    pos = 2 * n_pairs
    res_ref = refs[pos] if has_res else None
    pos += int(has_res)
    outs = refs[pos:pos + n_out]
    w16_outs = refs[pos + n_out:]

    total = None
    for p, (x_ref, w_ref) in enumerate(zip(xs, ws)):
        wb = w_ref[...].astype(BF16)
        if w16_outs:
            w16_outs[p][...] = wb
        d = jnp.dot(x_ref[...], wb, preferred_element_type=F32)
        total = d if total is None else total + d
    if scale != 1.0:
        total = total * scale
    if has_res:
        total = total + res_ref[...]
    for o_ref in outs:
        o_ref[...] = total.astype(o_ref.dtype)


def _matmul(pairs, n, out_dtypes, *, residual=None, scale=1.0, bm, bn, emit_w16=False, name="matmul"):
    m = pairs[0][0].shape[0]
    bm = min(bm, m)
    bn = min(bn, n)
    assert m % bm == 0 and n % bn == 0
    in_specs, args = [], []
    for x, w, r0, c0 in pairs:
        kx = x.shape[1]
        assert r0 % kx == 0 and c0 % bn == 0
        in_specs.append(pl.BlockSpec((bm, kx), lambda j, i: (i, 0)))
        in_specs.append(pl.BlockSpec((kx, bn), lambda j, i, rb=r0 // kx, cb=c0 // bn: (rb, cb + j)))
        args += [x, w]
    if residual is not None:
        in_specs.append(pl.BlockSpec((bm, bn), lambda j, i: (i, j)))
        args.append(residual)
    out_specs = [pl.BlockSpec((bm, bn), lambda j, i: (i, j)) for _ in out_dtypes]
    out_shape = [jax.ShapeDtypeStruct((m, n), dt) for dt in out_dtypes]
    if emit_w16:
        for x, _, _, _ in pairs:
            out_specs.append(pl.BlockSpec((x.shape[1], bn), lambda j, i: (0, j)))
            out_shape.append(jax.ShapeDtypeStruct((x.shape[1], n), BF16))
    kern = functools.partial(_mm_kernel, n_pairs=len(pairs), has_res=residual is not None,
                             n_out=len(out_dtypes), scale=scale)
    return pl.pallas_call(
        kern,
        grid=(n // bn, m // bm),
        in_specs=in_specs,
        out_specs=out_specs,
        out_shape=out_shape,
        compiler_params=_cparams("parallel", "arbitrary"),
        name=name,
    )(*args)


def _gateup_kernel(x_ref, wg_ref, wu_ref, o_ref, *w16_outs):
    x = x_ref[...]
    wg = wg_ref[...].astype(BF16)
    wu = wu_ref[...].astype(BF16)
    if w16_outs:
        w16_outs[0][...] = wg
        w16_outs[1][...] = wu
    g = jnp.dot(x, wg, preferred_element_type=F32)
    u = jnp.dot(x, wu, preferred_element_type=F32)
    o_ref[...] = (g * jax.nn.sigmoid(g) * u).astype(o_ref.dtype)


def _gateup(x, wg, wu, *, bm, bn, emit_w16=False):
    m, kdim = x.shape
    n = wg.shape[1]
    bm = min(bm, m)
    assert m % bm == 0 and n % bn == 0
    w_spec = pl.BlockSpec((kdim, bn), lambda j, i: (0, j))
    out_specs = [pl.BlockSpec((bm, bn), lambda j, i: (i, j))]
    out_shape = [jax.ShapeDtypeStruct((m, n), BF16)]
    if emit_w16:
        out_specs += [w_spec, w_spec]
        out_shape += [jax.ShapeDtypeStruct((kdim, n), BF16)] * 2
    return pl.pallas_call(
        _gateup_kernel,
        grid=(n // bn, m // bm),
        in_specs=[pl.BlockSpec((bm, kdim), lambda j, i: (i, 0)), w_spec, w_spec],
        out_specs=out_specs,
        out_shape=out_shape,
        compiler_params=_cparams("parallel", "arbitrary"),
        name="ffn_gate_up",
    )(x, wg, wu)


def _rel_bucket(rel):
    nb = REL_BUCKETS // 2
    max_exact = nb // 2
    ret = jnp.where(rel > 0, nb, 0)
    n = jnp.abs(rel)
    nf = jnp.maximum(n, 1).astype(F32)
    large = max_exact + (jnp.log(nf / max_exact) / math.log(REL_MAX_DIST / max_exact)
                         * (nb - max_exact)).astype(jnp.int32)
    large = jnp.minimum(large, nb - 1)
    return ret + jnp.where(n < max_exact, n, large)


def _bias_tile_kernel(far_ref, table_ref, bucket_ref, mask_ref, o_ref):
    h = pl.program_id(0)
    bucket = bucket_ref[...]
    acc = jnp.zeros(bucket.shape, F32)
    for b in range(REL_BUCKETS):
        acc = jnp.where(bucket == b, table_ref[b, h], acc)
    o_ref[0] = (acc - table_ref[far_ref[0], h]) * LOG2E + mask_ref[...]


def _bias_tiles(table, bucket, maskadd, far_bucket, n_heads):
    r, c = bucket.shape
    return pl.pallas_call(
        _bias_tile_kernel,
        grid=(n_heads,),
        in_specs=[pl.BlockSpec(memory_space=pltpu.SMEM),
                  pl.BlockSpec(memory_space=pltpu.SMEM),
                  pl.BlockSpec((r, c), lambda h: (0, 0)),
                  pl.BlockSpec((r, c), lambda h: (0, 0))],
        out_specs=pl.BlockSpec((1, r, c), lambda h: (h, 0, 0)),
        out_shape=jax.ShapeDtypeStruct((n_heads, r, c), F32),
        compiler_params=_cparams("arbitrary"),
        name="rel_bias_tiles",
    )(far_bucket.reshape(1).astype(jnp.int32), table.astype(F32), bucket.astype(jnp.int32), maskadd)


def _diff_lambda(lp):
    a = jnp.sum(lp[0:1, :] * lp[1:2, :], axis=-1, keepdims=True)
    b = jnp.sum(lp[2:3, :] * lp[3:4, :], axis=-1, keepdims=True)
    return jnp.exp(a) - jnp.exp(b) + LAM_INIT_L0


def _stack_maps(q):
    lane = lax.broadcasted_iota(jnp.int32, q.shape, 1)
    zero = jnp.zeros_like(q)
    return jnp.concatenate([jnp.where(lane < DIFF_QK, q, zero),
                            jnp.where(lane >= DIFF_QK, q, zero)], axis=0)


def _diff_epilogue(acc, l, lam, g, tq):
    o = acc * (1.0 / l)
    o = o[:tq] - lam * o[tq:]
    ms = jnp.mean(o * o, axis=-1, keepdims=True)
    return o * lax.rsqrt(ms + EPS) * g * (1.0 - LAM_INIT_L0)


def _attn_prompt_kernel(lam_ref, g_ref, q_ref, k_ref, v_ref, bias_ref, o_ref, qs_ref, s_ref, m_ref, acc_ref, *,
                        tile):
    i = pl.program_id(1)
    n_lane_tiles = tile // V7X_LANES
    ones = jnp.ones((tile, V7X_LANES), BF16)
    nt = (((1,), (1,)), ((), ()))

    qs_ref[...] = _stack_maps(q_ref[...])
    m_ref[...] = jnp.full(m_ref.shape, MASK_VALUE, F32)
    acc_ref[...] = jnp.zeros(acc_ref.shape, F32)

    def key_rows(d):
        return pl.ds(pl.multiple_of((i - d) * tile, tile), tile)

    def scores(d, slot):
        s_ref[slot] = lax.dot_general(qs_ref[...], k_ref[key_rows(d), :], nt, preferred_element_type=F32)

    def absorb(d, slot, bias):
        vt = jnp.concatenate([v_ref[key_rows(d), :], ones], axis=1)
        s = s_ref[slot]
        if bias is not None:
            s = (s.reshape(2, tile, tile) + bias[None]).reshape(2 * tile, tile)
        cols = [s[:, c * V7X_LANES:(c + 1) * V7X_LANES] for c in range(n_lane_tiles)]
        mx = cols[0]
        for sc in cols[1:]:
            mx = jnp.maximum(mx, sc)
        m_prev = m_ref[...]
        m_new = jnp.maximum(m_prev, jnp.max(mx, axis=-1, keepdims=True))
        alpha = jnp.exp2(m_prev - m_new)
        p = jnp.concatenate([jnp.exp2(sc - m_new).astype(BF16) for sc in cols], axis=1)
        pv = jnp.dot(p, vt, preferred_element_type=F32)
        acc_ref[...] = jnp.concatenate([alpha, alpha], axis=1) * acc_ref[...] + pv
        m_ref[...] = m_new

    i_even = lax.rem(i, 2) == 0

    @pl.when(i_even)
    def _():
        scores(i, 0)

    @pl.when(jnp.logical_not(i_even))
    def _():
        scores(i, 1)

    @pl.when(jnp.logical_and(i_even, i >= 2))
    def _():
        scores(i - 1, 1)
        absorb(i, 0, None)

    d_odd = i - 1 + lax.rem(i, 2)

    def pair_body(u, carry):
        d = d_odd - 2 * u
        scores(d - 1, 0)
        absorb(d, 1, None)
        scores(d - 2, 1)
        absorb(d - 1, 0, None)
        return carry

    lax.fori_loop(0, jnp.where(d_odd >= 3, (d_odd - 1) // 2, 0), pair_body, 0)

    @pl.when(i >= 1)
    def _():
        scores(0, 0)
        absorb(1, 1, bias_ref[0, :, :tile])

    absorb(0, 0, bias_ref[0, :, tile:])

    acc = acc_ref[...]
    y = _diff_epilogue(acc[:, :HEAD_DIM], acc[:, HEAD_DIM:], _diff_lambda(lam_ref[...]), g_ref[...], tile)
    o_ref[...] = y.astype(o_ref.dtype)


def _attn_prompt(lam_p, g, q, k, v, bias, n_heads, tile=ATTN_TILE):
    t = q.shape[0]
    kern = functools.partial(_attn_prompt_kernel, tile=tile)
    return pl.pallas_call(
        kern,
        grid=(n_heads, t // tile),
        in_specs=[pl.BlockSpec((4, DIFF_QK), lambda h, i: (0, 0)),
                  pl.BlockSpec((1, HEAD_DIM), lambda h, i: (0, 0)),
                  pl.BlockSpec((tile, HEAD_DIM), lambda h, i: (i, h)),
                  pl.BlockSpec((t, HEAD_DIM), lambda h, i: (0, h)),
                  pl.BlockSpec((t, HEAD_DIM), lambda h, i: (0, h)),
                  pl.BlockSpec((1, tile, 2 * tile), lambda h, i: (h, 0, 0))],
        out_specs=pl.BlockSpec((tile, HEAD_DIM), lambda h, i: (i, h)),
        out_shape=jax.ShapeDtypeStruct((t, n_heads * HEAD_DIM), BF16),
        scratch_shapes=[pltpu.VMEM((2 * tile, HEAD_DIM), BF16),
                        pltpu.VMEM((2, 2 * tile, tile), F32),
                        pltpu.VMEM((2 * tile, V7X_LANES), F32),
                        pltpu.VMEM((2 * tile, 2 * HEAD_DIM), F32)],
        compiler_params=_cparams("parallel", "arbitrary"),
        name="diff_attn_prompt",
    )(lam_p, g, q, k, v, bias)


def _attn_sample_kernel(lam_ref, g_ref, q_ref, kn_ref, vn_ref, kc_ref, vc_ref, bias_ref, o_ref, m_ref, acc_ref, *,
                        n_heads, keys, ts):
    part = pl.program_id(1)
    is_last = part == pl.num_programs(1) - 1
    lam = _diff_lambda(lam_ref[...])
    g = g_ref[...]
    n_lane_tiles = keys // V7X_LANES
    ones_c = jnp.ones((keys, V7X_LANES), BF16)
    ones_n = jnp.ones((ts, V7X_LANES), BF16)
    nt = (((1,), (1,)), ((), ()))

    @pl.when(part == 0)
    def _():
        m_ref[...] = jnp.full(m_ref.shape, MASK_VALUE, F32)
        acc_ref[...] = jnp.zeros(acc_ref.shape, F32)

    def one_head(h):
        col = pl.ds(pl.multiple_of(h * HEAD_DIM, HEAD_DIM), HEAD_DIM)
        rows = pl.ds(h, keys, stride=n_heads)
        qs = _stack_maps(q_ref[0, :, col])
        kc = kc_ref[0, rows, :].astype(BF16)
        vc = jnp.concatenate([vc_ref[0, rows, :].astype(BF16), ones_c], axis=1)
        vn = jnp.concatenate([vn_ref[0, :, col], ones_n], axis=1)
        bias = bias_ref[h]
        near_bias = jnp.where(is_last, bias[:, :SAMPLE_NEAR], 0.0)
        new_bias = jnp.where(is_last, bias[:, SAMPLE_NEAR:], MASK_VALUE)

        s = lax.dot_general(qs, kc, nt, preferred_element_type=F32)
        cols = [s[:, c * V7X_LANES:(c + 1) * V7X_LANES] for c in range(n_lane_tiles)]
        cols[-1] = (cols[-1].reshape(2, ts, SAMPLE_NEAR) + near_bias[None]).reshape(2 * ts, SAMPLE_NEAR)
        s_new = lax.dot_general(qs, kn_ref[0, :, col], nt, preferred_element_type=F32)
        s_new = (s_new.reshape(2, ts, ts) + new_bias[None]).reshape(2 * ts, ts)

        mx = cols[0]
        for sc in cols[1:]:
            mx = jnp.maximum(mx, sc)
        m_prev = m_ref[h]
        m_new = jnp.maximum(m_prev, jnp.maximum(jnp.max(mx, axis=-1, keepdims=True),
                                                jnp.max(s_new, axis=-1, keepdims=True)))
        alpha = jnp.exp2(m_prev - m_new)
        p = jnp.concatenate([jnp.exp2(sc - m_new).astype(BF16) for sc in cols], axis=1)
        p_new = jnp.exp2(s_new - m_new[:, :ts]).astype(BF16)
        pv = (jnp.dot(p, vc, preferred_element_type=F32)
              + jnp.dot(p_new, vn, preferred_element_type=F32))
        acc = jnp.concatenate([alpha, alpha], axis=1) * acc_ref[h] + pv
        acc_ref[h] = acc
        m_ref[h] = m_new

    def head_group(hg, carry):
        heads = [hg * SAMPLE_HEADS_PER_ITER + u for u in range(SAMPLE_HEADS_PER_ITER)]
        for h in heads:
            one_head(h)

        @pl.when(is_last)
        def _():
            for h in heads:
                acc = acc_ref[h]
                y = _diff_epilogue(acc[:, :HEAD_DIM], acc[:, HEAD_DIM:], lam, g, ts)
                o_ref[0, :, pl.ds(pl.multiple_of(h * HEAD_DIM, HEAD_DIM), HEAD_DIM)] = y.astype(o_ref.dtype)

        return carry

    lax.fori_loop(0, n_heads // SAMPLE_HEADS_PER_ITER, head_group, 0)


def _attn_sample(lam_p, g, q, kn, vn, kc, vc, bias, n_heads):
    nb, ts, d = q.shape
    past = kc.shape[1] // n_heads
    parts = SAMPLE_CACHE_PARTS
    keys = past // parts
    assert past % parts == 0 and keys % V7X_LANES == 0 and SAMPLE_NEAR == V7X_LANES
    kern = functools.partial(_attn_sample_kernel, n_heads=n_heads, keys=keys, ts=ts)
    new = pl.BlockSpec((1, ts, d), lambda b, c: (b, 0, 0))
    cache = pl.BlockSpec((1, keys * n_heads, HEAD_DIM), lambda b, c: (b, c, 0))
    return pl.pallas_call(
        kern,
        grid=(nb, parts),
        in_specs=[pl.BlockSpec((4, DIFF_QK), lambda b, c: (0, 0)),
                  pl.BlockSpec((1, HEAD_DIM), lambda b, c: (0, 0)),
                  new, new, new, cache, cache,
                  pl.BlockSpec((n_heads, ts, SAMPLE_NEAR + ts), lambda b, c: (0, 0, 0))],
        out_specs=new,
        out_shape=jax.ShapeDtypeStruct((nb, ts, d), BF16),
        scratch_shapes=[pltpu.VMEM((n_heads, 2 * ts, V7X_LANES), F32),
                        pltpu.VMEM((n_heads, 2 * ts, 2 * HEAD_DIM), F32)],
        compiler_params=_cparams("parallel", "arbitrary"),
        name="diff_attn_sample",
    )(lam_p, g, q, kn, vn, kc, vc, bias)


def _split3(x):
    hi = x.astype(BF16)
    r1 = x - hi.astype(F32)
    mid = r1.astype(BF16)
    lo = (r1 - mid.astype(F32)).astype(BF16)
    return hi, mid, lo


def _sigmoid(x):
    return 1.0 / (1.0 + jnp.exp(-x))


def _hgrn_kernel(lbl_ref, g_ref, hq_ref, hf_ref, hi_ref, hg_ref, s0_ref, o_ref, s_out_ref, st_ref, *,
                 n_heads, chain):
    c = pl.program_id(1)
    nc = pl.num_programs(1)
    n_streams = st_ref.shape[0]
    n_chunks = HGRN_CHUNKS_PER_STEP
    rows = n_chunks * CHUNK
    d = n_heads * HEAD_DIM

    @pl.when(c == 0)
    def _():
        for s in range(n_streams):
            for h in range(n_heads):
                st_ref[s, h] = s0_ref[s, h].T

    lg = lbl_ref[...]
    e = jnp.exp(lg - jnp.max(lg, axis=0, keepdims=True))
    lb = e[0:1, :] / jnp.sum(e, axis=0, keepdims=True)

    z = hf_ref[...].reshape(rows, d)
    sig = _sigmoid(z)
    logf = jnp.log(lb + (1.0 - lb) * sig)
    kk = (1.0 - lb) * (1.0 - sig)
    hq = hq_ref[...].reshape(rows, d)
    qq = hq * _sigmoid(hq)

    row = lax.broadcasted_iota(jnp.int32, (rows, rows), 0)
    col = lax.broadcasted_iota(jnp.int32, (rows, rows), 1)
    causal = jnp.logical_and(col <= row, col // CHUNK == row // CHUNK)
    tri = jnp.where(causal, 1.0, 0.0).astype(BF16)
    b = None
    for part in _split3(logf):
        t = jnp.dot(tri, part, preferred_element_type=F32)
        b = t if b is None else b + t
    qd = (qq * jnp.exp(b)).astype(BF16)
    kd32 = kk * jnp.exp(-b)
    kd = kd32.astype(BF16)
    decays = [jnp.exp(b[(r + 1) * CHUNK - 1:(r + 1) * CHUNK, :]) for r in range(n_chunks)]
    k_last = jnp.concatenate([kd32[r * CHUNK:(r + 1) * CHUNK] * decays[r] for r in range(n_chunks)],
                             axis=0).astype(BF16)
    v = hi_ref[...].reshape(rows, d).astype(BF16)
    hg = hg_ref[...].reshape(rows, d)
    gate = hg * _sigmoid(hg)
    g = g_ref[...]

    nt = (((1,), (1,)), ((), ()))
    tn = (((0,), (0,)), ((), ()))
    for h in range(n_heads):
        cols = slice(h * HEAD_DIM, (h + 1) * HEAD_DIM)
        a = lax.dot_general(qd[:, cols], kd[:, cols], nt, preferred_element_type=F32)
        a = jnp.where(causal, a, 0.0)
        o_intra = jnp.dot(a.astype(BF16), v[:, cols], preferred_element_type=F32)
        o_parts = []
        st = st_ref[0, h]
        for r in range(n_chunks):
            rr = slice(r * CHUNK, (r + 1) * CHUNK)
            if not chain:
                st = st_ref[r, h]
            o_parts.append(o_intra[rr] + lax.dot_general(qd[rr, cols], st.astype(BF16), nt,
                                                         preferred_element_type=F32))
            st = decays[r][:, cols] * st + lax.dot_general(v[rr, cols], k_last[rr, cols], tn,
                                                           preferred_element_type=F32)
            if not chain:
                st_ref[r, h] = st
        if chain:
            st_ref[0, h] = st
        o = jnp.concatenate(o_parts, axis=0)
        ms = jnp.mean(o * o, axis=-1, keepdims=True)
        y = (o * lax.rsqrt(ms + EPS) * g * gate[:, cols]).astype(o_ref.dtype)
        if chain:
            o_ref[0, :, cols] = y
        else:
            for r in range(n_chunks):
                o_ref[r, :, cols] = y[r * CHUNK:(r + 1) * CHUNK]

    @pl.when(c == nc - 1)
    def _():
        for s in range(n_streams):
            for h in range(n_heads):
                s_out_ref[s, h] = st_ref[s, h].T


def _hgrn(lb_logits, g, hpart, s0, n_heads):
    nb, t, _ = hpart.shape
    d = n_heads * HEAD_DIM
    per = HGRN_CHUNKS_PER_STEP
    chain = t > CHUNK
    sb, tb = (1, per * CHUNK) if chain else (per, CHUNK)
    assert nb % sb == 0 and t % tb == 0
    kern = functools.partial(_hgrn_kernel, n_heads=n_heads, chain=chain)
    part = lambda p: pl.BlockSpec((sb, tb, d), lambda b, c: (b, c, p))
    state = pl.BlockSpec((sb, n_heads, HEAD_DIM, HEAD_DIM), lambda b, c: (b, 0, 0, 0))
    return pl.pallas_call(
        kern,
        grid=(nb // sb, t // tb),
        in_specs=[pl.BlockSpec(lb_logits.shape, lambda b, c: (0, 0)),
                  pl.BlockSpec((1, HEAD_DIM), lambda b, c: (0, 0)),
                  part(0), part(1), part(2), part(3), state],
        out_specs=[pl.BlockSpec((sb, tb, d), lambda b, c: (b, c, 0)), state],
        out_shape=[jax.ShapeDtypeStruct((nb, t, d), BF16),
                   jax.ShapeDtypeStruct(s0.shape, F32)],
        scratch_shapes=[pltpu.VMEM((sb, n_heads, HEAD_DIM, HEAD_DIM), F32)],
        compiler_params=_cparams("parallel", "arbitrary"),
        name="hgrn2_chunks",
    )(lb_logits, g, hpart, hpart, hpart, hpart, s0)


def _cross_kernel(q_ref, k_ref, v_ref, o_ref, *, heads, hd, split_heads):
    nt = (((1,), (1,)), ((), ()))
    if split_heads:
        tq = q_ref.shape[1]
        n_mem = k_ref.shape[1]
        xk = k_ref[0].reshape(n_mem * heads, hd).astype(BF16)
        xv = v_ref[0].reshape(n_mem * heads, hd).astype(BF16)
        q = jnp.concatenate([q_ref[0, :, h * hd:(h + 1) * hd] for h in range(heads)], axis=0)
        s = lax.dot_general(q, xk, nt, preferred_element_type=F32)
        row_head = lax.broadcasted_iota(jnp.int32, s.shape, 0) // tq
        col_head = lax.rem(lax.broadcasted_iota(jnp.int32, s.shape, 1), heads)
        s = jnp.where(row_head == col_head, s, MASK_VALUE)
        p = jnp.exp(s - jnp.max(s, axis=-1, keepdims=True))
        l = jnp.sum(p, axis=-1, keepdims=True)
        o = jnp.dot(p.astype(BF16), xv, preferred_element_type=F32) * (1.0 / l)
        for h in range(heads):
            o_ref[0, :, h * hd:(h + 1) * hd] = o[h * tq:(h + 1) * tq].astype(o_ref.dtype)
        return
    for h in range(heads):
        cols = slice(h * hd, (h + 1) * hd)
        q = q_ref[0, :, cols]
        k = k_ref[0, :, cols].astype(BF16)
        v = v_ref[0, :, cols].astype(BF16)
        s = lax.dot_general(q, k, nt, preferred_element_type=F32)
        p = jnp.exp(s - jnp.max(s, axis=-1, keepdims=True))
        l = jnp.sum(p, axis=-1, keepdims=True)
        o = jnp.dot(p.astype(BF16), v, preferred_element_type=F32) * (1.0 / l)
        o_ref[0, :, cols] = o.astype(o_ref.dtype)


def _cross_attn(q, mk, mv, tq):
    nb, t, d = q.shape
    split_heads = mk.ndim == 4
    kern = functools.partial(_cross_kernel, heads=MEM_HEADS, hd=d // MEM_HEADS, split_heads=split_heads)
    if split_heads:
        mem_spec = pl.BlockSpec((1,) + mk.shape[1:], lambda b, i: (b, 0, 0, 0))
    else:
        mem_spec = pl.BlockSpec((1, N_MEM, d), lambda b, i: (b, 0, 0))
    return pl.pallas_call(
        kern,
        grid=(nb, t // tq),
        in_specs=[pl.BlockSpec((1, tq, d), lambda b, i: (b, i, 0)),
                  mem_spec,
                  mem_spec],
        out_specs=pl.BlockSpec((1, tq, d), lambda b, i: (b, i, 0)),
        out_shape=jax.ShapeDtypeStruct((nb, t, d), BF16),
        compiler_params=_cparams("parallel", "parallel"),
        name="mem_cross_attn",
    )(q, mk, mv)


def _trunk(x, norms, w, widths, mixer_fn, mem_kv_fn, *, blocks, emit_w16):
    nb, t, d = x.shape
    m = nb * t
    bm, bn = blocks
    w16 = {}

    def mm(names, xs, n, out_dtypes, **kw):
        pairs = [(xi,) + w[nm] for nm, xi in zip(names, xs)]
        outs = _matmul(pairs, n, out_dtypes, bm=bm, bn=bn, emit_w16=emit_w16, **kw)
        if emit_w16:
            for nm, arr in zip(names, outs[len(out_dtypes):]):
                w16[nm] = (arr, 0, 0)
        return outs[:len(out_dtypes)]

    x2 = x.reshape(m, d)
    n = _rmsnorm(x2, norms["mix"], BF16)
    (dq,) = mm(["dq"], [n], widths["diff"], [BF16], scale=DIFF_QK ** -0.5 * LOG2E, name="proj_dq")
    dk32, dk16 = mm(["dk"], [n], widths["diff"], [F32, BF16], name="proj_dk")
    dv32, dv16 = mm(["dv"], [n], widths["diff"], [F32, BF16], name="proj_dv")
    (hpart,) = mm(["h"], [n], widths["hgrn_in"], [F32], name="proj_hgrn")

    cat_diff, cat_hgrn, s_new = mixer_fn(dq, dk16, dv16, hpart)

    (h1,) = mm(["out_diff", "out_hgrn"], [cat_diff, cat_hgrn], d, [F32], residual=x2, name="proj_out")

    n2 = _rmsnorm(h1, norms["cross"], BF16)
    (cq,) = mm(["mem_q"], [n2], d, [BF16], scale=(d // MEM_HEADS) ** -0.5, name="proj_mem_q")
    mk, mv = mem_kv_fn()
    co = _cross_attn(cq.reshape(nb, t, d), mk, mv, tq=min(t, 512))
    (h2,) = mm(["mem_o"], [co.reshape(m, d)], d, [F32], residual=h1, name="proj_mem_o")

    n3 = _rmsnorm(h2, norms["ffn"], BF16)
    gu = _gateup(n3, w["gate"][0], w["up"][0], bm=FFN_GATE_UP_BLOCK[0], bn=FFN_GATE_UP_BLOCK[1],
                 emit_w16=emit_w16)
    act = gu[0]
    if emit_w16:
        w16["gate"], w16["up"] = (gu[1], 0, 0), (gu[2], 0, 0)
    (h3,) = _matmul([(act,) + w["down"]], d, [F32], residual=h2, bm=FFN_DOWN_BLOCK[0], bn=FFN_DOWN_BLOCK[1],
                    name="ffn_down")
    y = _rmsnorm(h3, norms["final"], F32)
    return y.reshape(nb, t, d), dk32, dv32, s_new, w16


def kernel(x_prompt, x_sample, mem_prompt, cache_diff_k, cache_diff_v, state_hgrn, cache_mem_k, cache_mem_v, norm_mix, w_in, diff_lambda, diff_subln, rel_bias, hgrn_lb_logits, hgrn_norm, w_out, norm_cross, norm_mem, w_mem_q, w_mem_k, w_mem_v, w_mem_o, norm_ffn, w_ffn_gate, w_ffn_up, w_ffn_down, norm_final):
    bp, tp, d = x_prompt.shape
    bs, ts, _ = x_sample.shape
    depth = w_in.shape[0]
    assert depth == 1 and bp == 1
    past = cache_diff_k.shape[2]
    n_dh = cache_diff_k.shape[3]
    d_diff = n_dh * HEAD_DIM
    n_hh = state_hgrn.shape[2]
    d_hgrn = n_hh * HEAD_DIM
    assert past % CHUNK == 0 and ts == CHUNK and tp % ATTN_TILE == 0

    norms = {"mix": norm_mix[0], "cross": norm_cross[0], "ffn": norm_ffn[0], "final": norm_final}
    widths = {"diff": d_diff, "hgrn_in": w_in.shape[2] - 3 * d_diff}
    w_f32 = {
        "dq": (w_in[0], 0, 0), "dk": (w_in[0], 0, d_diff), "dv": (w_in[0], 0, 2 * d_diff),
        "h": (w_in[0], 0, 3 * d_diff),
        "out_diff": (w_out[0], 0, 0), "out_hgrn": (w_out[0], d_diff, 0),
        "mem_q": (w_mem_q[0], 0, 0), "mem_o": (w_mem_o[0], 0, 0),
        "gate": (w_ffn_gate[0], 0, 0), "up": (w_ffn_up[0], 0, 0),
        "down": (w_ffn_down[0].astype(BF16), 0, 0),
    }
    lam_p = diff_lambda[0].astype(F32)
    subln = diff_subln[0].reshape(1, HEAD_DIM)
    hnorm = hgrn_norm[0].reshape(1, HEAD_DIM)

    tile = ATTN_TILE
    r = jnp.arange(tile)[:, None]
    c = jnp.arange(2 * tile)[None, :]
    rel_p = (c - tile) - r
    vis_p = jnp.logical_or(c < tile, (c - tile) // CHUNK <= r // CHUNK)
    bias_p = _bias_tiles(rel_bias, _rel_bucket(rel_p), jnp.where(vis_p, 0.0, MASK_VALUE).astype(F32),
                         _rel_bucket(jnp.int32(-2 * tile)), n_dh)
    qpos = past + jnp.arange(ts)[:, None]
    kpos = (past - SAMPLE_NEAR) + jnp.arange(SAMPLE_NEAR + ts)[None, :]
    vis_s = kpos // CHUNK <= qpos // CHUNK
    bias_s = _bias_tiles(rel_bias, _rel_bucket(kpos - qpos), jnp.where(vis_s, 0.0, MASK_VALUE).astype(F32),
                         _rel_bucket(jnp.int32(-(SAMPLE_NEAR + 1))), n_dh)

    def mixer_prompt(dq, dk16, dv16, hpart):
        cat_diff = _attn_prompt(lam_p, subln, dq, dk16, dv16, bias_p, n_dh)
        s0 = jnp.zeros((bp, n_hh, HEAD_DIM, HEAD_DIM), F32)
        cat_hgrn, s_new = _hgrn(hgrn_lb_logits, hnorm, hpart.reshape(bp, tp, 4 * d_hgrn), s0, n_hh)
        return cat_diff, cat_hgrn.reshape(bp * tp, d_hgrn), s_new

    mem_out = {}

    def mem_kv_prompt():
        mn = _rmsnorm(mem_prompt.reshape(bp * N_MEM, d), norm_mem[0], BF16)
        bm, bn = MM_BLOCK_F32_WEIGHTS
        (mk,) = _matmul([(mn, w_mem_k[0], 0, 0)], d, [F32], bm=bm, bn=bn, name="proj_mem_k")
        (mv,) = _matmul([(mn, w_mem_v[0], 0, 0)], d, [F32], bm=bm, bn=bn, name="proj_mem_v")
        mem_out["k"], mem_out["v"] = mk, mv
        return mk.reshape(bp, N_MEM, d), mv.reshape(bp, N_MEM, d)

    def mixer_sample(dq, dk16, dv16, hpart):
        shp = (bs, ts, d_diff)
        cat_diff = _attn_sample(lam_p, subln, dq.reshape(shp), dk16.reshape(shp), dv16.reshape(shp),
                                cache_diff_k[0].reshape(bs, past * n_dh, HEAD_DIM),
                                cache_diff_v[0].reshape(bs, past * n_dh, HEAD_DIM), bias_s, n_dh)
        cat_hgrn, s_new = _hgrn(hgrn_lb_logits, hnorm, hpart.reshape(bs, ts, 4 * d_hgrn), state_hgrn[0], n_hh)
        return cat_diff.reshape(bs * ts, d_diff), cat_hgrn.reshape(bs * ts, d_hgrn), s_new

    def mem_kv_sample():
        return cache_mem_k[0], cache_mem_v[0]

    y_s, dk_s, dv_s, s_s, w_bf16 = _trunk(x_sample, norms, w_f32, widths, mixer_sample, mem_kv_sample,
                                          blocks=MM_BLOCK_F32_WEIGHTS, emit_w16=True)
    w_bf16["down"] = w_f32["down"]
    y_p, dk_p, dv_p, s_p, _ = _trunk(x_prompt, norms, w_bf16, widths, mixer_prompt, mem_kv_prompt,
                                     blocks=MM_BLOCK_BF16_WEIGHTS, emit_w16=False)

    mhd = d // MEM_HEADS
    return (y_p, y_s,
            dk_p.reshape(depth, bp, tp, n_dh, HEAD_DIM), dv_p.reshape(depth, bp, tp, n_dh, HEAD_DIM),
            s_p.reshape(depth, bp, n_hh, HEAD_DIM, HEAD_DIM),
            mem_out["k"].reshape(depth, bp, N_MEM, MEM_HEADS, mhd),
            mem_out["v"].reshape(depth, bp, N_MEM, MEM_HEADS, mhd),
            dk_s.reshape(depth, bs, ts, n_dh, HEAD_DIM), dv_s.reshape(depth, bs, ts, n_dh, HEAD_DIM),
            s_s.reshape(depth, bs, n_hh, HEAD_DIM, HEAD_DIM))
```

```python
import functools
import math

import jax
import jax.numpy as jnp
from jax import lax
from jax.experimental import pallas as pl
from jax.experimental.pallas import tpu as pltpu

F32 = jnp.float32
BF16 = jnp.bfloat16

CHUNK = 64
HEAD_DIM = 128
DIFF_QK = HEAD_DIM // 2
N_MEM = 256
MEM_HEADS = 4
REL_BUCKETS = 32
REL_MAX_DIST = 128
EPS = 1e-6
LAM_INIT_L0 = 0.8 - 0.6 * math.exp(-0.3 * 0)
MASK_VALUE = -1e30
LOG2E = math.log2(math.e)

V7X_LANES = 128
V7X_SCOPED_VMEM_LIMIT_BYTES = 60000 * 1024
MM_BLOCK_F32_WEIGHTS = (2048, 256)
MM_BLOCK_BF16_WEIGHTS = (1024, 1024)
FFN_GATE_UP_BLOCK = (2048, 256)
FFN_DOWN_BLOCK = (512, 512)

DIAG_SLOT = 2
ATTN_TILE = 512
SAMPLE_CACHE_PARTS = 2
SAMPLE_HEADS_PER_ITER = 8
HGRN_CHUNKS_PER_STEP = 4
SAMPLE_NEAR = 128


def _cparams(*sem):
    return pltpu.CompilerParams(dimension_semantics=sem,
                                vmem_limit_bytes=V7X_SCOPED_VMEM_LIMIT_BYTES)


def _rmsnorm_kernel(x_ref, g_ref, o_ref):
    x = x_ref[...]
    ms = jnp.mean(x * x, axis=-1, keepdims=True)
    o_ref[...] = (x * lax.rsqrt(ms + EPS) * g_ref[...]).astype(o_ref.dtype)


def _rmsnorm(x, g, out_dtype, rows=256):
    m, d = x.shape
    rows = min(rows, m)
    return pl.pallas_call(
        _rmsnorm_kernel,
        grid=(m // rows,),
        in_specs=[pl.BlockSpec((rows, d), lambda i: (i, 0)),
                  pl.BlockSpec((1, d), lambda i: (0, 0))],
        out_specs=pl.BlockSpec((rows, d), lambda i: (i, 0)),
        out_shape=jax.ShapeDtypeStruct((m, d), out_dtype),
        compiler_params=_cparams("parallel"),
        name="rmsnorm",
    )(x, g.reshape(1, d))


def _mm_kernel(*refs, n_pairs, has_res, n_out, scale):
    xs = refs[0:2 * n_pairs:2]
    ws = refs[1:2 * n_pairs:2]
    pos = 2 * n_pairs
    res_ref = refs[pos] if has_res else None
    pos += int(has_res)
    outs = refs[pos:pos + n_out]
    w16_outs = refs[pos + n_out:]

    total = None
    for p, (x_ref, w_ref) in enumerate(zip(xs, ws)):
        wb = w_ref[...].astype(BF16)
        if w16_outs:
            w16_outs[p][...] = wb
        d = jnp.dot(x_ref[...], wb, preferred_element_type=F32)
        total = d if total is None else total + d
    if scale != 1.0:
        total = total * scale
    if has_res:
        total = total + res_ref[...]
    for o_ref in outs:
        o_ref[...] = total.astype(o_ref.dtype)


def _matmul(pairs, n, out_dtypes, *, residual=None, scale=1.0, bm, bn, emit_w16=False, name="matmul"):
    m = pairs[0][0].shape[0]
    bm = min(bm, m)
    bn = min(bn, n)
    assert m % bm == 0 and n % bn == 0
    in_specs, args = [], []
    for x, w, r0, c0 in pairs:
        kx = x.shape[1]
        assert r0 % kx == 0 and c0 % bn == 0
        in_specs.append(pl.BlockSpec((bm, kx), lambda j, i: (i, 0)))
        in_specs.append(pl.BlockSpec((kx, bn), lambda j, i, rb=r0 // kx, cb=c0 // bn: (rb, cb + j)))
        args += [x, w]
    if residual is not None:
        in_specs.append(pl.BlockSpec((bm, bn), lambda j, i: (i, j)))
        args.append(residual)
    out_specs = [pl.BlockSpec((bm, bn), lambda j, i: (i, j)) for _ in out_dtypes]
    out_shape = [jax.ShapeDtypeStruct((m, n), dt) for dt in out_dtypes]
    if emit_w16:
        for x, _, _, _ in pairs:
            out_specs.append(pl.BlockSpec((x.shape[1], bn), lambda j, i: (0, j)))
            out_shape.append(jax.ShapeDtypeStruct((x.shape[1], n), BF16))
    kern = functools.partial(_mm_kernel, n_pairs=len(pairs), has_res=residual is not None,
                             n_out=len(out_dtypes), scale=scale)
    return pl.pallas_call(
        kern,
        grid=(n // bn, m // bm),
        in_specs=in_specs,
        out_specs=out_specs,
        out_shape=out_shape,
        compiler_params=_cparams("parallel", "arbitrary"),
        name=name,
    )(*args)


def _gateup_kernel(x_ref, wg_ref, wu_ref, o_ref, *w16_outs):
    x = x_ref[...]
    wg = wg_ref[...].astype(BF16)
    wu = wu_ref[...].astype(BF16)
    if w16_outs:
        w16_outs[0][...] = wg
        w16_outs[1][...] = wu
    g = jnp.dot(x, wg, preferred_element_type=F32)
    u = jnp.dot(x, wu, preferred_element_type=F32)
    o_ref[...] = (_silu(g) * u).astype(o_ref.dtype)


def _gateup(x, wg, wu, *, bm, bn, emit_w16=False):
    m, kdim = x.shape
    n = wg.shape[1]
    bm = min(bm, m)
    assert m % bm == 0 and n % bn == 0
    w_spec = pl.BlockSpec((kdim, bn), lambda j, i: (0, j))
    out_specs = [pl.BlockSpec((bm, bn), lambda j, i: (i, j))]
    out_shape = [jax.ShapeDtypeStruct((m, n), BF16)]
    if emit_w16:
        out_specs += [w_spec, w_spec]
        out_shape += [jax.ShapeDtypeStruct((kdim, n), BF16)] * 2
    return pl.pallas_call(
        _gateup_kernel,
        grid=(n // bn, m // bm),
        in_specs=[pl.BlockSpec((bm, kdim), lambda j, i: (i, 0)), w_spec, w_spec],
        out_specs=out_specs,
        out_shape=out_shape,
        compiler_params=_cparams("parallel", "arbitrary"),
        name="ffn_gate_up",
    )(x, wg, wu)


def _rel_bucket(rel):
    nb = REL_BUCKETS // 2
    max_exact = nb // 2
    ret = jnp.where(rel > 0, nb, 0)
    n = jnp.abs(rel)
    nf = jnp.maximum(n, 1).astype(F32)
    large = max_exact + (jnp.log(nf / max_exact) / math.log(REL_MAX_DIST / max_exact)
                         * (nb - max_exact)).astype(jnp.int32)
    large = jnp.minimum(large, nb - 1)
    return ret + jnp.where(n < max_exact, n, large)


def _bias_tile_kernel(far_ref, table_ref, bucket_ref, mask_ref, o_ref):
    h = pl.program_id(0)
    bucket = bucket_ref[...]
    acc = jnp.zeros(bucket.shape, F32)
    for b in range(REL_BUCKETS):
        acc = jnp.where(bucket == b, table_ref[b, h], acc)
    o_ref[0] = (acc - table_ref[far_ref[0], h]) * LOG2E + mask_ref[...]


def _bias_tiles(table, bucket, maskadd, far_bucket, n_heads):
    r, c = bucket.shape
    return pl.pallas_call(
        _bias_tile_kernel,
        grid=(n_heads,),
        in_specs=[pl.BlockSpec(memory_space=pltpu.SMEM),
                  pl.BlockSpec(memory_space=pltpu.SMEM),
                  pl.BlockSpec((r, c), lambda h: (0, 0)),
                  pl.BlockSpec((r, c), lambda h: (0, 0))],
        out_specs=pl.BlockSpec((1, r, c), lambda h: (h, 0, 0)),
        out_shape=jax.ShapeDtypeStruct((n_heads, r, c), F32),
        compiler_params=_cparams("arbitrary"),
        name="rel_bias_tiles",
    )(far_bucket.reshape(1).astype(jnp.int32), table.astype(F32), bucket.astype(jnp.int32), maskadd)


def _diff_lambda(lp):
    a = jnp.sum(lp[0:1, :] * lp[1:2, :], axis=-1, keepdims=True)
    b = jnp.sum(lp[2:3, :] * lp[3:4, :], axis=-1, keepdims=True)
    return jnp.exp(a) - jnp.exp(b) + LAM_INIT_L0


def _stack_maps(q):
    lane = lax.broadcasted_iota(jnp.int32, q.shape, 1)
    zero = jnp.zeros_like(q)
    return jnp.concatenate([jnp.where(lane < DIFF_QK, q, zero),
                            jnp.where(lane >= DIFF_QK, q, zero)], axis=0)


def _diff_epilogue(acc, l, lam, g, tq):
    o = acc * (1.0 / l)
    o = o[:tq] - lam * o[tq:]
    ms = jnp.mean(o * o, axis=-1, keepdims=True)
    return o * lax.rsqrt(ms + EPS) * g * (1.0 - LAM_INIT_L0)


def _attn_prompt_kernel(lam_ref, g_ref, q_ref, qn_ref, k_ref, v_ref, bias_ref, o_ref, qs_ref, s_ref, m_ref,
                        acc_ref, *, tile):
    i = pl.program_id(1)
    n_lane_tiles = tile // V7X_LANES
    ones = jnp.ones((tile, V7X_LANES), BF16)
    nt = (((1,), (1,)), ((), ()))

    qs_ref[...] = _stack_maps(q_ref[...])
    m_ref[...] = jnp.full(m_ref.shape, MASK_VALUE, F32)
    acc_ref[...] = jnp.zeros(acc_ref.shape, F32)

    def key_rows(d):
        return pl.ds(pl.multiple_of((i - d) * tile, tile), tile)

    def scores(d, slot):
        s_ref[slot] = lax.dot_general(qs_ref[...], k_ref[key_rows(d), :], nt, preferred_element_type=F32)

    def absorb(d, slot, bias):
        vt = jnp.concatenate([v_ref[key_rows(d), :], ones], axis=1)
        s = s_ref[slot]
        if bias is not None:
            s = (s.reshape(2, tile, tile) + bias[None]).reshape(2 * tile, tile)
        cols = [s[:, c * V7X_LANES:(c + 1) * V7X_LANES] for c in range(n_lane_tiles)]
        mx = cols[0]
        for sc in cols[1:]:
            mx = jnp.maximum(mx, sc)
        m_prev = m_ref[...]
        m_new = jnp.maximum(m_prev, jnp.max(mx, axis=-1, keepdims=True))
        alpha = jnp.exp2(m_prev - m_new)
        p = jnp.concatenate([jnp.exp2(sc - m_new).astype(BF16) for sc in cols], axis=1)
        pv = jnp.dot(p, vt, preferred_element_type=F32)
        acc_ref[...] = jnp.concatenate([alpha, alpha], axis=1) * acc_ref[...] + pv
        m_ref[...] = m_new

    i_even = lax.rem(i, 2) == 0

    @pl.when(i == 0)
    def _():
        scores(0, DIAG_SLOT)

    @pl.when(jnp.logical_and(i_even, i >= 2))
    def _():
        scores(i - 1, 1)
        absorb(i, 0, None)

    d_odd = i - 1 + lax.rem(i, 2)

    def pair_body(u, carry):
        d = d_odd - 2 * u
        scores(d - 1, 0)
        absorb(d, 1, None)
        scores(d - 2, 1)
        absorb(d - 1, 0, None)
        return carry

    lax.fori_loop(0, jnp.where(d_odd >= 3, (d_odd - 1) // 2, 0), pair_body, 0)

    @pl.when(i >= 1)
    def _():
        scores(0, DIAG_SLOT)
        absorb(1, 1, bias_ref[0, :, :tile])

    def diagonal(next_slot):
        if next_slot is not None:
            s_ref[next_slot] = lax.dot_general(_stack_maps(qn_ref[...]), k_ref[pl.ds(0, tile), :], nt,
                                               preferred_element_type=F32)
        absorb(0, DIAG_SLOT, bias_ref[0, :, tile:])

    has_next = i + 1 < pl.num_programs(1)

    @pl.when(jnp.logical_and(has_next, jnp.logical_not(i_even)))
    def _():
        diagonal(0)

    @pl.when(jnp.logical_and(has_next, i_even))
    def _():
        diagonal(1)

    @pl.when(jnp.logical_not(has_next))
    def _():
        diagonal(None)

    acc = acc_ref[...]
    y = _diff_epilogue(acc[:, :HEAD_DIM], acc[:, HEAD_DIM:], _diff_lambda(lam_ref[...]), g_ref[...], tile)
    o_ref[...] = y.astype(o_ref.dtype)


def _attn_prompt(lam_p, g, q, k, v, bias, n_heads, tile=ATTN_TILE):
    t = q.shape[0]
    nq = t // tile
    kern = functools.partial(_attn_prompt_kernel, tile=tile)
    return pl.pallas_call(
        kern,
        grid=(n_heads, nq),
        in_specs=[pl.BlockSpec((4, DIFF_QK), lambda h, i: (0, 0)),
                  pl.BlockSpec((1, HEAD_DIM), lambda h, i: (0, 0)),
                  pl.BlockSpec((tile, HEAD_DIM), lambda h, i: (i, h)),
                  pl.BlockSpec((tile, HEAD_DIM), lambda h, i: (jnp.minimum(i + 1, nq - 1), h)),
                  pl.BlockSpec((t, HEAD_DIM), lambda h, i: (0, h)),
                  pl.BlockSpec((t, HEAD_DIM), lambda h, i: (0, h)),
                  pl.BlockSpec((1, tile, 2 * tile), lambda h, i: (h, 0, 0))],
        out_specs=pl.BlockSpec((tile, HEAD_DIM), lambda h, i: (i, h)),
        out_shape=jax.ShapeDtypeStruct((t, n_heads * HEAD_DIM), BF16),
        scratch_shapes=[pltpu.VMEM((2 * tile, HEAD_DIM), BF16),
                        pltpu.VMEM((DIAG_SLOT + 1, 2 * tile, tile), F32),
                        pltpu.VMEM((2 * tile, V7X_LANES), F32),
                        pltpu.VMEM((2 * tile, 2 * HEAD_DIM), F32)],
        compiler_params=_cparams("parallel", "arbitrary"),
        name="diff_attn_prompt",
    )(lam_p, g, q, q, k, v, bias)


def _attn_sample_kernel(lam_ref, g_ref, q_ref, kn_ref, vn_ref, kc_ref, vc_ref, bias_ref, o_ref, m_ref, acc_ref, *,
                        n_heads, keys, ts):
    part = pl.program_id(1)
    is_last = part == pl.num_programs(1) - 1
    lam = _diff_lambda(lam_ref[...])
    g = g_ref[...]
    n_lane_tiles = keys // V7X_LANES
    ones_c = jnp.ones((keys, V7X_LANES), BF16)
    ones_n = jnp.ones((ts, V7X_LANES), BF16)
    nt = (((1,), (1,)), ((), ()))

    @pl.when(part == 0)
    def _():
        m_ref[...] = jnp.full(m_ref.shape, MASK_VALUE, F32)
        acc_ref[...] = jnp.zeros(acc_ref.shape, F32)

    def one_head(h):
        col = pl.ds(pl.multiple_of(h * HEAD_DIM, HEAD_DIM), HEAD_DIM)
        rows = pl.ds(h, keys, stride=n_heads)
        qs = _stack_maps(q_ref[0, :, col])
        kc = kc_ref[0, rows, :].astype(BF16)
        vc = jnp.concatenate([vc_ref[0, rows, :].astype(BF16), ones_c], axis=1)
        vn = jnp.concatenate([vn_ref[0, :, col], ones_n], axis=1)
        bias = bias_ref[h]
        near_bias = jnp.where(is_last, bias[:, :SAMPLE_NEAR], 0.0)
        new_bias = jnp.where(is_last, bias[:, SAMPLE_NEAR:], MASK_VALUE)

        s = lax.dot_general(qs, kc, nt, preferred_element_type=F32)
        cols = [s[:, c * V7X_LANES:(c + 1) * V7X_LANES] for c in range(n_lane_tiles)]
        cols[-1] = (cols[-1].reshape(2, ts, SAMPLE_NEAR) + near_bias[None]).reshape(2 * ts, SAMPLE_NEAR)
        s_new = lax.dot_general(qs, kn_ref[0, :, col], nt, preferred_element_type=F32)
        s_new = (s_new.reshape(2, ts, ts) + new_bias[None]).reshape(2 * ts, ts)

        mx = cols[0]
        for sc in cols[1:]:
            mx = jnp.maximum(mx, sc)
        m_prev = m_ref[h]
        m_new = jnp.maximum(m_prev, jnp.maximum(jnp.max(mx, axis=-1, keepdims=True),
                                                jnp.max(s_new, axis=-1, keepdims=True)))
        alpha = jnp.exp2(m_prev - m_new)
        p = jnp.concatenate([jnp.exp2(sc - m_new).astype(BF16) for sc in cols], axis=1)
        p_new = jnp.exp2(s_new - m_new[:, :ts]).astype(BF16)
        pv = (jnp.dot(p, vc, preferred_element_type=F32)
              + jnp.dot(p_new, vn, preferred_element_type=F32))
        acc = jnp.concatenate([alpha, alpha], axis=1) * acc_ref[h] + pv
        acc_ref[h] = acc
        m_ref[h] = m_new

    def head_group(hg, carry):
        heads = [hg * SAMPLE_HEADS_PER_ITER + u for u in range(SAMPLE_HEADS_PER_ITER)]
        for h in heads:
            one_head(h)

        @pl.when(is_last)
        def _():
            for h in heads:
                acc = acc_ref[h]
                y = _diff_epilogue(acc[:, :HEAD_DIM], acc[:, HEAD_DIM:], lam, g, ts)
                o_ref[0, :, pl.ds(pl.multiple_of(h * HEAD_DIM, HEAD_DIM), HEAD_DIM)] = y.astype(o_ref.dtype)

        return carry

    lax.fori_loop(0, n_heads // SAMPLE_HEADS_PER_ITER, head_group, 0)


def _attn_sample(lam_p, g, q, kn, vn, kc, vc, bias, n_heads):
    nb, ts, d = q.shape
    past = kc.shape[1] // n_heads
    parts = SAMPLE_CACHE_PARTS
    keys = past // parts
    assert past % parts == 0 and keys % V7X_LANES == 0 and SAMPLE_NEAR == V7X_LANES
    kern = functools.partial(_attn_sample_kernel, n_heads=n_heads, keys=keys, ts=ts)
    new = pl.BlockSpec((1, ts, d), lambda b, c: (b, 0, 0))
    cache = pl.BlockSpec((1, keys * n_heads, HEAD_DIM), lambda b, c: (b, c, 0))
    return pl.pallas_call(
        kern,
        grid=(nb, parts),
        in_specs=[pl.BlockSpec((4, DIFF_QK), lambda b, c: (0, 0)),
                  pl.BlockSpec((1, HEAD_DIM), lambda b, c: (0, 0)),
                  new, new, new, cache, cache,
                  pl.BlockSpec((n_heads, ts, SAMPLE_NEAR + ts), lambda b, c: (0, 0, 0))],
        out_specs=new,
        out_shape=jax.ShapeDtypeStruct((nb, ts, d), BF16),
        scratch_shapes=[pltpu.VMEM((n_heads, 2 * ts, V7X_LANES), F32),
                        pltpu.VMEM((n_heads, 2 * ts, 2 * HEAD_DIM), F32)],
        compiler_params=_cparams("parallel", "arbitrary"),
        name="diff_attn_sample",
    )(lam_p, g, q, kn, vn, kc, vc, bias)


def _split3(x):
    hi = x.astype(BF16)
    r1 = x - hi.astype(F32)
    mid = r1.astype(BF16)
    lo = (r1 - mid.astype(F32)).astype(BF16)
    return hi, mid, lo


def _silu(x):
    h = 0.5 * x
    return h + h * jnp.tanh(h)


def _hgrn_kernel(lbl_ref, g_ref, hq_ref, hf_ref, hi_ref, hg_ref, s0_ref, o_ref, s_out_ref, st_ref, *,
                 n_heads, chain):
    c = pl.program_id(1)
    nc = pl.num_programs(1)
    n_streams = st_ref.shape[0]
    n_chunks = HGRN_CHUNKS_PER_STEP
    rows = n_chunks * CHUNK
    d = n_heads * HEAD_DIM

    @pl.when(c == 0)
    def _():
        for s in range(n_streams):
            for h in range(n_heads):
                st_ref[s, h] = s0_ref[s, h].T

    lg = lbl_ref[...]
    e = jnp.exp(lg - jnp.max(lg, axis=0, keepdims=True))
    lb = e[0:1, :] / jnp.sum(e, axis=0, keepdims=True)

    c_half = 0.5 * (1.0 - lb)
    t = jnp.tanh(0.5 * hf_ref[...].reshape(rows, d))
    ct = c_half * t
    logf = jnp.log2((lb + c_half) + ct)
    kk = c_half - ct
    qq = _silu(hq_ref[...].reshape(rows, d))

    row = lax.broadcasted_iota(jnp.int32, (rows, rows), 0)
    col = lax.broadcasted_iota(jnp.int32, (rows, rows), 1)
    causal = jnp.logical_and(col <= row, col // CHUNK == row // CHUNK)
    tri = jnp.where(causal, 1.0, 0.0).astype(BF16)
    b = None
    for part in _split3(logf):
        term = jnp.dot(tri, part, preferred_element_type=F32)
        b = term if b is None else b + term
    qd = (qq * jnp.exp2(b)).astype(BF16)
    kd32 = kk * jnp.exp2(-b)
    kd = kd32.astype(BF16)
    decays = [jnp.exp2(b[(r + 1) * CHUNK - 1:(r + 1) * CHUNK, :]) for r in range(n_chunks)]
    k_last = jnp.concatenate([kd32[r * CHUNK:(r + 1) * CHUNK] * decays[r] for r in range(n_chunks)],
                             axis=0).astype(BF16)
    v = hi_ref[...].reshape(rows, d).astype(BF16)
    gate = _silu(hg_ref[...].reshape(rows, d))
    g = g_ref[...]

    nt = (((1,), (1,)), ((), ()))
    tn = (((0,), (0,)), ((), ()))
    for h in range(n_heads):
        cols = slice(h * HEAD_DIM, (h + 1) * HEAD_DIM)
        a = lax.dot_general(qd[:, cols], kd[:, cols], nt, preferred_element_type=F32)
        a = jnp.where(causal, a, 0.0)
        o_intra = jnp.dot(a.astype(BF16), v[:, cols], preferred_element_type=F32)
        o_parts = []
        st = st_ref[0, h]
        for r in range(n_chunks):
            rr = slice(r * CHUNK, (r + 1) * CHUNK)
            if not chain:
                st = st_ref[r, h]
            o_parts.append(o_intra[rr] + lax.dot_general(qd[rr, cols], st.astype(BF16), nt,
                                                         preferred_element_type=F32))
            st = decays[r][:, cols] * st + lax.dot_general(v[rr, cols], k_last[rr, cols], tn,
                                                           preferred_element_type=F32)
            if not chain:
                st_ref[r, h] = st
        if chain:
            st_ref[0, h] = st
        o = jnp.concatenate(o_parts, axis=0)
        ms = jnp.mean(o * o, axis=-1, keepdims=True)
        y = (o * lax.rsqrt(ms + EPS) * g * gate[:, cols]).astype(o_ref.dtype)
        if chain:
            o_ref[0, :, cols] = y
        else:
            for r in range(n_chunks):
                o_ref[r, :, cols] = y[r * CHUNK:(r + 1) * CHUNK]

    @pl.when(c == nc - 1)
    def _():
        for s in range(n_streams):
            for h in range(n_heads):
                s_out_ref[s, h] = st_ref[s, h].T


def _hgrn(lb_logits, g, hpart, s0, n_heads):
    nb, t, _ = hpart.shape
    d = n_heads * HEAD_DIM
    per = HGRN_CHUNKS_PER_STEP
    chain = t > CHUNK
    sb, tb = (1, per * CHUNK) if chain else (per, CHUNK)
    assert nb % sb == 0 and t % tb == 0
    kern = functools.partial(_hgrn_kernel, n_heads=n_heads, chain=chain)
    part = lambda p: pl.BlockSpec((sb, tb, d), lambda b, c: (b, c, p))
    state = pl.BlockSpec((sb, n_heads, HEAD_DIM, HEAD_DIM), lambda b, c: (b, 0, 0, 0))
    return pl.pallas_call(
        kern,
        grid=(nb // sb, t // tb),
        in_specs=[pl.BlockSpec(lb_logits.shape, lambda b, c: (0, 0)),
                  pl.BlockSpec((1, HEAD_DIM), lambda b, c: (0, 0)),
                  part(0), part(1), part(2), part(3), state],
        out_specs=[pl.BlockSpec((sb, tb, d), lambda b, c: (b, c, 0)), state],
        out_shape=[jax.ShapeDtypeStruct((nb, t, d), BF16),
                   jax.ShapeDtypeStruct(s0.shape, F32)],
        scratch_shapes=[pltpu.VMEM((sb, n_heads, HEAD_DIM, HEAD_DIM), F32)],
        compiler_params=_cparams("parallel", "arbitrary"),
        name="hgrn2_chunks",
    )(lb_logits, g, hpart, hpart, hpart, hpart, s0)


def _cross_kernel(q_ref, k_ref, v_ref, o_ref, *, heads, hd, split_heads):
    nt = (((1,), (1,)), ((), ()))
    if split_heads:
        tq = q_ref.shape[1]
        n_mem = k_ref.shape[1]
        xk = k_ref[0].reshape(n_mem * heads, hd).astype(BF16)
        xv = v_ref[0].reshape(n_mem * heads, hd).astype(BF16)
        q = jnp.concatenate([q_ref[0, :, h * hd:(h + 1) * hd] for h in range(heads)], axis=0)
        s = lax.dot_general(q, xk, nt, preferred_element_type=F32)
        row_head = lax.broadcasted_iota(jnp.int32, s.shape, 0) // tq
        col_head = lax.rem(lax.broadcasted_iota(jnp.int32, s.shape, 1), heads)
        s = jnp.where(row_head == col_head, s, MASK_VALUE)
        p = jnp.exp(s - jnp.max(s, axis=-1, keepdims=True))
        l = jnp.sum(p, axis=-1, keepdims=True)
        o = jnp.dot(p.astype(BF16), xv, preferred_element_type=F32) * (1.0 / l)
        for h in range(heads):
            o_ref[0, :, h * hd:(h + 1) * hd] = o[h * tq:(h + 1) * tq].astype(o_ref.dtype)
        return
    for h in range(heads):
        cols = slice(h * hd, (h + 1) * hd)
        q = q_ref[0, :, cols]
        k = k_ref[0, :, cols].astype(BF16)
        v = v_ref[0, :, cols].astype(BF16)
        s = lax.dot_general(q, k, nt, preferred_element_type=F32)
        p = jnp.exp(s - jnp.max(s, axis=-1, keepdims=True))
        l = jnp.sum(p, axis=-1, keepdims=True)
        o = jnp.dot(p.astype(BF16), v, preferred_element_type=F32) * (1.0 / l)
        o_ref[0, :, cols] = o.astype(o_ref.dtype)


def _cross_attn(q, mk, mv, tq):
    nb, t, d = q.shape
    split_heads = mk.ndim == 4
    kern = functools.partial(_cross_kernel, heads=MEM_HEADS, hd=d // MEM_HEADS, split_heads=split_heads)
    if split_heads:
        mem_spec = pl.BlockSpec((1,) + mk.shape[1:], lambda b, i: (b, 0, 0, 0))
    else:
        mem_spec = pl.BlockSpec((1, N_MEM, d), lambda b, i: (b, 0, 0))
    return pl.pallas_call(
        kern,
        grid=(nb, t // tq),
        in_specs=[pl.BlockSpec((1, tq, d), lambda b, i: (b, i, 0)),
                  mem_spec,
                  mem_spec],
        out_specs=pl.BlockSpec((1, tq, d), lambda b, i: (b, i, 0)),
        out_shape=jax.ShapeDtypeStruct((nb, t, d), BF16),
        compiler_params=_cparams("parallel", "parallel"),
        name="mem_cross_attn",
    )(q, mk, mv)


def _trunk(x, norms, w, widths, mixer_fn, mem_kv_fn, *, blocks, emit_w16):
    nb, t, d = x.shape
    m = nb * t
    bm, bn = blocks
    w16 = {}

    def mm(names, xs, n, out_dtypes, **kw):
        pairs = [(xi,) + w[nm] for nm, xi in zip(names, xs)]
        outs = _matmul(pairs, n, out_dtypes, bm=bm, bn=bn, emit_w16=emit_w16, **kw)
        if emit_w16:
            for nm, arr in zip(names, outs[len(out_dtypes):]):
                w16[nm] = (arr, 0, 0)
        return outs[:len(out_dtypes)]

    x2 = x.reshape(m, d)
    n = _rmsnorm(x2, norms["mix"], BF16)
    (dq,) = mm(["dq"], [n], widths["diff"], [BF16], scale=DIFF_QK ** -0.5 * LOG2E, name="proj_dq")
    dk32, dk16 = mm(["dk"], [n], widths["diff"], [F32, BF16], name="proj_dk")
    dv32, dv16 = mm(["dv"], [n], widths["diff"], [F32, BF16], name="proj_dv")
    (hpart,) = mm(["h"], [n], widths["hgrn_in"], [F32], name="proj_hgrn")

    cat_diff, cat_hgrn, s_new = mixer_fn(dq, dk16, dv16, hpart)

    (h1,) = mm(["out_diff", "out_hgrn"], [cat_diff, cat_hgrn], d, [F32], residual=x2, name="proj_out")

    n2 = _rmsnorm(h1, norms["cross"], BF16)
    (cq,) = mm(["mem_q"], [n2], d, [BF16], scale=(d // MEM_HEADS) ** -0.5, name="proj_mem_q")
    mk, mv = mem_kv_fn()
    co = _cross_attn(cq.reshape(nb, t, d), mk, mv, tq=min(t, 512))
    (h2,) = mm(["mem_o"], [co.reshape(m, d)], d, [F32], residual=h1, name="proj_mem_o")

    n3 = _rmsnorm(h2, norms["ffn"], BF16)
    gu = _gateup(n3, w["gate"][0], w["up"][0], bm=FFN_GATE_UP_BLOCK[0], bn=FFN_GATE_UP_BLOCK[1],
                 emit_w16=emit_w16)
    act = gu[0]
    if emit_w16:
        w16["gate"], w16["up"] = (gu[1], 0, 0), (gu[2], 0, 0)
    (h3,) = _matmul([(act,) + w["down"]], d, [F32], residual=h2, bm=FFN_DOWN_BLOCK[0], bn=FFN_DOWN_BLOCK[1],
                    name="ffn_down")
    y = _rmsnorm(h3, norms["final"], F32)
    return y.reshape(nb, t, d), dk32, dv32, s_new, w16


def kernel(x_prompt, x_sample, mem_prompt, cache_diff_k, cache_diff_v, state_hgrn, cache_mem_k, cache_mem_v, norm_mix, w_in, diff_lambda, diff_subln, rel_bias, hgrn_lb_logits, hgrn_norm, w_out, norm_cross, norm_mem, w_mem_q, w_mem_k, w_mem_v, w_mem_o, norm_ffn, w_ffn_gate, w_ffn_up, w_ffn_down, norm_final):
    bp, tp, d = x_prompt.shape
    bs, ts, _ = x_sample.shape
    depth = w_in.shape[0]
    assert depth == 1 and bp == 1
    past = cache_diff_k.shape[2]
    n_dh = cache_diff_k.shape[3]
    d_diff = n_dh * HEAD_DIM
    n_hh = state_hgrn.shape[2]
    d_hgrn = n_hh * HEAD_DIM
    assert past % CHUNK == 0 and ts == CHUNK and tp % ATTN_TILE == 0

    norms = {"mix": norm_mix[0], "cross": norm_cross[0], "ffn": norm_ffn[0], "final": norm_final}
    widths = {"diff": d_diff, "hgrn_in": w_in.shape[2] - 3 * d_diff}
    w_f32 = {
        "dq": (w_in[0], 0, 0), "dk": (w_in[0], 0, d_diff), "dv": (w_in[0], 0, 2 * d_diff),
        "h": (w_in[0], 0, 3 * d_diff),
        "out_diff": (w_out[0], 0, 0), "out_hgrn": (w_out[0], d_diff, 0),
        "mem_q": (w_mem_q[0], 0, 0), "mem_o": (w_mem_o[0], 0, 0),
        "gate": (w_ffn_gate[0], 0, 0), "up": (w_ffn_up[0], 0, 0),
        "down": (w_ffn_down[0].astype(BF16), 0, 0),
    }
    lam_p = diff_lambda[0].astype(F32)
    subln = diff_subln[0].reshape(1, HEAD_DIM)
    hnorm = hgrn_norm[0].reshape(1, HEAD_DIM)

    tile = ATTN_TILE
    r = jnp.arange(tile)[:, None]
    c = jnp.arange(2 * tile)[None, :]
    rel_p = (c - tile) - r
    vis_p = jnp.logical_or(c < tile, (c - tile) // CHUNK <= r // CHUNK)
    bias_p = _bias_tiles(rel_bias, _rel_bucket(rel_p), jnp.where(vis_p, 0.0, MASK_VALUE).astype(F32),
                         _rel_bucket(jnp.int32(-2 * tile)), n_dh)
    qpos = past + jnp.arange(ts)[:, None]
    kpos = (past - SAMPLE_NEAR) + jnp.arange(SAMPLE_NEAR + ts)[None, :]
    vis_s = kpos // CHUNK <= qpos // CHUNK
    bias_s = _bias_tiles(rel_bias, _rel_bucket(kpos - qpos), jnp.where(vis_s, 0.0, MASK_VALUE).astype(F32),
                         _rel_bucket(jnp.int32(-(SAMPLE_NEAR + 1))), n_dh)

    def mixer_prompt(dq, dk16, dv16, hpart):
        cat_diff = _attn_prompt(lam_p, subln, dq, dk16, dv16, bias_p, n_dh)
        s0 = jnp.zeros((bp, n_hh, HEAD_DIM, HEAD_DIM), F32)
        cat_hgrn, s_new = _hgrn(hgrn_lb_logits, hnorm, hpart.reshape(bp, tp, 4 * d_hgrn), s0, n_hh)
        return cat_diff, cat_hgrn.reshape(bp * tp, d_hgrn), s_new

    mem_out = {}

    def mem_kv_prompt():
        mn = _rmsnorm(mem_prompt.reshape(bp * N_MEM, d), norm_mem[0], BF16)
        bm, bn = MM_BLOCK_F32_WEIGHTS
        (mk,) = _matmul([(mn, w_mem_k[0], 0, 0)], d, [F32], bm=bm, bn=bn, name="proj_mem_k")
        (mv,) = _matmul([(mn, w_mem_v[0], 0, 0)], d, [F32], bm=bm, bn=bn, name="proj_mem_v")
        mem_out["k"], mem_out["v"] = mk, mv
        return mk.reshape(bp, N_MEM, d), mv.reshape(bp, N_MEM, d)

    def mixer_sample(dq, dk16, dv16, hpart):
        shp = (bs, ts, d_diff)
        cat_diff = _attn_sample(lam_p, subln, dq.reshape(shp), dk16.reshape(shp), dv16.reshape(shp),
                                cache_diff_k[0].reshape(bs, past * n_dh, HEAD_DIM),
                                cache_diff_v[0].reshape(bs, past * n_dh, HEAD_DIM), bias_s, n_dh)
        cat_hgrn, s_new = _hgrn(hgrn_lb_logits, hnorm, hpart.reshape(bs, ts, 4 * d_hgrn), state_hgrn[0], n_hh)
        return cat_diff.reshape(bs * ts, d_diff), cat_hgrn.reshape(bs * ts, d_hgrn), s_new

    def mem_kv_sample():
        return cache_mem_k[0], cache_mem_v[0]

    y_s, dk_s, dv_s, s_s, w_bf16 = _trunk(x_sample, norms, w_f32, widths, mixer_sample, mem_kv_sample,
                                          blocks=MM_BLOCK_F32_WEIGHTS, emit_w16=True)
    w_bf16["down"] = w_f32["down"]
    y_p, dk_p, dv_p, s_p, _ = _trunk(x_prompt, norms, w_bf16, widths, mixer_prompt, mem_kv_prompt,
                                     blocks=MM_BLOCK_BF16_WEIGHTS, emit_w16=False)

    mhd = d // MEM_HEADS
    return (y_p, y_s,
            dk_p.reshape(depth, bp, tp, n_dh, HEAD_DIM), dv_p.reshape(depth, bp, tp, n_dh, HEAD_DIM),
            s_p.reshape(depth, bp, n_hh, HEAD_DIM, HEAD_DIM),
            mem_out["k"].reshape(depth, bp, N_MEM, MEM_HEADS, mhd),
            mem_out["v"].reshape(depth, bp, N_MEM, MEM_HEADS, mhd),
            dk_s.reshape(depth, bs, ts, n_dh, HEAD_DIM), dv_s.reshape(depth, bs, ts, n_dh, HEAD_DIM),
            s_s.reshape(depth, bs, n_hh, HEAD_DIM, HEAD_DIM))
```

```python
import functools
import math

import jax
import jax.numpy as jnp
from jax import lax
from jax.experimental import pallas as pl
from jax.experimental.pallas import tpu as pltpu

F32 = jnp.float32
BF16 = jnp.bfloat16

CHUNK = 64
HEAD_DIM = 128
DIFF_QK = HEAD_DIM // 2
N_MEM = 256
MEM_HEADS = 4
REL_BUCKETS = 32
REL_MAX_DIST = 128
EPS = 1e-6
LAM_INIT_L0 = 0.8 - 0.6 * math.exp(-0.3 * 0)
MASK_VALUE = -1e30
LOG2E = math.log2(math.e)

V7X_LANES = 128
V7X_SCOPED_VMEM_LIMIT_BYTES = 60000 * 1024
MM_BLOCK_F32_WEIGHTS = (2048, 256)
MM_BLOCK_BF16_WEIGHTS = (1024, 1024)
FFN_GATE_UP_BLOCK = (2048, 256)
FFN_DOWN_BLOCK = (512, 512)

DIAG_SLOT = 2
ATTN_TILE = 512
SAMPLE_CACHE_PARTS = 4
FUSED_PROJ_BLOCK = (1024, 512)
HGRN_CHUNKS_PER_STEP = 4
SAMPLE_NEAR = 128


def _cparams(*sem):
    return pltpu.CompilerParams(dimension_semantics=sem,
                                vmem_limit_bytes=V7X_SCOPED_VMEM_LIMIT_BYTES)


def _rmsnorm_kernel(x_ref, g_ref, o_ref):
    x = x_ref[...]
    ms = jnp.mean(x * x, axis=-1, keepdims=True)
    o_ref[...] = (x * lax.rsqrt(ms + EPS) * g_ref[...]).astype(o_ref.dtype)


def _rmsnorm(x, g, out_dtype, rows=256):
    m, d = x.shape
    rows = min(rows, m)
    return pl.pallas_call(
        _rmsnorm_kernel,
        grid=(m // rows,),
        in_specs=[pl.BlockSpec((rows, d), lambda i: (i, 0)),
                  pl.BlockSpec((1, d), lambda i: (0, 0))],
        out_specs=pl.BlockSpec((rows, d), lambda i: (i, 0)),
        out_shape=jax.ShapeDtypeStruct((m, d), out_dtype),
        compiler_params=_cparams("parallel"),
        name="rmsnorm",
    )(x, g.reshape(1, d))


def _mm_kernel(*refs, n_pairs, has_res, n_out, scale):
    xs = refs[0:2 * n_pairs:2]
    ws = refs[1:2 * n_pairs:2]
    pos = 2 * n_pairs
    res_ref = refs[pos] if has_res else None
    pos += int(has_res)
    outs = refs[pos:pos + n_out]
    w16_outs = refs[pos + n_out:]

    total = None
    for p, (x_ref, w_ref) in enumerate(zip(xs, ws)):
        wb = w_ref[...].astype(BF16)
        if w16_outs:
            w16_outs[p][...] = wb
        d = jnp.dot(x_ref[...], wb, preferred_element_type=F32)
        total = d if total is None else total + d
    if scale != 1.0:
        total = total * scale
    if has_res:
        total = total + res_ref[...]
    for o_ref in outs:
        o_ref[...] = total.astype(o_ref.dtype)


def _matmul(pairs, n, out_dtypes, *, residual=None, scale=1.0, bm, bn, emit_w16=False, name="matmul"):
    m = pairs[0][0].shape[0]
    bm = min(bm, m)
    bn = min(bn, n)
    assert m % bm == 0 and n % bn == 0
    in_specs, args = [], []
    for x, w, r0, c0 in pairs:
        kx = x.shape[1]
        assert r0 % kx == 0 and c0 % bn == 0
        in_specs.append(pl.BlockSpec((bm, kx), lambda j, i: (i, 0)))
        in_specs.append(pl.BlockSpec((kx, bn), lambda j, i, rb=r0 // kx, cb=c0 // bn: (rb, cb + j)))
        args += [x, w]
    if residual is not None:
        in_specs.append(pl.BlockSpec((bm, bn), lambda j, i: (i, j)))
        args.append(residual)
    out_specs = [pl.BlockSpec((bm, bn), lambda j, i: (i, j)) for _ in out_dtypes]
    out_shape = [jax.ShapeDtypeStruct((m, n), dt) for dt in out_dtypes]
    if emit_w16:
        for x, _, _, _ in pairs:
            out_specs.append(pl.BlockSpec((x.shape[1], bn), lambda j, i: (0, j)))
            out_shape.append(jax.ShapeDtypeStruct((x.shape[1], n), BF16))
    kern = functools.partial(_mm_kernel, n_pairs=len(pairs), has_res=residual is not None,
                             n_out=len(out_dtypes), scale=scale)
    return pl.pallas_call(
        kern,
        grid=(n // bn, m // bm),
        in_specs=in_specs,
        out_specs=out_specs,
        out_shape=out_shape,
        compiler_params=_cparams("parallel", "arbitrary"),
        name=name,
    )(*args)


def _gateup_kernel(x_ref, wg_ref, wu_ref, o_ref, *w16_outs):
    x = x_ref[...]
    wg = wg_ref[...].astype(BF16)
    wu = wu_ref[...].astype(BF16)
    if w16_outs:
        w16_outs[0][...] = wg
        w16_outs[1][...] = wu
    g = jnp.dot(x, wg, preferred_element_type=F32)
    u = jnp.dot(x, wu, preferred_element_type=F32)
    o_ref[...] = (_silu(g) * u).astype(o_ref.dtype)


def _gateup(x, wg, wu, *, bm, bn, emit_w16=False):
    m, kdim = x.shape
    n = wg.shape[1]
    bm = min(bm, m)
    assert m % bm == 0 and n % bn == 0
    w_spec = pl.BlockSpec((kdim, bn), lambda j, i: (0, j))
    out_specs = [pl.BlockSpec((bm, bn), lambda j, i: (i, j))]
    out_shape = [jax.ShapeDtypeStruct((m, n), BF16)]
    if emit_w16:
        out_specs += [w_spec, w_spec]
        out_shape += [jax.ShapeDtypeStruct((kdim, n), BF16)] * 2
    return pl.pallas_call(
        _gateup_kernel,
        grid=(n // bn, m // bm),
        in_specs=[pl.BlockSpec((bm, kdim), lambda j, i: (i, 0)), w_spec, w_spec],
        out_specs=out_specs,
        out_shape=out_shape,
        compiler_params=_cparams("parallel", "arbitrary"),
        name="ffn_gate_up",
    )(x, wg, wu)


def _rel_bucket(rel):
    nb = REL_BUCKETS // 2
    max_exact = nb // 2
    ret = jnp.where(rel > 0, nb, 0)
    n = jnp.abs(rel)
    nf = jnp.maximum(n, 1).astype(F32)
    large = max_exact + (jnp.log(nf / max_exact) / math.log(REL_MAX_DIST / max_exact)
                         * (nb - max_exact)).astype(jnp.int32)
    large = jnp.minimum(large, nb - 1)
    return ret + jnp.where(n < max_exact, n, large)


def _bias_tile_kernel(far_ref, table_ref, bucket_ref, mask_ref, o_ref):
    h = pl.program_id(0)
    bucket = bucket_ref[...]
    acc = jnp.zeros(bucket.shape, F32)
    for b in range(REL_BUCKETS):
        acc = jnp.where(bucket == b, table_ref[b, h], acc)
    o_ref[0] = (acc - table_ref[far_ref[0], h]) * LOG2E + mask_ref[...]


def _bias_tiles(table, bucket, maskadd, far_bucket, n_heads):
    r, c = bucket.shape
    return pl.pallas_call(
        _bias_tile_kernel,
        grid=(n_heads,),
        in_specs=[pl.BlockSpec(memory_space=pltpu.SMEM),
                  pl.BlockSpec(memory_space=pltpu.SMEM),
                  pl.BlockSpec((r, c), lambda h: (0, 0)),
                  pl.BlockSpec((r, c), lambda h: (0, 0))],
        out_specs=pl.BlockSpec((1, r, c), lambda h: (h, 0, 0)),
        out_shape=jax.ShapeDtypeStruct((n_heads, r, c), F32),
        compiler_params=_cparams("arbitrary"),
        name="rel_bias_tiles",
    )(far_bucket.reshape(1).astype(jnp.int32), table.astype(F32), bucket.astype(jnp.int32), maskadd)


def _diff_lambda(lp):
    a = jnp.sum(lp[0:1, :] * lp[1:2, :], axis=-1, keepdims=True)
    b = jnp.sum(lp[2:3, :] * lp[3:4, :], axis=-1, keepdims=True)
    return jnp.exp(a) - jnp.exp(b) + LAM_INIT_L0


def _stack_maps(q):
    lane = lax.broadcasted_iota(jnp.int32, q.shape, 1)
    zero = jnp.zeros_like(q)
    return jnp.concatenate([jnp.where(lane < DIFF_QK, q, zero),
                            jnp.where(lane >= DIFF_QK, q, zero)], axis=0)


def _diff_epilogue(acc, l, lam, g, tq):
    o = acc * (1.0 / l)
    o = o[:tq] - lam * o[tq:]
    ms = jnp.mean(o * o, axis=-1, keepdims=True)
    return o * lax.rsqrt(ms + EPS) * g * (1.0 - LAM_INIT_L0)


def _attn_prompt_kernel(lam_ref, g_ref, q_ref, qn_ref, k_ref, v_ref, bias_ref, o_ref, qs_ref, s_ref, m_ref,
                        acc_ref, *, tile):
    i = pl.program_id(1)
    n_lane_tiles = tile // V7X_LANES
    ones = jnp.ones((tile, V7X_LANES), BF16)
    nt = (((1,), (1,)), ((), ()))

    qs_ref[...] = _stack_maps(q_ref[...])
    m_ref[...] = jnp.full(m_ref.shape, MASK_VALUE, F32)
    acc_ref[...] = jnp.zeros(acc_ref.shape, F32)

    def key_rows(d):
        return pl.ds(pl.multiple_of((i - d) * tile, tile), tile)

    def scores(d, slot):
        s_ref[slot] = lax.dot_general(qs_ref[...], k_ref[key_rows(d), :], nt, preferred_element_type=F32)

    def absorb(d, slot, bias):
        vt = jnp.concatenate([v_ref[key_rows(d), :], ones], axis=1)
        s = s_ref[slot]
        if bias is not None:
            s = (s.reshape(2, tile, tile) + bias[None]).reshape(2 * tile, tile)
        cols = [s[:, c * V7X_LANES:(c + 1) * V7X_LANES] for c in range(n_lane_tiles)]
        mx = cols[0]
        for sc in cols[1:]:
            mx = jnp.maximum(mx, sc)
        m_prev = m_ref[...]
        m_new = jnp.maximum(m_prev, jnp.max(mx, axis=-1, keepdims=True))
        alpha = jnp.exp2(m_prev - m_new)
        p = jnp.concatenate([jnp.exp2(sc - m_new).astype(BF16) for sc in cols], axis=1)
        pv = jnp.dot(p, vt, preferred_element_type=F32)
        acc_ref[...] = jnp.concatenate([alpha, alpha], axis=1) * acc_ref[...] + pv
        m_ref[...] = m_new

    i_even = lax.rem(i, 2) == 0

    @pl.when(i == 0)
    def _():
        scores(0, DIAG_SLOT)

    @pl.when(jnp.logical_and(i_even, i >= 2))
    def _():
        scores(i - 1, 1)
        absorb(i, 0, None)

    d_odd = i - 1 + lax.rem(i, 2)

    def pair_body(u, carry):
        d = d_odd - 2 * u
        scores(d - 1, 0)
        absorb(d, 1, None)
        scores(d - 2, 1)
        absorb(d - 1, 0, None)
        return carry

    lax.fori_loop(0, jnp.where(d_odd >= 3, (d_odd - 1) // 2, 0), pair_body, 0)

    @pl.when(i >= 1)
    def _():
        scores(0, DIAG_SLOT)
        absorb(1, 1, bias_ref[0, :, :tile])

    def diagonal(next_slot):
        if next_slot is not None:
            s_ref[next_slot] = lax.dot_general(_stack_maps(qn_ref[...]), k_ref[pl.ds(0, tile), :], nt,
                                               preferred_element_type=F32)
        absorb(0, DIAG_SLOT, bias_ref[0, :, tile:])

    has_next = i + 1 < pl.num_programs(1)

    @pl.when(jnp.logical_and(has_next, jnp.logical_not(i_even)))
    def _():
        diagonal(0)

    @pl.when(jnp.logical_and(has_next, i_even))
    def _():
        diagonal(1)

    @pl.when(jnp.logical_not(has_next))
    def _():
        diagonal(None)

    acc = acc_ref[...]
    y = _diff_epilogue(acc[:, :HEAD_DIM], acc[:, HEAD_DIM:], _diff_lambda(lam_ref[...]), g_ref[...], tile)
    o_ref[...] = y.astype(o_ref.dtype)


def _attn_prompt(lam_p, g, q, k, v, bias, n_heads, tile=ATTN_TILE):
    t = q.shape[0]
    nq = t // tile
    kern = functools.partial(_attn_prompt_kernel, tile=tile)
    return pl.pallas_call(
        kern,
        grid=(n_heads, nq),
        in_specs=[pl.BlockSpec((4, DIFF_QK), lambda h, i: (0, 0)),
                  pl.BlockSpec((1, HEAD_DIM), lambda h, i: (0, 0)),
                  pl.BlockSpec((tile, HEAD_DIM), lambda h, i: (i, h)),
                  pl.BlockSpec((tile, HEAD_DIM), lambda h, i: (jnp.minimum(i + 1, nq - 1), h)),
                  pl.BlockSpec((t, HEAD_DIM), lambda h, i: (0, h)),
                  pl.BlockSpec((t, HEAD_DIM), lambda h, i: (0, h)),
                  pl.BlockSpec((1, tile, 2 * tile), lambda h, i: (h, 0, 0))],
        out_specs=pl.BlockSpec((tile, HEAD_DIM), lambda h, i: (i, h)),
        out_shape=jax.ShapeDtypeStruct((t, n_heads * HEAD_DIM), BF16),
        scratch_shapes=[pltpu.VMEM((2 * tile, HEAD_DIM), BF16),
                        pltpu.VMEM((DIAG_SLOT + 1, 2 * tile, tile), F32),
                        pltpu.VMEM((2 * tile, V7X_LANES), F32),
                        pltpu.VMEM((2 * tile, 2 * HEAD_DIM), F32)],
        compiler_params=_cparams("parallel", "arbitrary"),
        name="diff_attn_prompt",
    )(lam_p, g, q, q, k, v, bias)


def _attn_sample_step(part, n_parts, lam_ref, g_ref, q_ref, kn_ref, vn_ref, kc_ref, vc_ref, bias_ref, o_ref,
                      m_ref, acc_ref, *, n_heads, keys, ts):
    is_last = part == n_parts - 1
    n_lane_tiles = keys // V7X_LANES
    ones_c = jnp.ones((keys, V7X_LANES), BF16)
    ones_n = jnp.ones((ts, V7X_LANES), BF16)
    nt = (((1,), (1,)), ((), ()))

    @pl.when(part == 0)
    def _():
        m_ref[...] = jnp.full(m_ref.shape, MASK_VALUE, F32)
        acc_ref[...] = jnp.zeros(acc_ref.shape, F32)

    for h in range(n_heads):
        col = slice(h * HEAD_DIM, (h + 1) * HEAD_DIM)
        rows = pl.ds(h, keys, stride=n_heads)
        qs = _stack_maps(q_ref[0, :, col])
        kc = kc_ref[0, rows, :].astype(BF16)
        vc = jnp.concatenate([vc_ref[0, rows, :].astype(BF16), ones_c], axis=1)
        vn = jnp.concatenate([vn_ref[0, :, col], ones_n], axis=1)
        bias = bias_ref[h]
        near_bias = jnp.where(is_last, bias[:, :SAMPLE_NEAR], 0.0)
        new_bias = jnp.where(is_last, bias[:, SAMPLE_NEAR:], MASK_VALUE)

        s = lax.dot_general(qs, kc, nt, preferred_element_type=F32)
        cols = [s[:, c * V7X_LANES:(c + 1) * V7X_LANES] for c in range(n_lane_tiles)]
        cols[-1] = (cols[-1].reshape(2, ts, SAMPLE_NEAR) + near_bias[None]).reshape(2 * ts, SAMPLE_NEAR)
        s_new = lax.dot_general(qs, kn_ref[0, :, col], nt, preferred_element_type=F32)
        s_new = (s_new.reshape(2, ts, ts) + new_bias[None]).reshape(2 * ts, ts)

        mx = cols[0]
        for sc in cols[1:]:
            mx = jnp.maximum(mx, sc)
        m_prev = m_ref[h]
        m_new = jnp.maximum(m_prev, jnp.maximum(jnp.max(mx, axis=-1, keepdims=True),
                                                jnp.max(s_new, axis=-1, keepdims=True)))
        alpha = jnp.exp2(m_prev - m_new)
        p = jnp.concatenate([jnp.exp2(sc - m_new).astype(BF16) for sc in cols], axis=1)
        p_new = jnp.exp2(s_new - m_new[:, :ts]).astype(BF16)
        pv = (jnp.dot(p, vc, preferred_element_type=F32)
              + jnp.dot(p_new, vn, preferred_element_type=F32))
        acc_ref[h] = jnp.concatenate([alpha, alpha], axis=1) * acc_ref[h] + pv
        m_ref[h] = m_new

    @pl.when(is_last)
    def _():
        lam = _diff_lambda(lam_ref[...])
        g = g_ref[...]
        for h in range(n_heads):
            acc = acc_ref[h]
            y = _diff_epilogue(acc[:, :HEAD_DIM], acc[:, HEAD_DIM:], lam, g, ts)
            o_ref[0, :, h * HEAD_DIM:(h + 1) * HEAD_DIM] = y.astype(o_ref.dtype)


def _proj_and_attn_sample_kernel(x_ref, w_ref, lam_ref, g_ref, q_ref, kn_ref, vn_ref, kc_ref, vc_ref, bias_ref,
                                 y_ref, o_ref, m_ref, acc_ref, *, parts, n_heads, keys, ts):
    y_ref[...] = jnp.dot(x_ref[...], w_ref[...], preferred_element_type=F32)
    part = lax.rem(pl.program_id(0), parts)
    _attn_sample_step(part, parts, lam_ref, g_ref, q_ref, kn_ref, vn_ref, kc_ref, vc_ref, bias_ref, o_ref,
                      m_ref, acc_ref, n_heads=n_heads, keys=keys, ts=ts)


def _proj_and_attn_sample(x, w16, lam_p, g, q, kn, vn, kc, vc, bias, n_heads):
    m, kdim = x.shape
    n = w16.shape[1]
    bm, bn = FUSED_PROJ_BLOCK
    nm, nn = m // bm, n // bn
    nb, ts, d = q.shape
    past = kc.shape[1] // n_heads
    parts = SAMPLE_CACHE_PARTS
    keys = past // parts
    assert m % bm == 0 and n % bn == 0 and nm * nn == nb * parts
    assert past % parts == 0 and keys % V7X_LANES == 0 and SAMPLE_NEAR == V7X_LANES
    kern = functools.partial(_proj_and_attn_sample_kernel, parts=parts, n_heads=n_heads, keys=keys, ts=ts)
    new = pl.BlockSpec((1, ts, d), lambda s: (s // parts, 0, 0))
    cache = pl.BlockSpec((1, keys * n_heads, HEAD_DIM), lambda s: (s // parts, s % parts, 0))
    return pl.pallas_call(
        kern,
        grid=(nb * parts,),
        in_specs=[pl.BlockSpec((bm, kdim), lambda s: (s // nn, 0)),
                  pl.BlockSpec((kdim, bn), lambda s: (0, s % nn)),
                  pl.BlockSpec((4, DIFF_QK), lambda s: (0, 0)),
                  pl.BlockSpec((1, HEAD_DIM), lambda s: (0, 0)),
                  new, new, new, cache, cache,
                  pl.BlockSpec((n_heads, ts, SAMPLE_NEAR + ts), lambda s: (0, 0, 0))],
        out_specs=[pl.BlockSpec((bm, bn), lambda s: (s // nn, s % nn)), new],
        out_shape=[jax.ShapeDtypeStruct((m, n), F32),
                   jax.ShapeDtypeStruct((nb, ts, d), BF16)],
        scratch_shapes=[pltpu.VMEM((n_heads, 2 * ts, V7X_LANES), F32),
                        pltpu.VMEM((n_heads, 2 * ts, 2 * HEAD_DIM), F32)],
        compiler_params=_cparams("arbitrary"),
        name="proj_hgrn_prompt_and_diff_attn_sample",
    )(x, w16, lam_p, g, q, kn, vn, kc, vc, bias)


def _split3(x):
    hi = x.astype(BF16)
    r1 = x - hi.astype(F32)
    mid = r1.astype(BF16)
    lo = (r1 - mid.astype(F32)).astype(BF16)
    return hi, mid, lo


def _silu(x):
    h = 0.5 * x
    return h + h * jnp.tanh(h)


def _hgrn_kernel(lbl_ref, g_ref, hq_ref, hf_ref, hi_ref, hg_ref, s0_ref, o_ref, s_out_ref, st_ref, *,
                 n_heads, chain):
    c = pl.program_id(1)
    nc = pl.num_programs(1)
    n_streams = st_ref.shape[0]
    n_chunks = HGRN_CHUNKS_PER_STEP
    rows = n_chunks * CHUNK
    d = n_heads * HEAD_DIM

    @pl.when(c == 0)
    def _():
        for s in range(n_streams):
            for h in range(n_heads):
                st_ref[s, h] = s0_ref[s, h].T

    lg = lbl_ref[...]
    e = jnp.exp(lg - jnp.max(lg, axis=0, keepdims=True))
    lb = e[0:1, :] / jnp.sum(e, axis=0, keepdims=True)

    c_half = 0.5 * (1.0 - lb)
    t = jnp.tanh(0.5 * hf_ref[...].reshape(rows, d))
    ct = c_half * t
    logf = jnp.log2((lb + c_half) + ct)
    kk = c_half - ct
    qq = _silu(hq_ref[...].reshape(rows, d))

    row = lax.broadcasted_iota(jnp.int32, (rows, rows), 0)
    col = lax.broadcasted_iota(jnp.int32, (rows, rows), 1)
    causal = jnp.logical_and(col <= row, col // CHUNK == row // CHUNK)
    tri = jnp.where(causal, 1.0, 0.0).astype(BF16)
    b = None
    for part in _split3(logf):
        term = jnp.dot(tri, part, preferred_element_type=F32)
        b = term if b is None else b + term
    qd = (qq * jnp.exp2(b)).astype(BF16)
    kd32 = kk * jnp.exp2(-b)
    kd = kd32.astype(BF16)
    decays = [jnp.exp2(b[(r + 1) * CHUNK - 1:(r + 1) * CHUNK, :]) for r in range(n_chunks)]
    k_last = jnp.concatenate([kd32[r * CHUNK:(r + 1) * CHUNK] * decays[r] for r in range(n_chunks)],
                             axis=0).astype(BF16)
    v = hi_ref[...].reshape(rows, d).astype(BF16)
    gate = _silu(hg_ref[...].reshape(rows, d))
    g = g_ref[...]

    nt = (((1,), (1,)), ((), ()))
    tn = (((0,), (0,)), ((), ()))
    for h in range(n_heads):
        cols = slice(h * HEAD_DIM, (h + 1) * HEAD_DIM)
        a = lax.dot_general(qd[:, cols], kd[:, cols], nt, preferred_element_type=F32)
        a = jnp.where(causal, a, 0.0)
        o_intra = jnp.dot(a.astype(BF16), v[:, cols], preferred_element_type=F32)
        o_parts = []
        st = st_ref[0, h]
        for r in range(n_chunks):
            rr = slice(r * CHUNK, (r + 1) * CHUNK)
            if not chain:
                st = st_ref[r, h]
            o_parts.append(o_intra[rr] + lax.dot_general(qd[rr, cols], st.astype(BF16), nt,
                                                         preferred_element_type=F32))
            st = decays[r][:, cols] * st + lax.dot_general(v[rr, cols], k_last[rr, cols], tn,
                                                           preferred_element_type=F32)
            if not chain:
                st_ref[r, h] = st
        if chain:
            st_ref[0, h] = st
        o = jnp.concatenate(o_parts, axis=0)
        ms = jnp.mean(o * o, axis=-1, keepdims=True)
        y = (o * lax.rsqrt(ms + EPS) * g * gate[:, cols]).astype(o_ref.dtype)
        if chain:
            o_ref[0, :, cols] = y
        else:
            for r in range(n_chunks):
                o_ref[r, :, cols] = y[r * CHUNK:(r + 1) * CHUNK]

    @pl.when(c == nc - 1)
    def _():
        for s in range(n_streams):
            for h in range(n_heads):
                s_out_ref[s, h] = st_ref[s, h].T


def _hgrn(lb_logits, g, hpart, s0, n_heads):
    nb, t, _ = hpart.shape
    d = n_heads * HEAD_DIM
    per = HGRN_CHUNKS_PER_STEP
    chain = t > CHUNK
    sb, tb = (1, per * CHUNK) if chain else (per, CHUNK)
    assert nb % sb == 0 and t % tb == 0
    kern = functools.partial(_hgrn_kernel, n_heads=n_heads, chain=chain)
    part = lambda p: pl.BlockSpec((sb, tb, d), lambda b, c: (b, c, p))
    state = pl.BlockSpec((sb, n_heads, HEAD_DIM, HEAD_DIM), lambda b, c: (b, 0, 0, 0))
    return pl.pallas_call(
        kern,
        grid=(nb // sb, t // tb),
        in_specs=[pl.BlockSpec(lb_logits.shape, lambda b, c: (0, 0)),
                  pl.BlockSpec((1, HEAD_DIM), lambda b, c: (0, 0)),
                  part(0), part(1), part(2), part(3), state],
        out_specs=[pl.BlockSpec((sb, tb, d), lambda b, c: (b, c, 0)), state],
        out_shape=[jax.ShapeDtypeStruct((nb, t, d), BF16),
                   jax.ShapeDtypeStruct(s0.shape, F32)],
        scratch_shapes=[pltpu.VMEM((sb, n_heads, HEAD_DIM, HEAD_DIM), F32)],
        compiler_params=_cparams("parallel", "arbitrary"),
        name="hgrn2_chunks",
    )(lb_logits, g, hpart, hpart, hpart, hpart, s0)


def _cross_kernel(q_ref, k_ref, v_ref, o_ref, *, heads, hd, split_heads):
    nt = (((1,), (1,)), ((), ()))
    if split_heads:
        tq = q_ref.shape[1]
        n_mem = k_ref.shape[1]
        xk = k_ref[0].reshape(n_mem * heads, hd).astype(BF16)
        xv = v_ref[0].reshape(n_mem * heads, hd).astype(BF16)
        q = jnp.concatenate([q_ref[0, :, h * hd:(h + 1) * hd] for h in range(heads)], axis=0)
        s = lax.dot_general(q, xk, nt, preferred_element_type=F32)
        row_head = lax.broadcasted_iota(jnp.int32, s.shape, 0) // tq
        col_head = lax.rem(lax.broadcasted_iota(jnp.int32, s.shape, 1), heads)
        s = jnp.where(row_head == col_head, s, MASK_VALUE)
        p = jnp.exp(s - jnp.max(s, axis=-1, keepdims=True))
        l = jnp.sum(p, axis=-1, keepdims=True)
        o = jnp.dot(p.astype(BF16), xv, preferred_element_type=F32) * (1.0 / l)
        for h in range(heads):
            o_ref[0, :, h * hd:(h + 1) * hd] = o[h * tq:(h + 1) * tq].astype(o_ref.dtype)
        return
    for h in range(heads):
        cols = slice(h * hd, (h + 1) * hd)
        q = q_ref[0, :, cols]
        k = k_ref[0, :, cols].astype(BF16)
        v = v_ref[0, :, cols].astype(BF16)
        s = lax.dot_general(q, k, nt, preferred_element_type=F32)
        p = jnp.exp(s - jnp.max(s, axis=-1, keepdims=True))
        l = jnp.sum(p, axis=-1, keepdims=True)
        o = jnp.dot(p.astype(BF16), v, preferred_element_type=F32) * (1.0 / l)
        o_ref[0, :, cols] = o.astype(o_ref.dtype)


def _cross_attn(q, mk, mv, tq):
    nb, t, d = q.shape
    split_heads = mk.ndim == 4
    kern = functools.partial(_cross_kernel, heads=MEM_HEADS, hd=d // MEM_HEADS, split_heads=split_heads)
    if split_heads:
        mem_spec = pl.BlockSpec((1,) + mk.shape[1:], lambda b, i: (b, 0, 0, 0))
    else:
        mem_spec = pl.BlockSpec((1, N_MEM, d), lambda b, i: (b, 0, 0))
    return pl.pallas_call(
        kern,
        grid=(nb, t // tq),
        in_specs=[pl.BlockSpec((1, tq, d), lambda b, i: (b, i, 0)),
                  mem_spec,
                  mem_spec],
        out_specs=pl.BlockSpec((1, tq, d), lambda b, i: (b, i, 0)),
        out_shape=jax.ShapeDtypeStruct((nb, t, d), BF16),
        compiler_params=_cparams("parallel", "parallel"),
        name="mem_cross_attn",
    )(q, mk, mv)


def _named_matmul(w, blocks, emit_w16, w16):
    bm, bn = blocks

    def mm(names, xs, n, out_dtypes, **kw):
        pairs = [(xi,) + w[nm] for nm, xi in zip(names, xs)]
        outs = _matmul(pairs, n, out_dtypes, bm=bm, bn=bn, emit_w16=emit_w16, **kw)
        if emit_w16:
            for nm, arr in zip(names, outs[len(out_dtypes):]):
                w16[nm] = (arr, 0, 0)
        return outs[:len(out_dtypes)]

    return mm


def _in_proj(x2, norms, w, widths, *, blocks, emit_w16, with_hgrn):
    w16 = {}
    mm = _named_matmul(w, blocks, emit_w16, w16)
    n = _rmsnorm(x2, norms["mix"], BF16)
    (dq,) = mm(["dq"], [n], widths["diff"], [BF16], scale=DIFF_QK ** -0.5 * LOG2E, name="proj_dq")
    dk32, dk16 = mm(["dk"], [n], widths["diff"], [F32, BF16], name="proj_dk")
    dv32, dv16 = mm(["dv"], [n], widths["diff"], [F32, BF16], name="proj_dv")
    hpart = mm(["h"], [n], widths["hgrn_in"], [F32], name="proj_hgrn")[0] if with_hgrn else None
    return n, dq, dk32, dk16, dv32, dv16, hpart, w16


def _tail(x2, nb, t, cat_diff, cat_hgrn, norms, w, mem_kv_fn, *, blocks, emit_w16):
    m, d = x2.shape
    w16 = {}
    mm = _named_matmul(w, blocks, emit_w16, w16)

    (h1,) = mm(["out_diff", "out_hgrn"], [cat_diff, cat_hgrn], d, [F32], residual=x2, name="proj_out")

    n2 = _rmsnorm(h1, norms["cross"], BF16)
    (cq,) = mm(["mem_q"], [n2], d, [BF16], scale=(d // MEM_HEADS) ** -0.5, name="proj_mem_q")
    mk, mv = mem_kv_fn()
    co = _cross_attn(cq.reshape(nb, t, d), mk, mv, tq=min(t, 512))
    (h2,) = mm(["mem_o"], [co.reshape(m, d)], d, [F32], residual=h1, name="proj_mem_o")

    n3 = _rmsnorm(h2, norms["ffn"], BF16)
    gu = _gateup(n3, w["gate"][0], w["up"][0], bm=FFN_GATE_UP_BLOCK[0], bn=FFN_GATE_UP_BLOCK[1],
                 emit_w16=emit_w16)
    act = gu[0]
    if emit_w16:
        w16["gate"], w16["up"] = (gu[1], 0, 0), (gu[2], 0, 0)
    (h3,) = _matmul([(act,) + w["down"]], d, [F32], residual=h2, bm=FFN_DOWN_BLOCK[0], bn=FFN_DOWN_BLOCK[1],
                    name="ffn_down")
    y = _rmsnorm(h3, norms["final"], F32)
    return y.reshape(nb, t, d), w16


def kernel(x_prompt, x_sample, mem_prompt, cache_diff_k, cache_diff_v, state_hgrn, cache_mem_k, cache_mem_v, norm_mix, w_in, diff_lambda, diff_subln, rel_bias, hgrn_lb_logits, hgrn_norm, w_out, norm_cross, norm_mem, w_mem_q, w_mem_k, w_mem_v, w_mem_o, norm_ffn, w_ffn_gate, w_ffn_up, w_ffn_down, norm_final):
    bp, tp, d = x_prompt.shape
    bs, ts, _ = x_sample.shape
    depth = w_in.shape[0]
    assert depth == 1 and bp == 1
    past = cache_diff_k.shape[2]
    n_dh = cache_diff_k.shape[3]
    d_diff = n_dh * HEAD_DIM
    n_hh = state_hgrn.shape[2]
    d_hgrn = n_hh * HEAD_DIM
    assert past % CHUNK == 0 and ts == CHUNK and tp % ATTN_TILE == 0

    norms = {"mix": norm_mix[0], "cross": norm_cross[0], "ffn": norm_ffn[0], "final": norm_final}
    widths = {"diff": d_diff, "hgrn_in": w_in.shape[2] - 3 * d_diff}
    w_f32 = {
        "dq": (w_in[0], 0, 0), "dk": (w_in[0], 0, d_diff), "dv": (w_in[0], 0, 2 * d_diff),
        "h": (w_in[0], 0, 3 * d_diff),
        "out_diff": (w_out[0], 0, 0), "out_hgrn": (w_out[0], d_diff, 0),
        "mem_q": (w_mem_q[0], 0, 0), "mem_o": (w_mem_o[0], 0, 0),
        "gate": (w_ffn_gate[0], 0, 0), "up": (w_ffn_up[0], 0, 0),
        "down": (w_ffn_down[0].astype(BF16), 0, 0),
    }
    lam_p = diff_lambda[0].astype(F32)
    subln = diff_subln[0].reshape(1, HEAD_DIM)
    hnorm = hgrn_norm[0].reshape(1, HEAD_DIM)

    tile = ATTN_TILE
    r = jnp.arange(tile)[:, None]
    c = jnp.arange(2 * tile)[None, :]
    rel_p = (c - tile) - r
    vis_p = jnp.logical_or(c < tile, (c - tile) // CHUNK <= r // CHUNK)
    bias_p = _bias_tiles(rel_bias, _rel_bucket(rel_p), jnp.where(vis_p, 0.0, MASK_VALUE).astype(F32),
                         _rel_bucket(jnp.int32(-2 * tile)), n_dh)
    qpos = past + jnp.arange(ts)[:, None]
    kpos = (past - SAMPLE_NEAR) + jnp.arange(SAMPLE_NEAR + ts)[None, :]
    vis_s = kpos // CHUNK <= qpos // CHUNK
    bias_s = _bias_tiles(rel_bias, _rel_bucket(kpos - qpos), jnp.where(vis_s, 0.0, MASK_VALUE).astype(F32),
                         _rel_bucket(jnp.int32(-(SAMPLE_NEAR + 1))), n_dh)

    mem_out = {}

    def mem_kv_prompt():
        mn = _rmsnorm(mem_prompt.reshape(bp * N_MEM, d), norm_mem[0], BF16)
        bm, bn = MM_BLOCK_F32_WEIGHTS
        (mk,) = _matmul([(mn, w_mem_k[0], 0, 0)], d, [F32], bm=bm, bn=bn, name="proj_mem_k")
        (mv,) = _matmul([(mn, w_mem_v[0], 0, 0)], d, [F32], bm=bm, bn=bn, name="proj_mem_v")
        mem_out["k"], mem_out["v"] = mk, mv
        return mk.reshape(bp, N_MEM, d), mv.reshape(bp, N_MEM, d)

    def mem_kv_sample():
        return cache_mem_k[0], cache_mem_v[0]

    xs2 = x_sample.reshape(bs * ts, d)
    xp2 = x_prompt.reshape(bp * tp, d)
    _, dq_s, dk_s, dk16_s, dv_s, dv16_s, hpart_s, w16_in = _in_proj(
        xs2, norms, w_f32, widths, blocks=MM_BLOCK_F32_WEIGHTS, emit_w16=True, with_hgrn=True)
    n_p, dq_p, dk_p, dk16_p, dv_p, dv16_p, _, _ = _in_proj(
        xp2, norms, w16_in, widths, blocks=MM_BLOCK_BF16_WEIGHTS, emit_w16=False, with_hgrn=False)

    shp = (bs, ts, d_diff)
    hpart_p, cat_diff_s = _proj_and_attn_sample(
        n_p, w16_in["h"][0], lam_p, subln, dq_s.reshape(shp), dk16_s.reshape(shp), dv16_s.reshape(shp),
        cache_diff_k[0].reshape(bs, past * n_dh, HEAD_DIM), cache_diff_v[0].reshape(bs, past * n_dh, HEAD_DIM),
        bias_s, n_dh)

    cat_hgrn_s, s_s = _hgrn(hgrn_lb_logits, hnorm, hpart_s.reshape(bs, ts, 4 * d_hgrn), state_hgrn[0], n_hh)
    y_s, w16_tail = _tail(xs2, bs, ts, cat_diff_s.reshape(bs * ts, d_diff), cat_hgrn_s.reshape(bs * ts, d_hgrn),
                          norms, w_f32, mem_kv_sample, blocks=MM_BLOCK_F32_WEIGHTS, emit_w16=True)

    w_bf16 = {**w16_in, **w16_tail, "down": w_f32["down"]}
    cat_diff_p = _attn_prompt(lam_p, subln, dq_p, dk16_p, dv16_p, bias_p, n_dh)
    s0 = jnp.zeros((bp, n_hh, HEAD_DIM, HEAD_DIM), F32)
    cat_hgrn_p, s_p = _hgrn(hgrn_lb_logits, hnorm, hpart_p.reshape(bp, tp, 4 * d_hgrn), s0, n_hh)
    y_p, _ = _tail(xp2, bp, tp, cat_diff_p, cat_hgrn_p.reshape(bp * tp, d_hgrn), norms, w_bf16, mem_kv_prompt,
                   blocks=MM_BLOCK_BF16_WEIGHTS, emit_w16=False)

    mhd = d // MEM_HEADS
    return (y_p, y_s,
            dk_p.reshape(depth, bp, tp, n_dh, HEAD_DIM), dv_p.reshape(depth, bp, tp, n_dh, HEAD_DIM),
            s_p.reshape(depth, bp, n_hh, HEAD_DIM, HEAD_DIM),
            mem_out["k"].reshape(depth, bp, N_MEM, MEM_HEADS, mhd),
            mem_out["v"].reshape(depth, bp, N_MEM, MEM_HEADS, mhd),
            dk_s.reshape(depth, bs, ts, n_dh, HEAD_DIM), dv_s.reshape(depth, bs, ts, n_dh, HEAD_DIM),
            s_s.reshape(depth, bs, n_hh, HEAD_DIM, HEAD_DIM))
```

```python
import functools
import math

import jax
import jax.numpy as jnp
from jax import lax
from jax.experimental import pallas as pl
from jax.experimental.pallas import tpu as pltpu

F32 = jnp.float32
BF16 = jnp.bfloat16

CHUNK = 64
HEAD_DIM = 128
DIFF_QK = HEAD_DIM // 2
N_MEM = 256
MEM_HEADS = 4
REL_BUCKETS = 32
REL_MAX_DIST = 128
EPS = 1e-6
LAM_INIT_L0 = 0.8 - 0.6 * math.exp(-0.3 * 0)
MASK_VALUE = -1e30
LOG2E = math.log2(math.e)

V7X_LANES = 128
V7X_SCOPED_VMEM_LIMIT_BYTES = 60000 * 1024
MM_BLOCK_F32_WEIGHTS = (2048, 256)
MM_BLOCK_BF16_WEIGHTS = (1024, 1024)
FFN_GATE_UP_BLOCK = (2048, 256)
FFN_DOWN_BLOCK = (512, 512)

DIAG_SLOT = 2
ATTN_TILE = 512
SAMPLE_CACHE_PARTS = 4
FUSED_PROJ_BLOCK = (1024, 512)
HGRN_CHUNKS_PER_STEP = 4
SAMPLE_NEAR = 128


def _cparams(*sem):
    return pltpu.CompilerParams(dimension_semantics=sem,
                                vmem_limit_bytes=V7X_SCOPED_VMEM_LIMIT_BYTES)


def _rmsnorm_kernel(x_ref, g_ref, o_ref):
    x = x_ref[...]
    ms = jnp.mean(x * x, axis=-1, keepdims=True)
    o_ref[...] = (x * lax.rsqrt(ms + EPS) * g_ref[...]).astype(o_ref.dtype)


def _rmsnorm(x, g, out_dtype, rows=512):
    m, d = x.shape
    rows = min(rows, m)
    return pl.pallas_call(
        _rmsnorm_kernel,
        grid=(m // rows,),
        in_specs=[pl.BlockSpec((rows, d), lambda i: (i, 0)),
                  pl.BlockSpec((1, d), lambda i: (0, 0))],
        out_specs=pl.BlockSpec((rows, d), lambda i: (i, 0)),
        out_shape=jax.ShapeDtypeStruct((m, d), out_dtype),
        compiler_params=_cparams("parallel"),
        name="rmsnorm",
    )(x, g.reshape(1, d))


def _mm_kernel(*refs, n_pairs, has_res, n_out, scale):
    xs = refs[0:2 * n_pairs:2]
    ws = refs[1:2 * n_pairs:2]
    pos = 2 * n_pairs
    res_ref = refs[pos] if has_res else None
    pos += int(has_res)
    outs = refs[pos:pos + n_out]
    w16_outs = refs[pos + n_out:]

    total = None
    for p, (x_ref, w_ref) in enumerate(zip(xs, ws)):
        wb = w_ref[...].astype(BF16)
        if w16_outs:
            w16_outs[p][...] = wb
        d = jnp.dot(x_ref[...], wb, preferred_element_type=F32)
        total = d if total is None else total + d
    if scale != 1.0:
        total = total * scale
    if has_res:
        total = total + res_ref[...]
    for o_ref in outs:
        o_ref[...] = total.astype(o_ref.dtype)


def _matmul(pairs, n, out_dtypes, *, residual=None, scale=1.0, bm, bn, emit_w16=False, name="matmul"):
    m = pairs[0][0].shape[0]
    bm = min(bm, m)
    bn = min(bn, n)
    assert m % bm == 0 and n % bn == 0
    in_specs, args = [], []
    for x, w, r0, c0 in pairs:
        kx = x.shape[1]
        assert r0 % kx == 0 and c0 % bn == 0
        in_specs.append(pl.BlockSpec((bm, kx), lambda j, i: (i, 0)))
        in_specs.append(pl.BlockSpec((kx, bn), lambda j, i, rb=r0 // kx, cb=c0 // bn: (rb, cb + j)))
        args += [x, w]
    if residual is not None:
        in_specs.append(pl.BlockSpec((bm, bn), lambda j, i: (i, j)))
        args.append(residual)
    out_specs = [pl.BlockSpec((bm, bn), lambda j, i: (i, j)) for _ in out_dtypes]
    out_shape = [jax.ShapeDtypeStruct((m, n), dt) for dt in out_dtypes]
    if emit_w16:
        for x, _, _, _ in pairs:
            out_specs.append(pl.BlockSpec((x.shape[1], bn), lambda j, i: (0, j)))
            out_shape.append(jax.ShapeDtypeStruct((x.shape[1], n), BF16))
    kern = functools.partial(_mm_kernel, n_pairs=len(pairs), has_res=residual is not None,
                             n_out=len(out_dtypes), scale=scale)
    return pl.pallas_call(
        kern,
        grid=(n // bn, m // bm),
        in_specs=in_specs,
        out_specs=out_specs,
        out_shape=out_shape,
        compiler_params=_cparams("parallel", "arbitrary"),
        name=name,
    )(*args)


def _gateup_kernel(x_ref, wg_ref, wu_ref, o_ref, *w16_outs):
    x = x_ref[...]
    wg = wg_ref[...].astype(BF16)
    wu = wu_ref[...].astype(BF16)
    if w16_outs:
        w16_outs[0][...] = wg
        w16_outs[1][...] = wu
    g = jnp.dot(x, wg, preferred_element_type=F32)
    u = jnp.dot(x, wu, preferred_element_type=F32)
    o_ref[...] = (_silu(g) * u).astype(o_ref.dtype)


def _gateup(x, wg, wu, *, bm, bn, emit_w16=False):
    m, kdim = x.shape
    n = wg.shape[1]
    bm = min(bm, m)
    assert m % bm == 0 and n % bn == 0
    w_spec = pl.BlockSpec((kdim, bn), lambda j, i: (0, j))
    out_specs = [pl.BlockSpec((bm, bn), lambda j, i: (i, j))]
    out_shape = [jax.ShapeDtypeStruct((m, n), BF16)]
    if emit_w16:
        out_specs += [w_spec, w_spec]
        out_shape += [jax.ShapeDtypeStruct((kdim, n), BF16)] * 2
    return pl.pallas_call(
        _gateup_kernel,
        grid=(n // bn, m // bm),
        in_specs=[pl.BlockSpec((bm, kdim), lambda j, i: (i, 0)), w_spec, w_spec],
        out_specs=out_specs,
        out_shape=out_shape,
        compiler_params=_cparams("parallel", "arbitrary"),
        name="ffn_gate_up",
    )(x, wg, wu)


def _rel_bucket(rel):
    nb = REL_BUCKETS // 2
    max_exact = nb // 2
    ret = jnp.where(rel > 0, nb, 0)
    n = jnp.abs(rel)
    nf = jnp.maximum(n, 1).astype(F32)
    large = max_exact + (jnp.log(nf / max_exact) / math.log(REL_MAX_DIST / max_exact)
                         * (nb - max_exact)).astype(jnp.int32)
    large = jnp.minimum(large, nb - 1)
    return ret + jnp.where(n < max_exact, n, large)


def _bias_tile_kernel(far_ref, table_ref, bucket_ref, mask_ref, o_ref):
    h = pl.program_id(0)
    bucket = bucket_ref[...]
    acc = jnp.zeros(bucket.shape, F32)
    for b in range(REL_BUCKETS):
        acc = jnp.where(bucket == b, table_ref[b, h], acc)
    o_ref[0] = (acc - table_ref[far_ref[0], h]) * LOG2E + mask_ref[...]


def _bias_tiles(table, bucket, maskadd, far_bucket, n_heads):
    r, c = bucket.shape
    return pl.pallas_call(
        _bias_tile_kernel,
        grid=(n_heads,),
        in_specs=[pl.BlockSpec(memory_space=pltpu.SMEM),
                  pl.BlockSpec(memory_space=pltpu.SMEM),
                  pl.BlockSpec((r, c), lambda h: (0, 0)),
                  pl.BlockSpec((r, c), lambda h: (0, 0))],
        out_specs=pl.BlockSpec((1, r, c), lambda h: (h, 0, 0)),
        out_shape=jax.ShapeDtypeStruct((n_heads, r, c), F32),
        compiler_params=_cparams("arbitrary"),
        name="rel_bias_tiles",
    )(far_bucket.reshape(1).astype(jnp.int32), table.astype(F32), bucket.astype(jnp.int32), maskadd)


def _diff_lambda(lp):
    a = jnp.sum(lp[0:1, :] * lp[1:2, :], axis=-1, keepdims=True)
    b = jnp.sum(lp[2:3, :] * lp[3:4, :], axis=-1, keepdims=True)
    return jnp.exp(a) - jnp.exp(b) + LAM_INIT_L0


def _stack_maps(q):
    lane = lax.broadcasted_iota(jnp.int32, q.shape, 1)
    zero = jnp.zeros_like(q)
    return jnp.concatenate([jnp.where(lane < DIFF_QK, q, zero),
                            jnp.where(lane >= DIFF_QK, q, zero)], axis=0)


def _diff_epilogue(acc, l, lam, g, tq):
    o = acc * (1.0 / l)
    o = o[:tq] - lam * o[tq:]
    ms = jnp.mean(o * o, axis=-1, keepdims=True)
    return o * lax.rsqrt(ms + EPS) * g * (1.0 - LAM_INIT_L0)


def _attn_prompt_kernel(lam_ref, g_ref, q_ref, qn_ref, k_ref, v_ref, bias_ref, o_ref, qs_ref, s_ref, m_ref,
                        acc_ref, *, tile):
    i = pl.program_id(1)
    n_lane_tiles = tile // V7X_LANES
    ones = jnp.ones((tile, V7X_LANES), BF16)
    nt = (((1,), (1,)), ((), ()))

    qs_ref[...] = _stack_maps(q_ref[...])
    m_ref[...] = jnp.full(m_ref.shape, MASK_VALUE, F32)
    acc_ref[...] = jnp.zeros(acc_ref.shape, F32)

    def key_rows(d):
        return pl.ds(pl.multiple_of((i - d) * tile, tile), tile)

    def scores(d, slot):
        s_ref[slot] = lax.dot_general(qs_ref[...], k_ref[key_rows(d), :], nt, preferred_element_type=F32)

    def absorb(d, slot, bias):
        vt = jnp.concatenate([v_ref[key_rows(d), :], ones], axis=1)
        s = s_ref[slot]
        if bias is not None:
            s = (s.reshape(2, tile, tile) + bias[None]).reshape(2 * tile, tile)
        cols = [s[:, c * V7X_LANES:(c + 1) * V7X_LANES] for c in range(n_lane_tiles)]
        mx = cols[0]
        for sc in cols[1:]:
            mx = jnp.maximum(mx, sc)
        m_prev = m_ref[...]
        m_new = jnp.maximum(m_prev, jnp.max(mx, axis=-1, keepdims=True))
        alpha = jnp.exp2(m_prev - m_new)
        p = jnp.concatenate([jnp.exp2(sc - m_new).astype(BF16) for sc in cols], axis=1)
        pv = jnp.dot(p, vt, preferred_element_type=F32)
        acc_ref[...] = jnp.concatenate([alpha, alpha], axis=1) * acc_ref[...] + pv
        m_ref[...] = m_new

    i_even = lax.rem(i, 2) == 0

    @pl.when(i == 0)
    def _():
        scores(0, DIAG_SLOT)

    @pl.when(jnp.logical_and(i_even, i >= 2))
    def _():
        scores(i - 1, 1)
        absorb(i, 0, None)

    d_odd = i - 1 + lax.rem(i, 2)

    def pair_body(u, carry):
        d = d_odd - 2 * u
        scores(d - 1, 0)
        absorb(d, 1, None)
        scores(d - 2, 1)
        absorb(d - 1, 0, None)
        return carry

    lax.fori_loop(0, jnp.where(d_odd >= 3, (d_odd - 1) // 2, 0), pair_body, 0)

    @pl.when(i >= 1)
    def _():
        scores(0, DIAG_SLOT)
        absorb(1, 1, bias_ref[0, :, :tile])

    def diagonal(next_slot):
        if next_slot is not None:
            s_ref[next_slot] = lax.dot_general(_stack_maps(qn_ref[...]), k_ref[pl.ds(0, tile), :], nt,
                                               preferred_element_type=F32)
        absorb(0, DIAG_SLOT, bias_ref[0, :, tile:])

    has_next = i + 1 < pl.num_programs(1)

    @pl.when(jnp.logical_and(has_next, jnp.logical_not(i_even)))
    def _():
        diagonal(0)

    @pl.when(jnp.logical_and(has_next, i_even))
    def _():
        diagonal(1)

    @pl.when(jnp.logical_not(has_next))
    def _():
        diagonal(None)

    acc = acc_ref[...]
    y = _diff_epilogue(acc[:, :HEAD_DIM], acc[:, HEAD_DIM:], _diff_lambda(lam_ref[...]), g_ref[...], tile)
    o_ref[...] = y.astype(o_ref.dtype)


def _attn_prompt(lam_p, g, q, k, v, bias, n_heads, tile=ATTN_TILE):
    t = q.shape[0]
    nq = t // tile
    kern = functools.partial(_attn_prompt_kernel, tile=tile)
    return pl.pallas_call(
        kern,
        grid=(n_heads, nq),
        in_specs=[pl.BlockSpec((4, DIFF_QK), lambda h, i: (0, 0)),
                  pl.BlockSpec((1, HEAD_DIM), lambda h, i: (0, 0)),
                  pl.BlockSpec((tile, HEAD_DIM), lambda h, i: (i, h)),
                  pl.BlockSpec((tile, HEAD_DIM), lambda h, i: (jnp.minimum(i + 1, nq - 1), h)),
                  pl.BlockSpec((t, HEAD_DIM), lambda h, i: (0, h)),
                  pl.BlockSpec((t, HEAD_DIM), lambda h, i: (0, h)),
                  pl.BlockSpec((1, tile, 2 * tile), lambda h, i: (h, 0, 0))],
        out_specs=pl.BlockSpec((tile, HEAD_DIM), lambda h, i: (i, h)),
        out_shape=jax.ShapeDtypeStruct((t, n_heads * HEAD_DIM), BF16),
        scratch_shapes=[pltpu.VMEM((2 * tile, HEAD_DIM), BF16),
                        pltpu.VMEM((DIAG_SLOT + 1, 2 * tile, tile), F32),
                        pltpu.VMEM((2 * tile, V7X_LANES), F32),
                        pltpu.VMEM((2 * tile, 2 * HEAD_DIM), F32)],
        compiler_params=_cparams("parallel", "arbitrary"),
        name="diff_attn_prompt",
    )(lam_p, g, q, q, k, v, bias)


def _attn_sample_step(part, n_parts, lam_ref, g_ref, q_ref, kn_ref, vn_ref, kc_ref, vc_ref, bias_ref, o_ref,
                      m_ref, acc_ref, *, n_heads, keys, ts):
    is_last = part == n_parts - 1
    n_lane_tiles = keys // V7X_LANES
    ones_c = jnp.ones((keys, V7X_LANES), BF16)
    nt = (((1,), (1,)), ((), ()))

    @pl.when(part == 0)
    def _():
        m_ref[...] = jnp.full(m_ref.shape, MASK_VALUE, F32)
        acc_ref[...] = jnp.zeros(acc_ref.shape, F32)

    def absorb(h, cols, values):
        mx = cols[0]
        for sc in cols[1:]:
            mx = jnp.maximum(mx, sc)
        m_prev = m_ref[h]
        m_new = jnp.maximum(m_prev, jnp.max(mx, axis=-1, keepdims=True))
        alpha = jnp.exp2(m_prev - m_new)
        p = jnp.concatenate([jnp.exp2(sc - m_new[:, :sc.shape[1]]).astype(BF16) for sc in cols], axis=1)
        pv = jnp.dot(p, values, preferred_element_type=F32)
        acc_ref[h] = jnp.concatenate([alpha, alpha], axis=1) * acc_ref[h] + pv
        m_ref[h] = m_new

    for h in range(n_heads):
        col = slice(h * HEAD_DIM, (h + 1) * HEAD_DIM)
        rows = pl.ds(h, keys, stride=n_heads)
        qs = _stack_maps(q_ref[0, :, col])
        kc = kc_ref[0, rows, :].astype(BF16)
        vc = jnp.concatenate([vc_ref[0, rows, :].astype(BF16), ones_c], axis=1)
        near_bias = jnp.where(is_last, bias_ref[h, :, :SAMPLE_NEAR], 0.0)
        s = lax.dot_general(qs, kc, nt, preferred_element_type=F32)
        cols = [s[:, c * V7X_LANES:(c + 1) * V7X_LANES] for c in range(n_lane_tiles)]
        cols[-1] = (cols[-1].reshape(2, ts, SAMPLE_NEAR) + near_bias[None]).reshape(2 * ts, SAMPLE_NEAR)
        absorb(h, cols, vc)

    @pl.when(is_last)
    def _():
        lam = _diff_lambda(lam_ref[...])
        g = g_ref[...]
        ones_n = jnp.ones((ts, V7X_LANES), BF16)
        for h in range(n_heads):
            col = slice(h * HEAD_DIM, (h + 1) * HEAD_DIM)
            qs = _stack_maps(q_ref[0, :, col])
            s_new = lax.dot_general(qs, kn_ref[0, :, col], nt, preferred_element_type=F32)
            s_new = (s_new.reshape(2, ts, ts) + bias_ref[h, :, SAMPLE_NEAR:][None]).reshape(2 * ts, ts)
            absorb(h, [s_new], jnp.concatenate([vn_ref[0, :, col], ones_n], axis=1))
            acc = acc_ref[h]
            y = _diff_epilogue(acc[:, :HEAD_DIM], acc[:, HEAD_DIM:], lam, g, ts)
            o_ref[0, :, h * HEAD_DIM:(h + 1) * HEAD_DIM] = y.astype(o_ref.dtype)


def _proj_and_attn_sample_kernel(x_ref, w_ref, lam_ref, g_ref, q_ref, kn_ref, vn_ref, kc_ref, vc_ref, bias_ref,
                                 y_ref, o_ref, m_ref, acc_ref, *, parts, n_heads, keys, ts):
    y_ref[...] = jnp.dot(x_ref[...], w_ref[...], preferred_element_type=F32)
    part = lax.rem(pl.program_id(0), parts)
    _attn_sample_step(part, parts, lam_ref, g_ref, q_ref, kn_ref, vn_ref, kc_ref, vc_ref, bias_ref, o_ref,
                      m_ref, acc_ref, n_heads=n_heads, keys=keys, ts=ts)


def _proj_and_attn_sample(x, w16, lam_p, g, q, kn, vn, kc, vc, bias, n_heads):
    m, kdim = x.shape
    n = w16.shape[1]
    bm, bn = FUSED_PROJ_BLOCK
    nm, nn = m // bm, n // bn
    nb, ts, d = q.shape
    past = kc.shape[1] // n_heads
    parts = SAMPLE_CACHE_PARTS
    keys = past // parts
    assert m % bm == 0 and n % bn == 0 and nm * nn == nb * parts
    assert past % parts == 0 and keys % V7X_LANES == 0 and SAMPLE_NEAR == V7X_LANES
    kern = functools.partial(_proj_and_attn_sample_kernel, parts=parts, n_heads=n_heads, keys=keys, ts=ts)
    new = pl.BlockSpec((1, ts, d), lambda s: (s // parts, 0, 0))
    cache = pl.BlockSpec((1, keys * n_heads, HEAD_DIM), lambda s: (s // parts, s % parts, 0))
    return pl.pallas_call(
        kern,
        grid=(nb * parts,),
        in_specs=[pl.BlockSpec((bm, kdim), lambda s: (s // nn, 0)),
                  pl.BlockSpec((kdim, bn), lambda s: (0, s % nn)),
                  pl.BlockSpec((4, DIFF_QK), lambda s: (0, 0)),
                  pl.BlockSpec((1, HEAD_DIM), lambda s: (0, 0)),
                  new, new, new, cache, cache,
                  pl.BlockSpec((n_heads, ts, SAMPLE_NEAR + ts), lambda s: (0, 0, 0))],
        out_specs=[pl.BlockSpec((bm, bn), lambda s: (s // nn, s % nn)), new],
        out_shape=[jax.ShapeDtypeStruct((m, n), F32),
                   jax.ShapeDtypeStruct((nb, ts, d), BF16)],
        scratch_shapes=[pltpu.VMEM((n_heads, 2 * ts, V7X_LANES), F32),
                        pltpu.VMEM((n_heads, 2 * ts, 2 * HEAD_DIM), F32)],
        compiler_params=_cparams("arbitrary"),
        name="proj_hgrn_prompt_and_diff_attn_sample",
    )(x, w16, lam_p, g, q, kn, vn, kc, vc, bias)


def _split3(x):
    hi = x.astype(BF16)
    r1 = x - hi.astype(F32)
    mid = r1.astype(BF16)
    lo = (r1 - mid.astype(F32)).astype(BF16)
    return hi, mid, lo


def _silu(x):
    h = 0.5 * x
    return h + h * jnp.tanh(h)


def _hgrn_kernel(lbl_ref, g_ref, hq_ref, hf_ref, hi_ref, hg_ref, s0_ref, o_ref, s_out_ref, st_ref, *,
                 n_heads, chain):
    c = pl.program_id(1)
    nc = pl.num_programs(1)
    n_streams = st_ref.shape[0]
    n_chunks = HGRN_CHUNKS_PER_STEP
    rows = n_chunks * CHUNK
    d = n_heads * HEAD_DIM

    @pl.when(c == 0)
    def _():
        for s in range(n_streams):
            for h in range(n_heads):
                st_ref[s, h] = s0_ref[s, h].T

    lg = lbl_ref[...]
    e = jnp.exp(lg - jnp.max(lg, axis=0, keepdims=True))
    lb = e[0:1, :] / jnp.sum(e, axis=0, keepdims=True)

    c_half = 0.5 * (1.0 - lb)
    t = jnp.tanh(0.5 * hf_ref[...].reshape(rows, d))
    ct = c_half * t
    logf = jnp.log2((lb + c_half) + ct)
    kk = c_half - ct
    qq = _silu(hq_ref[...].reshape(rows, d))

    row = lax.broadcasted_iota(jnp.int32, (rows, rows), 0)
    col = lax.broadcasted_iota(jnp.int32, (rows, rows), 1)
    causal = jnp.logical_and(col <= row, col // CHUNK == row // CHUNK)
    tri = jnp.where(causal, 1.0, 0.0).astype(BF16)
    b = None
    for part in _split3(logf):
        term = jnp.dot(tri, part, preferred_element_type=F32)
        b = term if b is None else b + term
    qd = (qq * jnp.exp2(b)).astype(BF16)
    kd32 = kk * jnp.exp2(-b)
    kd = kd32.astype(BF16)
    decays = [jnp.exp2(b[(r + 1) * CHUNK - 1:(r + 1) * CHUNK, :]) for r in range(n_chunks)]
    k_last = jnp.concatenate([kd32[r * CHUNK:(r + 1) * CHUNK] * decays[r] for r in range(n_chunks)],
                             axis=0).astype(BF16)
    v = hi_ref[...].reshape(rows, d).astype(BF16)
    gate = _silu(hg_ref[...].reshape(rows, d))
    g = g_ref[...]

    nt = (((1,), (1,)), ((), ()))
    tn = (((0,), (0,)), ((), ()))
    for h in range(n_heads):
        cols = slice(h * HEAD_DIM, (h + 1) * HEAD_DIM)
        a = lax.dot_general(qd[:, cols], kd[:, cols], nt, preferred_element_type=F32)
        a = jnp.where(causal, a, 0.0)
        o_intra = jnp.dot(a.astype(BF16), v[:, cols], preferred_element_type=F32)
        o_parts = []
        st = st_ref[0, h]
        for r in range(n_chunks):
            rr = slice(r * CHUNK, (r + 1) * CHUNK)
            if not chain:
                st = st_ref[r, h]
            o_parts.append(o_intra[rr] + lax.dot_general(qd[rr, cols], st.astype(BF16), nt,
                                                         preferred_element_type=F32))
            st = decays[r][:, cols] * st + lax.dot_general(v[rr, cols], k_last[rr, cols], tn,
                                                           preferred_element_type=F32)
            if not chain:
                st_ref[r, h] = st
        if chain:
            st_ref[0, h] = st
        o = jnp.concatenate(o_parts, axis=0)
        ms = jnp.mean(o * o, axis=-1, keepdims=True)
        y = (o * lax.rsqrt(ms + EPS) * g * gate[:, cols]).astype(o_ref.dtype)
        if chain:
            o_ref[0, :, cols] = y
        else:
            for r in range(n_chunks):
                o_ref[r, :, cols] = y[r * CHUNK:(r + 1) * CHUNK]

    @pl.when(c == nc - 1)
    def _():
        for s in range(n_streams):
            for h in range(n_heads):
                s_out_ref[s, h] = st_ref[s, h].T


def _hgrn(lb_logits, g, hpart, s0, n_heads):
    nb, t, _ = hpart.shape
    d = n_heads * HEAD_DIM
    per = HGRN_CHUNKS_PER_STEP
    chain = t > CHUNK
    sb, tb = (1, per * CHUNK) if chain else (per, CHUNK)
    assert nb % sb == 0 and t % tb == 0
    kern = functools.partial(_hgrn_kernel, n_heads=n_heads, chain=chain)
    part = lambda p: pl.BlockSpec((sb, tb, d), lambda b, c: (b, c, p))
    state = pl.BlockSpec((sb, n_heads, HEAD_DIM, HEAD_DIM), lambda b, c: (b, 0, 0, 0))
    return pl.pallas_call(
        kern,
        grid=(nb // sb, t // tb),
        in_specs=[pl.BlockSpec(lb_logits.shape, lambda b, c: (0, 0)),
                  pl.BlockSpec((1, HEAD_DIM), lambda b, c: (0, 0)),
                  part(0), part(1), part(2), part(3), state],
        out_specs=[pl.BlockSpec((sb, tb, d), lambda b, c: (b, c, 0)), state],
        out_shape=[jax.ShapeDtypeStruct((nb, t, d), BF16),
                   jax.ShapeDtypeStruct(s0.shape, F32)],
        scratch_shapes=[pltpu.VMEM((sb, n_heads, HEAD_DIM, HEAD_DIM), F32)],
        compiler_params=_cparams("parallel", "arbitrary"),
        name="hgrn2_chunks",
    )(lb_logits, g, hpart, hpart, hpart, hpart, s0)


def _cross_kernel(q_ref, k_ref, v_ref, o_ref, *, heads, hd, split_heads):
    nt = (((1,), (1,)), ((), ()))
    if split_heads:
        tq = q_ref.shape[1]
        n_mem = k_ref.shape[1]
        xk = k_ref[0].reshape(n_mem * heads, hd).astype(BF16)
        xv = v_ref[0].reshape(n_mem * heads, hd).astype(BF16)
        q = jnp.concatenate([q_ref[0, :, h * hd:(h + 1) * hd] for h in range(heads)], axis=0)
        s = lax.dot_general(q, xk, nt, preferred_element_type=F32)
        row_head = lax.broadcasted_iota(jnp.int32, s.shape, 0) // tq
        col_head = lax.rem(lax.broadcasted_iota(jnp.int32, s.shape, 1), heads)
        s = jnp.where(row_head == col_head, s, MASK_VALUE)
        p = jnp.exp(s - jnp.max(s, axis=-1, keepdims=True))
        l = jnp.sum(p, axis=-1, keepdims=True)
        o = jnp.dot(p.astype(BF16), xv, preferred_element_type=F32) * (1.0 / l)
        for h in range(heads):
            o_ref[0, :, h * hd:(h + 1) * hd] = o[h * tq:(h + 1) * tq].astype(o_ref.dtype)
        return
    for h in range(heads):
        cols = slice(h * hd, (h + 1) * hd)
        q = q_ref[0, :, cols]
        k = k_ref[0, :, cols].astype(BF16)
        v = v_ref[0, :, cols].astype(BF16)
        s = lax.dot_general(q, k, nt, preferred_element_type=F32)
        p = jnp.exp(s - jnp.max(s, axis=-1, keepdims=True))
        l = jnp.sum(p, axis=-1, keepdims=True)
        o = jnp.dot(p.astype(BF16), v, preferred_element_type=F32) * (1.0 / l)
        o_ref[0, :, cols] = o.astype(o_ref.dtype)


def _cross_attn(q, mk, mv, tq):
    nb, t, d = q.shape
    split_heads = mk.ndim == 4
    kern = functools.partial(_cross_kernel, heads=MEM_HEADS, hd=d // MEM_HEADS, split_heads=split_heads)
    if split_heads:
        mem_spec = pl.BlockSpec((1,) + mk.shape[1:], lambda b, i: (b, 0, 0, 0))
    else:
        mem_spec = pl.BlockSpec((1, N_MEM, d), lambda b, i: (b, 0, 0))
    return pl.pallas_call(
        kern,
        grid=(nb, t // tq),
        in_specs=[pl.BlockSpec((1, tq, d), lambda b, i: (b, i, 0)),
                  mem_spec,
                  mem_spec],
        out_specs=pl.BlockSpec((1, tq, d), lambda b, i: (b, i, 0)),
        out_shape=jax.ShapeDtypeStruct((nb, t, d), BF16),
        compiler_params=_cparams("parallel", "parallel"),
        name="mem_cross_attn",
    )(q, mk, mv)


def _named_matmul(w, blocks, emit_w16, w16):
    bm, bn = blocks

    def mm(names, xs, n, out_dtypes, **kw):
        pairs = [(xi,) + w[nm] for nm, xi in zip(names, xs)]
        outs = _matmul(pairs, n, out_dtypes, bm=bm, bn=bn, emit_w16=emit_w16, **kw)
        if emit_w16:
            for nm, arr in zip(names, outs[len(out_dtypes):]):
                w16[nm] = (arr, 0, 0)
        return outs[:len(out_dtypes)]

    return mm


def _in_proj(x2, norms, w, widths, *, blocks, emit_w16, with_hgrn):
    w16 = {}
    mm = _named_matmul(w, blocks, emit_w16, w16)
    n = _rmsnorm(x2, norms["mix"], BF16)
    (dq,) = mm(["dq"], [n], widths["diff"], [BF16], scale=DIFF_QK ** -0.5 * LOG2E, name="proj_dq")
    dk32, dk16 = mm(["dk"], [n], widths["diff"], [F32, BF16], name="proj_dk")
    dv32, dv16 = mm(["dv"], [n], widths["diff"], [F32, BF16], name="proj_dv")
    hpart = mm(["h"], [n], widths["hgrn_in"], [F32], name="proj_hgrn")[0] if with_hgrn else None
    return n, dq, dk32, dk16, dv32, dv16, hpart, w16


def _tail(x2, nb, t, cat_diff, cat_hgrn, norms, w, mem_kv_fn, *, blocks, emit_w16):
    m, d = x2.shape
    w16 = {}
    mm = _named_matmul(w, blocks, emit_w16, w16)

    (h1,) = mm(["out_diff", "out_hgrn"], [cat_diff, cat_hgrn], d, [F32], residual=x2, name="proj_out")

    n2 = _rmsnorm(h1, norms["cross"], BF16)
    (cq,) = mm(["mem_q"], [n2], d, [BF16], scale=(d // MEM_HEADS) ** -0.5, name="proj_mem_q")
    mk, mv = mem_kv_fn()
    co = _cross_attn(cq.reshape(nb, t, d), mk, mv, tq=min(t, 512))
    (h2,) = mm(["mem_o"], [co.reshape(m, d)], d, [F32], residual=h1, name="proj_mem_o")

    n3 = _rmsnorm(h2, norms["ffn"], BF16)
    gu = _gateup(n3, w["gate"][0], w["up"][0], bm=FFN_GATE_UP_BLOCK[0], bn=FFN_GATE_UP_BLOCK[1],
                 emit_w16=emit_w16)
    act = gu[0]
    if emit_w16:
        w16["gate"], w16["up"] = (gu[1], 0, 0), (gu[2], 0, 0)
    (h3,) = _matmul([(act,) + w["down"]], d, [F32], residual=h2, bm=FFN_DOWN_BLOCK[0], bn=FFN_DOWN_BLOCK[1],
                    name="ffn_down")
    y = _rmsnorm(h3, norms["final"], F32)
    return y.reshape(nb, t, d), w16


def kernel(x_prompt, x_sample, mem_prompt, cache_diff_k, cache_diff_v, state_hgrn, cache_mem_k, cache_mem_v, norm_mix, w_in, diff_lambda, diff_subln, rel_bias, hgrn_lb_logits, hgrn_norm, w_out, norm_cross, norm_mem, w_mem_q, w_mem_k, w_mem_v, w_mem_o, norm_ffn, w_ffn_gate, w_ffn_up, w_ffn_down, norm_final):
    bp, tp, d = x_prompt.shape
    bs, ts, _ = x_sample.shape
    depth = w_in.shape[0]
    assert depth == 1 and bp == 1
    past = cache_diff_k.shape[2]
    n_dh = cache_diff_k.shape[3]
    d_diff = n_dh * HEAD_DIM
    n_hh = state_hgrn.shape[2]
    d_hgrn = n_hh * HEAD_DIM
    assert past % CHUNK == 0 and ts == CHUNK and tp % ATTN_TILE == 0

    norms = {"mix": norm_mix[0], "cross": norm_cross[0], "ffn": norm_ffn[0], "final": norm_final}
    widths = {"diff": d_diff, "hgrn_in": w_in.shape[2] - 3 * d_diff}
    w_f32 = {
        "dq": (w_in[0], 0, 0), "dk": (w_in[0], 0, d_diff), "dv": (w_in[0], 0, 2 * d_diff),
        "h": (w_in[0], 0, 3 * d_diff),
        "out_diff": (w_out[0], 0, 0), "out_hgrn": (w_out[0], d_diff, 0),
        "mem_q": (w_mem_q[0], 0, 0), "mem_o": (w_mem_o[0], 0, 0),
        "gate": (w_ffn_gate[0], 0, 0), "up": (w_ffn_up[0], 0, 0),
        "down": (w_ffn_down[0].astype(BF16), 0, 0),
    }
    lam_p = diff_lambda[0].astype(F32)
    subln = diff_subln[0].reshape(1, HEAD_DIM)
    hnorm = hgrn_norm[0].reshape(1, HEAD_DIM)

    tile = ATTN_TILE
    r = jnp.arange(tile)[:, None]
    c = jnp.arange(2 * tile)[None, :]
    rel_p = (c - tile) - r
    vis_p = jnp.logical_or(c < tile, (c - tile) // CHUNK <= r // CHUNK)
    bias_p = _bias_tiles(rel_bias, _rel_bucket(rel_p), jnp.where(vis_p, 0.0, MASK_VALUE).astype(F32),
                         _rel_bucket(jnp.int32(-2 * tile)), n_dh)
    qpos = past + jnp.arange(ts)[:, None]
    kpos = (past - SAMPLE_NEAR) + jnp.arange(SAMPLE_NEAR + ts)[None, :]
    vis_s = kpos // CHUNK <= qpos // CHUNK
    bias_s = _bias_tiles(rel_bias, _rel_bucket(kpos - qpos), jnp.where(vis_s, 0.0, MASK_VALUE).astype(F32),
                         _rel_bucket(jnp.int32(-(SAMPLE_NEAR + 1))), n_dh)

    mem_out = {}

    def mem_kv_prompt():
        mn = _rmsnorm(mem_prompt.reshape(bp * N_MEM, d), norm_mem[0], BF16)
        bm, bn = MM_BLOCK_F32_WEIGHTS
        (mk,) = _matmul([(mn, w_mem_k[0], 0, 0)], d, [F32], bm=bm, bn=bn, name="proj_mem_k")
        (mv,) = _matmul([(mn, w_mem_v[0], 0, 0)], d, [F32], bm=bm, bn=bn, name="proj_mem_v")
        mem_out["k"], mem_out["v"] = mk, mv
        return mk.reshape(bp, N_MEM, d), mv.reshape(bp, N_MEM, d)

    def mem_kv_sample():
        return cache_mem_k[0], cache_mem_v[0]

    xs2 = x_sample.reshape(bs * ts, d)
    xp2 = x_prompt.reshape(bp * tp, d)
    _, dq_s, dk_s, dk16_s, dv_s, dv16_s, hpart_s, w16_in = _in_proj(
        xs2, norms, w_f32, widths, blocks=MM_BLOCK_F32_WEIGHTS, emit_w16=True, with_hgrn=True)
    n_p, dq_p, dk_p, dk16_p, dv_p, dv16_p, _, _ = _in_proj(
        xp2, norms, w16_in, widths, blocks=MM_BLOCK_BF16_WEIGHTS, emit_w16=False, with_hgrn=False)

    shp = (bs, ts, d_diff)
    hpart_p, cat_diff_s = _proj_and_attn_sample(
        n_p, w16_in["h"][0], lam_p, subln, dq_s.reshape(shp), dk16_s.reshape(shp), dv16_s.reshape(shp),
        cache_diff_k[0].reshape(bs, past * n_dh, HEAD_DIM), cache_diff_v[0].reshape(bs, past * n_dh, HEAD_DIM),
        bias_s, n_dh)

    cat_hgrn_s, s_s = _hgrn(hgrn_lb_logits, hnorm, hpart_s.reshape(bs, ts, 4 * d_hgrn), state_hgrn[0], n_hh)
    y_s, w16_tail = _tail(xs2, bs, ts, cat_diff_s.reshape(bs * ts, d_diff), cat_hgrn_s.reshape(bs * ts, d_hgrn),
                          norms, w_f32, mem_kv_sample, blocks=MM_BLOCK_F32_WEIGHTS, emit_w16=True)

    w_bf16 = {**w16_in, **w16_tail, "down": w_f32["down"]}
    cat_diff_p = _attn_prompt(lam_p, subln, dq_p, dk16_p, dv16_p, bias_p, n_dh)
    s0 = jnp.zeros((bp, n_hh, HEAD_DIM, HEAD_DIM), F32)
    cat_hgrn_p, s_p = _hgrn(hgrn_lb_logits, hnorm, hpart_p.reshape(bp, tp, 4 * d_hgrn), s0, n_hh)
    y_p, _ = _tail(xp2, bp, tp, cat_diff_p, cat_hgrn_p.reshape(bp * tp, d_hgrn), norms, w_bf16, mem_kv_prompt,
                   blocks=MM_BLOCK_BF16_WEIGHTS, emit_w16=False)

    mhd = d // MEM_HEADS
    return (y_p, y_s,
            dk_p.reshape(depth, bp, tp, n_dh, HEAD_DIM), dv_p.reshape(depth, bp, tp, n_dh, HEAD_DIM),
            s_p.reshape(depth, bp, n_hh, HEAD_DIM, HEAD_DIM),
            mem_out["k"].reshape(depth, bp, N_MEM, MEM_HEADS, mhd),
            mem_out["v"].reshape(depth, bp, N_MEM, MEM_HEADS, mhd),
            dk_s.reshape(depth, bs, ts, n_dh, HEAD_DIM), dv_s.reshape(depth, bs, ts, n_dh, HEAD_DIM),
            s_s.reshape(depth, bs, n_hh, HEAD_DIM, HEAD_DIM))
```

```python
import functools
import math

import jax
import jax.numpy as jnp
from jax import lax
from jax.experimental import pallas as pl
from jax.experimental.pallas import tpu as pltpu

F32 = jnp.float32
BF16 = jnp.bfloat16

CHUNK = 64
HEAD_DIM = 128
DIFF_QK = HEAD_DIM // 2
N_MEM = 256
MEM_HEADS = 4
REL_BUCKETS = 32
REL_MAX_DIST = 128
EPS = 1e-6
LAM_INIT_L0 = 0.8 - 0.6 * math.exp(-0.3 * 0)
MASK_VALUE = -1e30
LOG2E = math.log2(math.e)

V7X_LANES = 128
V7X_SCOPED_VMEM_LIMIT_BYTES = 60000 * 1024
MM_BLOCK_F32_WEIGHTS = (2048, 256)
MM_BLOCK_BF16_WEIGHTS = (1024, 1024)
FFN_GATE_UP_BLOCK = (2048, 256)
FFN_DOWN_BLOCK = (512, 512)

DIAG_SLOT = 2
ATTN_TILE = 512
SAMPLE_CACHE_PARTS = 4
FUSED_PROJ_BLOCK = (1024, 512)
HGRN_CHUNKS_PER_STEP = 4
SAMPLE_NEAR = 128


def _cparams(*sem):
    return pltpu.CompilerParams(dimension_semantics=sem,
                                vmem_limit_bytes=V7X_SCOPED_VMEM_LIMIT_BYTES)


def _rmsnorm_kernel(x_ref, g_ref, o_ref):
    x = x_ref[...]
    ms = jnp.mean(x * x, axis=-1, keepdims=True)
    o_ref[...] = (x * lax.rsqrt(ms + EPS) * g_ref[...]).astype(o_ref.dtype)


def _rmsnorm(x, g, out_dtype, rows=512):
    m, d = x.shape
    rows = min(rows, m)
    return pl.pallas_call(
        _rmsnorm_kernel,
        grid=(m // rows,),
        in_specs=[pl.BlockSpec((rows, d), lambda i: (i, 0)),
                  pl.BlockSpec((1, d), lambda i: (0, 0))],
        out_specs=pl.BlockSpec((rows, d), lambda i: (i, 0)),
        out_shape=jax.ShapeDtypeStruct((m, d), out_dtype),
        compiler_params=_cparams("parallel"),
        name="rmsnorm",
    )(x, g.reshape(1, d))


def _mm_kernel(*refs, n_pairs, has_res, n_out, scale):
    xs = refs[0:2 * n_pairs:2]
    ws = refs[1:2 * n_pairs:2]
    pos = 2 * n_pairs
    res_ref = refs[pos] if has_res else None
    pos += int(has_res)
    outs = refs[pos:pos + n_out]
    w16_outs = refs[pos + n_out:]

    total = None
    for p, (x_ref, w_ref) in enumerate(zip(xs, ws)):
        wb = w_ref[...].astype(BF16)
        if w16_outs:
            w16_outs[p][...] = wb
        d = jnp.dot(x_ref[...], wb, preferred_element_type=F32)
        total = d if total is None else total + d
    if scale != 1.0:
        total = total * scale
    if has_res:
        total = total + res_ref[...]
    for o_ref in outs:
        o_ref[...] = total.astype(o_ref.dtype)


def _matmul(pairs, n, out_dtypes, *, residual=None, scale=1.0, bm, bn, emit_w16=False, name="matmul"):
    m = pairs[0][0].shape[0]
    bm = min(bm, m)
    bn = min(bn, n)
    assert m % bm == 0 and n % bn == 0
    in_specs, args = [], []
    for x, w, r0, c0 in pairs:
        kx = x.shape[1]
        assert r0 % kx == 0 and c0 % bn == 0
        in_specs.append(pl.BlockSpec((bm, kx), lambda j, i: (i, 0)))
        in_specs.append(pl.BlockSpec((kx, bn), lambda j, i, rb=r0 // kx, cb=c0 // bn: (rb, cb + j)))
        args += [x, w]
    if residual is not None:
        in_specs.append(pl.BlockSpec((bm, bn), lambda j, i: (i, j)))
        args.append(residual)
    out_specs = [pl.BlockSpec((bm, bn), lambda j, i: (i, j)) for _ in out_dtypes]
    out_shape = [jax.ShapeDtypeStruct((m, n), dt) for dt in out_dtypes]
    if emit_w16:
        for x, _, _, _ in pairs:
            out_specs.append(pl.BlockSpec((x.shape[1], bn), lambda j, i: (0, j)))
            out_shape.append(jax.ShapeDtypeStruct((x.shape[1], n), BF16))
    kern = functools.partial(_mm_kernel, n_pairs=len(pairs), has_res=residual is not None,
                             n_out=len(out_dtypes), scale=scale)
    return pl.pallas_call(
        kern,
        grid=(n // bn, m // bm),
        in_specs=in_specs,
        out_specs=out_specs,
        out_shape=out_shape,
        compiler_params=_cparams("parallel", "arbitrary"),
        name=name,
    )(*args)


def _gateup_kernel(x_ref, wg_ref, wu_ref, o_ref, *w16_outs):
    x = x_ref[...]
    wg = wg_ref[...].astype(BF16)
    wu = wu_ref[...].astype(BF16)
    if w16_outs:
        w16_outs[0][...] = wg
        w16_outs[1][...] = wu
    g = jnp.dot(x, wg, preferred_element_type=F32)
    u = jnp.dot(x, wu, preferred_element_type=F32)
    o_ref[...] = (_silu(g) * u).astype(o_ref.dtype)


def _gateup(x, wg, wu, *, bm, bn, emit_w16=False):
    m, kdim = x.shape
    n = wg.shape[1]
    bm = min(bm, m)
    assert m % bm == 0 and n % bn == 0
    w_spec = pl.BlockSpec((kdim, bn), lambda j, i: (0, j))
    out_specs = [pl.BlockSpec((bm, bn), lambda j, i: (i, j))]
    out_shape = [jax.ShapeDtypeStruct((m, n), BF16)]
    if emit_w16:
        out_specs += [w_spec, w_spec]
        out_shape += [jax.ShapeDtypeStruct((kdim, n), BF16)] * 2
    return pl.pallas_call(
        _gateup_kernel,
        grid=(n // bn, m // bm),
        in_specs=[pl.BlockSpec((bm, kdim), lambda j, i: (i, 0)), w_spec, w_spec],
        out_specs=out_specs,
        out_shape=out_shape,
        compiler_params=_cparams("parallel", "arbitrary"),
        name="ffn_gate_up",
    )(x, wg, wu)


def _rel_bucket(rel):
    nb = REL_BUCKETS // 2
    max_exact = nb // 2
    ret = jnp.where(rel > 0, nb, 0)
    n = jnp.abs(rel)
    nf = jnp.maximum(n, 1).astype(F32)
    large = max_exact + (jnp.log(nf / max_exact) / math.log(REL_MAX_DIST / max_exact)
                         * (nb - max_exact)).astype(jnp.int32)
    large = jnp.minimum(large, nb - 1)
    return ret + jnp.where(n < max_exact, n, large)


def _bias_tile_kernel(far_ref, table_ref, bucket_ref, mask_ref, o_ref):
    h = pl.program_id(0)
    bucket = bucket_ref[...]
    acc = jnp.zeros(bucket.shape, F32)
    for b in range(REL_BUCKETS):
        acc = jnp.where(bucket == b, table_ref[b, h], acc)
    o_ref[0] = (acc - table_ref[far_ref[0], h]) * LOG2E + mask_ref[...]


def _bias_tiles(table, bucket, maskadd, far_bucket, n_heads):
    r, c = bucket.shape
    return pl.pallas_call(
        _bias_tile_kernel,
        grid=(n_heads,),
        in_specs=[pl.BlockSpec(memory_space=pltpu.SMEM),
                  pl.BlockSpec(memory_space=pltpu.SMEM),
                  pl.BlockSpec((r, c), lambda h: (0, 0)),
                  pl.BlockSpec((r, c), lambda h: (0, 0))],
        out_specs=pl.BlockSpec((1, r, c), lambda h: (h, 0, 0)),
        out_shape=jax.ShapeDtypeStruct((n_heads, r, c), F32),
        compiler_params=_cparams("arbitrary"),
        name="rel_bias_tiles",
    )(far_bucket.reshape(1).astype(jnp.int32), table.astype(F32), bucket.astype(jnp.int32), maskadd)


def _diff_lambda(lp):
    a = jnp.sum(lp[0:1, :] * lp[1:2, :], axis=-1, keepdims=True)
    b = jnp.sum(lp[2:3, :] * lp[3:4, :], axis=-1, keepdims=True)
    return jnp.exp(a) - jnp.exp(b) + LAM_INIT_L0


def _stack_maps(q):
    lane = lax.broadcasted_iota(jnp.int32, q.shape, 1)
    zero = jnp.zeros_like(q)
    return jnp.concatenate([jnp.where(lane < DIFF_QK, q, zero),
                            jnp.where(lane >= DIFF_QK, q, zero)], axis=0)


def _diff_epilogue(acc, l, lam, g, tq):
    o = acc * (1.0 / l)
    o = o[:tq] - lam * o[tq:]
    ms = jnp.mean(o * o, axis=-1, keepdims=True)
    return o * lax.rsqrt(ms + EPS) * g * (1.0 - LAM_INIT_L0)


def _attn_prompt_kernel(lam_ref, g_ref, q_ref, qn_ref, k_ref, v_ref, bias_ref, o_ref, qs_ref, s_ref, m_ref,
                        acc_ref, *, tile):
    i = pl.program_id(1)
    n_lane_tiles = tile // V7X_LANES
    ones = jnp.ones((tile, V7X_LANES), BF16)
    nt = (((1,), (1,)), ((), ()))

    qs_ref[...] = _stack_maps(q_ref[...])
    m_ref[...] = jnp.full(m_ref.shape, MASK_VALUE, F32)
    acc_ref[...] = jnp.zeros(acc_ref.shape, F32)

    def key_rows(d):
        return pl.ds(pl.multiple_of((i - d) * tile, tile), tile)

    def scores(d, slot):
        s_ref[slot] = lax.dot_general(qs_ref[...], k_ref[key_rows(d), :], nt, preferred_element_type=F32)

    def absorb(d, slot, bias):
        vt = jnp.concatenate([v_ref[key_rows(d), :], ones], axis=1)
        s = s_ref[slot]
        if bias is not None:
            s = (s.reshape(2, tile, tile) + bias[None]).reshape(2 * tile, tile)
        cols = [s[:, c * V7X_LANES:(c + 1) * V7X_LANES] for c in range(n_lane_tiles)]
        mx = cols[0]
        for sc in cols[1:]:
            mx = jnp.maximum(mx, sc)
        m_prev = m_ref[...]
        m_new = jnp.maximum(m_prev, jnp.max(mx, axis=-1, keepdims=True))
        alpha = jnp.exp2(m_prev - m_new)
        p = jnp.concatenate([jnp.exp2(sc - m_new).astype(BF16) for sc in cols], axis=1)
        pv = jnp.dot(p, vt, preferred_element_type=F32)
        acc_ref[...] = jnp.concatenate([alpha, alpha], axis=1) * acc_ref[...] + pv
        m_ref[...] = m_new

    i_even = lax.rem(i, 2) == 0

    @pl.when(i == 0)
    def _():
        scores(0, DIAG_SLOT)

    @pl.when(jnp.logical_and(i_even, i >= 2))
    def _():
        scores(i - 1, 1)
        absorb(i, 0, None)

    d_odd = i - 1 + lax.rem(i, 2)

    def pair_body(u, carry):
        d = d_odd - 2 * u
        scores(d - 1, 0)
        absorb(d, 1, None)
        scores(d - 2, 1)
        absorb(d - 1, 0, None)
        return carry

    lax.fori_loop(0, jnp.where(d_odd >= 3, (d_odd - 1) // 2, 0), pair_body, 0)

    @pl.when(i >= 1)
    def _():
        scores(0, DIAG_SLOT)
        absorb(1, 1, bias_ref[0, :, :tile])

    def diagonal(next_slot):
        if next_slot is not None:
            s_ref[next_slot] = lax.dot_general(_stack_maps(qn_ref[...]), k_ref[pl.ds(0, tile), :], nt,
                                               preferred_element_type=F32)
        absorb(0, DIAG_SLOT, bias_ref[0, :, tile:])

    has_next = i + 1 < pl.num_programs(1)

    @pl.when(jnp.logical_and(has_next, jnp.logical_not(i_even)))
    def _():
        diagonal(0)

    @pl.when(jnp.logical_and(has_next, i_even))
    def _():
        diagonal(1)

    @pl.when(jnp.logical_not(has_next))
    def _():
        diagonal(None)

    acc = acc_ref[...]
    y = _diff_epilogue(acc[:, :HEAD_DIM], acc[:, HEAD_DIM:], _diff_lambda(lam_ref[...]), g_ref[...], tile)
    o_ref[...] = y.astype(o_ref.dtype)


def _attn_prompt(lam_p, g, q, k, v, bias, n_heads, tile=ATTN_TILE):
    t = q.shape[0]
    nq = t // tile
    kern = functools.partial(_attn_prompt_kernel, tile=tile)
    return pl.pallas_call(
        kern,
        grid=(n_heads, nq),
        in_specs=[pl.BlockSpec((4, DIFF_QK), lambda h, i: (0, 0)),
                  pl.BlockSpec((1, HEAD_DIM), lambda h, i: (0, 0)),
                  pl.BlockSpec((tile, HEAD_DIM), lambda h, i: (i, h)),
                  pl.BlockSpec((tile, HEAD_DIM), lambda h, i: (jnp.minimum(i + 1, nq - 1), h)),
                  pl.BlockSpec((t, HEAD_DIM), lambda h, i: (0, h)),
                  pl.BlockSpec((t, HEAD_DIM), lambda h, i: (0, h)),
                  pl.BlockSpec((1, tile, 2 * tile), lambda h, i: (h, 0, 0))],
        out_specs=pl.BlockSpec((tile, HEAD_DIM), lambda h, i: (i, h)),
        out_shape=jax.ShapeDtypeStruct((t, n_heads * HEAD_DIM), BF16),
        scratch_shapes=[pltpu.VMEM((2 * tile, HEAD_DIM), BF16),
                        pltpu.VMEM((DIAG_SLOT + 1, 2 * tile, tile), F32),
                        pltpu.VMEM((2 * tile, V7X_LANES), F32),
                        pltpu.VMEM((2 * tile, 2 * HEAD_DIM), F32)],
        compiler_params=_cparams("parallel", "arbitrary"),
        name="diff_attn_prompt",
    )(lam_p, g, q, q, k, v, bias)


def _attn_sample_step(part, n_parts, lam_ref, g_ref, q_ref, kn_ref, vn_ref, kc_ref, vc_ref, bias_ref, o_ref,
                      m_ref, acc_ref, *, n_heads, keys, ts):
    is_last = part == n_parts - 1
    n_lane_tiles = keys // V7X_LANES
    ones_c = jnp.ones((keys, V7X_LANES), BF16)
    ones_n = jnp.ones((ts, V7X_LANES), BF16)
    nt = (((1,), (1,)), ((), ()))

    @pl.when(part == 0)
    def _():
        m_ref[...] = jnp.full(m_ref.shape, MASK_VALUE, F32)
        acc_ref[...] = jnp.zeros(acc_ref.shape, F32)

    for h in range(n_heads):
        col = slice(h * HEAD_DIM, (h + 1) * HEAD_DIM)
        rows = pl.ds(h, keys, stride=n_heads)
        qs = _stack_maps(q_ref[0, :, col])
        kc = kc_ref[0, rows, :].astype(BF16)
        vc = jnp.concatenate([vc_ref[0, rows, :].astype(BF16), ones_c], axis=1)
        vn = jnp.concatenate([vn_ref[0, :, col], ones_n], axis=1)
        bias = bias_ref[h]
        near_bias = jnp.where(is_last, bias[:, :SAMPLE_NEAR], 0.0)
        new_bias = jnp.where(is_last, bias[:, SAMPLE_NEAR:], MASK_VALUE)

        s = lax.dot_general(qs, kc, nt, preferred_element_type=F32)
        cols = [s[:, c * V7X_LANES:(c + 1) * V7X_LANES] for c in range(n_lane_tiles)]
        cols[-1] = (cols[-1].reshape(2, ts, SAMPLE_NEAR) + near_bias[None]).reshape(2 * ts, SAMPLE_NEAR)
        s_new = lax.dot_general(qs, kn_ref[0, :, col], nt, preferred_element_type=F32)
        s_new = (s_new.reshape(2, ts, ts) + new_bias[None]).reshape(2 * ts, ts)

        mx = cols[0]
        for sc in cols[1:]:
            mx = jnp.maximum(mx, sc)
        m_prev = m_ref[h]
        m_new = jnp.maximum(m_prev, jnp.maximum(jnp.max(mx, axis=-1, keepdims=True),
                                                jnp.max(s_new, axis=-1, keepdims=True)))
        alpha = jnp.exp2(m_prev - m_new)
        p = jnp.concatenate([jnp.exp2(sc - m_new).astype(BF16) for sc in cols], axis=1)
        p_new = jnp.exp2(s_new - m_new[:, :ts]).astype(BF16)
        pv = (jnp.dot(p, vc, preferred_element_type=F32)
              + jnp.dot(p_new, vn, preferred_element_type=F32))
        acc_ref[h] = jnp.concatenate([alpha, alpha], axis=1) * acc_ref[h] + pv
        m_ref[h] = m_new

    @pl.when(is_last)
    def _():
        lam = _diff_lambda(lam_ref[...])
        g = g_ref[...]
        for h in range(n_heads):
            acc = acc_ref[h]
            y = _diff_epilogue(acc[:, :HEAD_DIM], acc[:, HEAD_DIM:], lam, g, ts)
            o_ref[0, :, h * HEAD_DIM:(h + 1) * HEAD_DIM] = y.astype(o_ref.dtype)


def _proj_and_attn_sample_kernel(x_ref, w_ref, lam_ref, g_ref, q_ref, kn_ref, vn_ref, kc_ref, vc_ref, bias_ref,
                                 y_ref, o_ref, m_ref, acc_ref, *, parts, n_heads, keys, ts):
    y_ref[...] = jnp.dot(x_ref[...], w_ref[...], preferred_element_type=F32)
    part = lax.rem(pl.program_id(0), parts)
    _attn_sample_step(part, parts, lam_ref, g_ref, q_ref, kn_ref, vn_ref, kc_ref, vc_ref, bias_ref, o_ref,
                      m_ref, acc_ref, n_heads=n_heads, keys=keys, ts=ts)


def _proj_and_attn_sample(x, w16, lam_p, g, q, kn, vn, kc, vc, bias, n_heads):
    m, kdim = x.shape
    n = w16.shape[1]
    bm, bn = FUSED_PROJ_BLOCK
    nm, nn = m // bm, n // bn
    nb, ts, d = q.shape
    past = kc.shape[1] // n_heads
    parts = SAMPLE_CACHE_PARTS
    keys = past // parts
    assert m % bm == 0 and n % bn == 0 and nm * nn == nb * parts
    assert past % parts == 0 and keys % V7X_LANES == 0 and SAMPLE_NEAR == V7X_LANES
    kern = functools.partial(_proj_and_attn_sample_kernel, parts=parts, n_heads=n_heads, keys=keys, ts=ts)
    new = pl.BlockSpec((1, ts, d), lambda s: (s // parts, 0, 0))
    cache = pl.BlockSpec((1, keys * n_heads, HEAD_DIM), lambda s: (s // parts, s % parts, 0))
    return pl.pallas_call(
        kern,
        grid=(nb * parts,),
        in_specs=[pl.BlockSpec((bm, kdim), lambda s: (s // nn, 0)),
                  pl.BlockSpec((kdim, bn), lambda s: (0, s % nn)),
                  pl.BlockSpec((4, DIFF_QK), lambda s: (0, 0)),
                  pl.BlockSpec((1, HEAD_DIM), lambda s: (0, 0)),
                  new, new, new, cache, cache,
                  pl.BlockSpec((n_heads, ts, SAMPLE_NEAR + ts), lambda s: (0, 0, 0))],
        out_specs=[pl.BlockSpec((bm, bn), lambda s: (s // nn, s % nn)), new],
        out_shape=[jax.ShapeDtypeStruct((m, n), F32),
                   jax.ShapeDtypeStruct((nb, ts, d), BF16)],
        scratch_shapes=[pltpu.VMEM((n_heads, 2 * ts, V7X_LANES), F32),
                        pltpu.VMEM((n_heads, 2 * ts, 2 * HEAD_DIM), F32)],
        compiler_params=_cparams("arbitrary"),
        name="proj_hgrn_prompt_and_diff_attn_sample",
    )(x, w16, lam_p, g, q, kn, vn, kc, vc, bias)


def _split3(x):
    hi = x.astype(BF16)
    r1 = x - hi.astype(F32)
    mid = r1.astype(BF16)
    lo = (r1 - mid.astype(F32)).astype(BF16)
    return hi, mid, lo


def _silu(x):
    h = 0.5 * x
    return h + h * jnp.tanh(h)


def _hgrn_kernel(lbl_ref, g_ref, hq_ref, hf_ref, hi_ref, hg_ref, s0_ref, o_ref, s_out_ref, st_ref, *,
                 n_heads, chain):
    c = pl.program_id(1)
    nc = pl.num_programs(1)
    n_streams = st_ref.shape[0]
    n_chunks = HGRN_CHUNKS_PER_STEP
    rows = n_chunks * CHUNK
    d = n_heads * HEAD_DIM

    @pl.when(c == 0)
    def _():
        for s in range(n_streams):
            for h in range(n_heads):
                st_ref[s, h] = s0_ref[s, h].T

    lg = lbl_ref[...]
    e = jnp.exp(lg - jnp.max(lg, axis=0, keepdims=True))
    lb = e[0:1, :] / jnp.sum(e, axis=0, keepdims=True)

    c_half = 0.5 * (1.0 - lb)
    t = jnp.tanh(0.5 * hf_ref[...].reshape(rows, d))
    ct = c_half * t
    logf = jnp.log2((lb + c_half) + ct)
    kk = c_half - ct
    qq = _silu(hq_ref[...].reshape(rows, d))

    row = lax.broadcasted_iota(jnp.int32, (rows, rows), 0)
    col = lax.broadcasted_iota(jnp.int32, (rows, rows), 1)
    causal = jnp.logical_and(col <= row, col // CHUNK == row // CHUNK)
    tri = jnp.where(causal, 1.0, 0.0).astype(BF16)
    b = None
    for part in _split3(logf):
        term = jnp.dot(tri, part, preferred_element_type=F32)
        b = term if b is None else b + term
    qd = (qq * jnp.exp2(b)).astype(BF16)
    kd32 = kk * jnp.exp2(-b)
    kd = kd32.astype(BF16)
    decays = [jnp.exp2(b[(r + 1) * CHUNK - 1:(r + 1) * CHUNK, :]) for r in range(n_chunks)]
    k_last = jnp.concatenate([kd32[r * CHUNK:(r + 1) * CHUNK] * decays[r] for r in range(n_chunks)],
                             axis=0).astype(BF16)
    v = hi_ref[...].reshape(rows, d).astype(BF16)
    gate = _silu(hg_ref[...].reshape(rows, d))
    g = g_ref[...]

    nt = (((1,), (1,)), ((), ()))
    tn = (((0,), (0,)), ((), ()))
    for h in range(n_heads):
        cols = slice(h * HEAD_DIM, (h + 1) * HEAD_DIM)
        a = lax.dot_general(qd[:, cols], kd[:, cols], nt, preferred_element_type=F32)
        a = jnp.where(causal, a, 0.0)
        o_intra = jnp.dot(a.astype(BF16), v[:, cols], preferred_element_type=F32)
        o_parts = []
        st = st_ref[0, h]
        for r in range(n_chunks):
            rr = slice(r * CHUNK, (r + 1) * CHUNK)
            if not chain:
                st = st_ref[r, h]
            o_parts.append(o_intra[rr] + lax.dot_general(qd[rr, cols], st.astype(BF16), nt,
                                                         preferred_element_type=F32))
            st = decays[r][:, cols] * st + lax.dot_general(v[rr, cols], k_last[rr, cols], tn,
                                                           preferred_element_type=F32)
            if not chain:
                st_ref[r, h] = st
        if chain:
            st_ref[0, h] = st
        o = jnp.concatenate(o_parts, axis=0)
        ms = jnp.mean(o * o, axis=-1, keepdims=True)
        y = (o * lax.rsqrt(ms + EPS) * g * gate[:, cols]).astype(o_ref.dtype)
        if chain:
            o_ref[0, :, cols] = y
        else:
            for r in range(n_chunks):
                o_ref[r, :, cols] = y[r * CHUNK:(r + 1) * CHUNK]

    @pl.when(c == nc - 1)
    def _():
        for s in range(n_streams):
            for h in range(n_heads):
                s_out_ref[s, h] = st_ref[s, h].T


def _hgrn(lb_logits, g, hpart, s0, n_heads):
    nb, t, _ = hpart.shape
    d = n_heads * HEAD_DIM
    per = HGRN_CHUNKS_PER_STEP
    chain = t > CHUNK
    sb, tb = (1, per * CHUNK) if chain else (per, CHUNK)
    assert nb % sb == 0 and t % tb == 0
    kern = functools.partial(_hgrn_kernel, n_heads=n_heads, chain=chain)
    part = lambda p: pl.BlockSpec((sb, tb, d), lambda b, c: (b, c, p))
    state = pl.BlockSpec((sb, n_heads, HEAD_DIM, HEAD_DIM), lambda b, c: (b, 0, 0, 0))
    return pl.pallas_call(
        kern,
        grid=(nb // sb, t // tb),
        in_specs=[pl.BlockSpec(lb_logits.shape, lambda b, c: (0, 0)),
                  pl.BlockSpec((1, HEAD_DIM), lambda b, c: (0, 0)),
                  part(0), part(1), part(2), part(3), state],
        out_specs=[pl.BlockSpec((sb, tb, d), lambda b, c: (b, c, 0)), state],
        out_shape=[jax.ShapeDtypeStruct((nb, t, d), BF16),
                   jax.ShapeDtypeStruct(s0.shape, F32)],
        scratch_shapes=[pltpu.VMEM((sb, n_heads, HEAD_DIM, HEAD_DIM), F32)],
        compiler_params=_cparams("parallel", "arbitrary"),
        name="hgrn2_chunks",
    )(lb_logits, g, hpart, hpart, hpart, hpart, s0)


def _cross_kernel(q_ref, k_ref, v_ref, o_ref, *, heads, hd, split_heads):
    nt = (((1,), (1,)), ((), ()))
    if split_heads:
        tq = q_ref.shape[1]
        n_mem = k_ref.shape[1]
        xk = k_ref[0].reshape(n_mem * heads, hd).astype(BF16)
        xv = v_ref[0].reshape(n_mem * heads, hd).astype(BF16)
        q = jnp.concatenate([q_ref[0, :, h * hd:(h + 1) * hd] for h in range(heads)], axis=0)
        s = lax.dot_general(q, xk, nt, preferred_element_type=F32)
        row_head = lax.broadcasted_iota(jnp.int32, s.shape, 0) // tq
        col_head = lax.rem(lax.broadcasted_iota(jnp.int32, s.shape, 1), heads)
        s = jnp.where(row_head == col_head, s, MASK_VALUE)
        p = jnp.exp(s - jnp.max(s, axis=-1, keepdims=True))
        l = jnp.sum(p, axis=-1, keepdims=True)
        o = jnp.dot(p.astype(BF16), xv, preferred_element_type=F32) * (1.0 / l)
        for h in range(heads):
            o_ref[0, :, h * hd:(h + 1) * hd] = o[h * tq:(h + 1) * tq].astype(o_ref.dtype)
        return
    for h in range(heads):
        cols = slice(h * hd, (h + 1) * hd)
        q = q_ref[0, :, cols]
        k = k_ref[0, :, cols].astype(BF16)
        v = v_ref[0, :, cols].astype(BF16)
        s = lax.dot_general(q, k, nt, preferred_element_type=F32)
        p = jnp.exp(s - jnp.max(s, axis=-1, keepdims=True))
        l = jnp.sum(p, axis=-1, keepdims=True)
        o = jnp.dot(p.astype(BF16), v, preferred_element_type=F32) * (1.0 / l)
        o_ref[0, :, cols] = o.astype(o_ref.dtype)


def _cross_attn(q, mk, mv, tq):
    nb, t, d = q.shape
    split_heads = mk.ndim == 4
    kern = functools.partial(_cross_kernel, heads=MEM_HEADS, hd=d // MEM_HEADS, split_heads=split_heads)
    if split_heads:
        mem_spec = pl.BlockSpec((1,) + mk.shape[1:], lambda b, i: (b, 0, 0, 0))
    else:
        mem_spec = pl.BlockSpec((1, N_MEM, d), lambda b, i: (b, 0, 0))
    return pl.pallas_call(
        kern,
        grid=(nb, t // tq),
        in_specs=[pl.BlockSpec((1, tq, d), lambda b, i: (b, i, 0)),
                  mem_spec,
                  mem_spec],
        out_specs=pl.BlockSpec((1, tq, d), lambda b, i: (b, i, 0)),
        out_shape=jax.ShapeDtypeStruct((nb, t, d), BF16),
        compiler_params=_cparams("parallel", "parallel"),
        name="mem_cross_attn",
    )(q, mk, mv)


def _named_matmul(w, blocks, emit_w16, w16):
    bm, bn = blocks

    def mm(names, xs, n, out_dtypes, **kw):
        pairs = [(xi,) + w[nm] for nm, xi in zip(names, xs)]
        outs = _matmul(pairs, n, out_dtypes, bm=bm, bn=bn, emit_w16=emit_w16, **kw)
        if emit_w16:
            for nm, arr in zip(names, outs[len(out_dtypes):]):
                w16[nm] = (arr, 0, 0)
        return outs[:len(out_dtypes)]

    return mm


def _in_proj(x2, norms, w, widths, *, blocks, emit_w16, with_hgrn):
    w16 = {}
    mm = _named_matmul(w, blocks, emit_w16, w16)
    n = _rmsnorm(x2, norms["mix"], BF16)
    (dq,) = mm(["dq"], [n], widths["diff"], [BF16], scale=DIFF_QK ** -0.5 * LOG2E, name="proj_dq")
    dk32, dk16 = mm(["dk"], [n], widths["diff"], [F32, BF16], name="proj_dk")
    dv32, dv16 = mm(["dv"], [n], widths["diff"], [F32, BF16], name="proj_dv")
    hpart = mm(["h"], [n], widths["hgrn_in"], [F32], name="proj_hgrn")[0] if with_hgrn else None
    return n, dq, dk32, dk16, dv32, dv16, hpart, w16


def _tail(x2, nb, t, cat_diff, cat_hgrn, norms, w, mem_kv_fn, *, blocks, emit_w16):
    m, d = x2.shape
    w16 = {}
    mm = _named_matmul(w, blocks, emit_w16, w16)

    (h1,) = mm(["out_diff", "out_hgrn"], [cat_diff, cat_hgrn], d, [F32], residual=x2, name="proj_out")

    n2 = _rmsnorm(h1, norms["cross"], BF16)
    (cq,) = mm(["mem_q"], [n2], d, [BF16], scale=(d // MEM_HEADS) ** -0.5, name="proj_mem_q")
    mk, mv = mem_kv_fn()
    co = _cross_attn(cq.reshape(nb, t, d), mk, mv, tq=min(t, 512))
    (h2,) = mm(["mem_o"], [co.reshape(m, d)], d, [F32], residual=h1, name="proj_mem_o")

    n3 = _rmsnorm(h2, norms["ffn"], BF16)
    gu = _gateup(n3, w["gate"][0], w["up"][0], bm=FFN_GATE_UP_BLOCK[0], bn=FFN_GATE_UP_BLOCK[1],
                 emit_w16=emit_w16)
    act = gu[0]
    if emit_w16:
        w16["gate"], w16["up"] = (gu[1], 0, 0), (gu[2], 0, 0)
    (h3,) = _matmul([(act,) + w["down"]], d, [F32], residual=h2, bm=FFN_DOWN_BLOCK[0], bn=FFN_DOWN_BLOCK[1],
                    name="ffn_down")
    y = _rmsnorm(h3, norms["final"], F32)
    return y.reshape(nb, t, d), w16


def kernel(x_prompt, x_sample, mem_prompt, cache_diff_k, cache_diff_v, state_hgrn, cache_mem_k, cache_mem_v, norm_mix, w_in, diff_lambda, diff_subln, rel_bias, hgrn_lb_logits, hgrn_norm, w_out, norm_cross, norm_mem, w_mem_q, w_mem_k, w_mem_v, w_mem_o, norm_ffn, w_ffn_gate, w_ffn_up, w_ffn_down, norm_final):
    bp, tp, d = x_prompt.shape
    bs, ts, _ = x_sample.shape
    depth = w_in.shape[0]
    assert depth == 1 and bp == 1
    past = cache_diff_k.shape[2]
    n_dh = cache_diff_k.shape[3]
    d_diff = n_dh * HEAD_DIM
    n_hh = state_hgrn.shape[2]
    d_hgrn = n_hh * HEAD_DIM
    assert past % CHUNK == 0 and ts == CHUNK and tp % ATTN_TILE == 0

    norms = {"mix": norm_mix[0], "cross": norm_cross[0], "ffn": norm_ffn[0], "final": norm_final}
    widths = {"diff": d_diff, "hgrn_in": w_in.shape[2] - 3 * d_diff}
    w_f32 = {
        "dq": (w_in[0], 0, 0), "dk": (w_in[0], 0, d_diff), "dv": (w_in[0], 0, 2 * d_diff),
        "h": (w_in[0], 0, 3 * d_diff),
        "out_diff": (w_out[0], 0, 0), "out_hgrn": (w_out[0], d_diff, 0),
        "mem_q": (w_mem_q[0], 0, 0), "mem_o": (w_mem_o[0], 0, 0),
        "gate": (w_ffn_gate[0], 0, 0), "up": (w_ffn_up[0], 0, 0),
        "down": (w_ffn_down[0].astype(BF16), 0, 0),
    }
    lam_p = diff_lambda[0].astype(F32)
    subln = diff_subln[0].reshape(1, HEAD_DIM)
    hnorm = hgrn_norm[0].reshape(1, HEAD_DIM)

    tile = ATTN_TILE
    r = jnp.arange(tile)[:, None]
    c = jnp.arange(2 * tile)[None, :]
    rel_p = (c - tile) - r
    vis_p = jnp.logical_or(c < tile, (c - tile) // CHUNK <= r // CHUNK)
    bias_p = _bias_tiles(rel_bias, _rel_bucket(rel_p), jnp.where(vis_p, 0.0, MASK_VALUE).astype(F32),
                         _rel_bucket(jnp.int32(-2 * tile)), n_dh)
    qpos = past + jnp.arange(ts)[:, None]
    kpos = (past - SAMPLE_NEAR) + jnp.arange(SAMPLE_NEAR + ts)[None, :]
    vis_s = kpos // CHUNK <= qpos // CHUNK
    bias_s = _bias_tiles(rel_bias, _rel_bucket(kpos - qpos), jnp.where(vis_s, 0.0, MASK_VALUE).astype(F32),
                         _rel_bucket(jnp.int32(-(SAMPLE_NEAR + 1))), n_dh)

    mem_out = {}

    def mem_kv_prompt():
        mn = _rmsnorm(mem_prompt.reshape(bp * N_MEM, d), norm_mem[0], BF16)
        bm, bn = MM_BLOCK_F32_WEIGHTS
        (mk,) = _matmul([(mn, w_mem_k[0], 0, 0)], d, [F32], bm=bm, bn=bn, name="proj_mem_k")
        (mv,) = _matmul([(mn, w_mem_v[0], 0, 0)], d, [F32], bm=bm, bn=bn, name="proj_mem_v")
        mem_out["k"], mem_out["v"] = mk, mv
        return mk.reshape(bp, N_MEM, d), mv.reshape(bp, N_MEM, d)

    def mem_kv_sample():
        return cache_mem_k[0], cache_mem_v[0]

    xs2 = x_sample.reshape(bs * ts, d)
    xp2 = x_prompt.reshape(bp * tp, d)
    _, dq_s, dk_s, dk16_s, dv_s, dv16_s, hpart_s, w16_in = _in_proj(
        xs2, norms, w_f32, widths, blocks=MM_BLOCK_F32_WEIGHTS, emit_w16=True, with_hgrn=True)
    n_p, dq_p, dk_p, dk16_p, dv_p, dv16_p, _, _ = _in_proj(
        xp2, norms, w16_in, widths, blocks=MM_BLOCK_BF16_WEIGHTS, emit_w16=False, with_hgrn=False)

    shp = (bs, ts, d_diff)
    hpart_p, cat_diff_s = _proj_and_attn_sample(
        n_p, w16_in["h"][0], lam_p, subln, dq_s.reshape(shp), dk16_s.reshape(shp), dv16_s.reshape(shp),
        cache_diff_k[0].reshape(bs, past * n_dh, HEAD_DIM), cache_diff_v[0].reshape(bs, past * n_dh, HEAD_DIM),
        bias_s, n_dh)

    cat_hgrn_s, s_s = _hgrn(hgrn_lb_logits, hnorm, hpart_s.reshape(bs, ts, 4 * d_hgrn), state_hgrn[0], n_hh)
    y_s, w16_tail = _tail(xs2, bs, ts, cat_diff_s.reshape(bs * ts, d_diff), cat_hgrn_s.reshape(bs * ts, d_hgrn),
                          norms, w_f32, mem_kv_sample, blocks=MM_BLOCK_F32_WEIGHTS, emit_w16=True)

    w_bf16 = {**w16_in, **w16_tail, "down": w_f32["down"]}
    cat_diff_p = _attn_prompt(lam_p, subln, dq_p, dk16_p, dv16_p, bias_p, n_dh)
    s0 = jnp.zeros((bp, n_hh, HEAD_DIM, HEAD_DIM), F32)
    cat_hgrn_p, s_p = _hgrn(hgrn_lb_logits, hnorm, hpart_p.reshape(bp, tp, 4 * d_hgrn), s0, n_hh)
    y_p, _ = _tail(xp2, bp, tp, cat_diff_p, cat_hgrn_p.reshape(bp * tp, d_hgrn), norms, w_bf16, mem_kv_prompt,
                   blocks=MM_BLOCK_BF16_WEIGHTS, emit_w16=False)

    mhd = d // MEM_HEADS
    return (y_p, y_s,
            dk_p.reshape(depth, bp, tp, n_dh, HEAD_DIM), dv_p.reshape(depth, bp, tp, n_dh, HEAD_DIM),
            s_p.reshape(depth, bp, n_hh, HEAD_DIM, HEAD_DIM),
            mem_out["k"].reshape(depth, bp, N_MEM, MEM_HEADS, mhd),
            mem_out["v"].reshape(depth, bp, N_MEM, MEM_HEADS, mhd),
            dk_s.reshape(depth, bs, ts, n_dh, HEAD_DIM), dv_s.reshape(depth, bs, ts, n_dh, HEAD_DIM),
            s_s.reshape(depth, bs, n_hh, HEAD_DIM, HEAD_DIM))
```

```python
import functools
import math

import jax
import jax.numpy as jnp
from jax import lax
from jax.experimental import pallas as pl
from jax.experimental.pallas import tpu as pltpu

F32 = jnp.float32
BF16 = jnp.bfloat16

CHUNK = 64
HEAD_DIM = 128
DIFF_QK = HEAD_DIM // 2
N_MEM = 256
MEM_HEADS = 4
REL_BUCKETS = 32
REL_MAX_DIST = 128
EPS = 1e-6
LAM_INIT_L0 = 0.8 - 0.6 * math.exp(-0.3 * 0)
MASK_VALUE = -1e30
LOG2E = math.log2(math.e)

V7X_LANES = 128
V7X_SCOPED_VMEM_LIMIT_BYTES = 60000 * 1024
MM_BLOCK_F32_WEIGHTS = (2048, 256)
MM_BLOCK_BF16_WEIGHTS = (1024, 1024)
FFN_GATE_UP_BLOCK = (2048, 256)
FFN_DOWN_BLOCK = (512, 512)

DIAG_SLOT = 2
ATTN_TILE = 512
SAMPLE_CACHE_PARTS = 4
FUSED_PROJ_BLOCK = (1024, 512)
HGRN_CHUNKS_PER_STEP = 4
SAMPLE_NEAR = 128


def _cparams(*sem):
    return pltpu.CompilerParams(dimension_semantics=sem,
                                vmem_limit_bytes=V7X_SCOPED_VMEM_LIMIT_BYTES)


def _rmsnorm_kernel(x_ref, g_ref, o_ref):
    x = x_ref[...]
    ms = jnp.mean(x * x, axis=-1, keepdims=True)
    o_ref[...] = (x * lax.rsqrt(ms + EPS) * g_ref[...]).astype(o_ref.dtype)


def _rmsnorm(x, g, out_dtype, rows=512):
    m, d = x.shape
    rows = min(rows, m)
    return pl.pallas_call(
        _rmsnorm_kernel,
        grid=(m // rows,),
        in_specs=[pl.BlockSpec((rows, d), lambda i: (i, 0)),
                  pl.BlockSpec((1, d), lambda i: (0, 0))],
        out_specs=pl.BlockSpec((rows, d), lambda i: (i, 0)),
        out_shape=jax.ShapeDtypeStruct((m, d), out_dtype),
        compiler_params=_cparams("parallel"),
        name="rmsnorm",
    )(x, g.reshape(1, d))


def _mm_kernel(*refs, n_pairs, has_res, n_out, scale):
    xs = refs[0:2 * n_pairs:2]
    ws = refs[1:2 * n_pairs:2]
    pos = 2 * n_pairs
    res_ref = refs[pos] if has_res else None
    pos += int(has_res)
    outs = refs[pos:pos + n_out]
    w16_outs = refs[pos + n_out:]

    total = None
    for p, (x_ref, w_ref) in enumerate(zip(xs, ws)):
        wb = w_ref[...].astype(BF16)
        if w16_outs:
            w16_outs[p][...] = wb
        d = jnp.dot(x_ref[...], wb, preferred_element_type=F32)
        total = d if total is None else total + d
    if scale != 1.0:
        total = total * scale
    if has_res:
        total = total + res_ref[...]
    for o_ref in outs:
        o_ref[...] = total.astype(o_ref.dtype)


def _matmul(pairs, n, out_dtypes, *, residual=None, scale=1.0, bm, bn, emit_w16=False, name="matmul"):
    m = pairs[0][0].shape[0]
    bm = min(bm, m)
    bn = min(bn, n)
    assert m % bm == 0 and n % bn == 0
    in_specs, args = [], []
    for x, w, r0, c0 in pairs:
        kx = x.shape[1]
        assert r0 % kx == 0 and c0 % bn == 0
        in_specs.append(pl.BlockSpec((bm, kx), lambda j, i: (i, 0)))
        in_specs.append(pl.BlockSpec((kx, bn), lambda j, i, rb=r0 // kx, cb=c0 // bn: (rb, cb + j)))
        args += [x, w]
    if residual is not None:
        in_specs.append(pl.BlockSpec((bm, bn), lambda j, i: (i, j)))
        args.append(residual)
    out_specs = [pl.BlockSpec((bm, bn), lambda j, i: (i, j)) for _ in out_dtypes]
    out_shape = [jax.ShapeDtypeStruct((m, n), dt) for dt in out_dtypes]
    if emit_w16:
        for x, _, _, _ in pairs:
            out_specs.append(pl.BlockSpec((x.shape[1], bn), lambda j, i: (0, j)))
            out_shape.append(jax.ShapeDtypeStruct((x.shape[1], n), BF16))
    kern = functools.partial(_mm_kernel, n_pairs=len(pairs), has_res=residual is not None,
                             n_out=len(out_dtypes), scale=scale)
    return pl.pallas_call(
        kern,
        grid=(n // bn, m // bm),
        in_specs=in_specs,
        out_specs=out_specs,
        out_shape=out_shape,
        compiler_params=_cparams("parallel", "arbitrary"),
        name=name,
    )(*args)


def _gateup_kernel(*refs, emit_w16, has_cast):
    x_ref, wg_ref, wu_ref = refs[:3]
    pos = 3 + int(has_cast)
    o_ref = refs[pos]
    x = x_ref[...]
    wg = wg_ref[...].astype(BF16)
    wu = wu_ref[...].astype(BF16)
    if emit_w16:
        refs[pos + 1][...] = wg
        refs[pos + 2][...] = wu
    if has_cast:
        refs[-1][...] = refs[3][...].astype(BF16)
    g = jnp.dot(x, wg, preferred_element_type=F32)
    u = jnp.dot(x, wu, preferred_element_type=F32)
    o_ref[...] = (_silu(g) * u).astype(o_ref.dtype)


def _gateup(x, wg, wu, *, bm, bn, emit_w16=False, cast_src=None):
    m, kdim = x.shape
    n = wg.shape[1]
    bm = min(bm, m)
    assert m % bm == 0 and n % bn == 0
    steps = n // bn
    w_spec = pl.BlockSpec((kdim, bn), lambda j, i: (0, j))
    x_spec = pl.BlockSpec((bm, kdim), lambda j, i: (i, 0))
    in_specs, args = [x_spec, w_spec, w_spec], [x, wg, wu]
    out_specs = [pl.BlockSpec((bm, bn), lambda j, i: (i, j))]
    out_shape = [jax.ShapeDtypeStruct((m, n), BF16)]
    if emit_w16:
        out_specs += [w_spec, w_spec]
        out_shape += [jax.ShapeDtypeStruct((kdim, n), BF16)] * 2
    if cast_src is not None:
        rows, width = cast_src.shape
        assert m == bm and rows % steps == 0 and (rows // steps) % 8 == 0
        in_specs[0] = pl.BlockSpec((bm, kdim), lambda j, i: (i, 0), pipeline_mode=pl.Buffered(1))
        cast_spec = pl.BlockSpec((rows // steps, width), lambda j, i: (j, 0))
        in_specs.append(cast_spec)
        args.append(cast_src)
        out_specs.append(cast_spec)
        out_shape.append(jax.ShapeDtypeStruct((rows, width), BF16))
    kern = functools.partial(_gateup_kernel, emit_w16=emit_w16, has_cast=cast_src is not None)
    return pl.pallas_call(
        kern,
        grid=(steps, m // bm),
        in_specs=in_specs,
        out_specs=out_specs,
        out_shape=out_shape,
        compiler_params=_cparams("parallel", "arbitrary"),
        name="ffn_gate_up",
    )(*args)


def _rel_bucket(rel):
    nb = REL_BUCKETS // 2
    max_exact = nb // 2
    ret = jnp.where(rel > 0, nb, 0)
    n = jnp.abs(rel)
    nf = jnp.maximum(n, 1).astype(F32)
    large = max_exact + (jnp.log(nf / max_exact) / math.log(REL_MAX_DIST / max_exact)
                         * (nb - max_exact)).astype(jnp.int32)
    large = jnp.minimum(large, nb - 1)
    return ret + jnp.where(n < max_exact, n, large)


def _bias_tile_kernel(far_ref, table_ref, bucket_ref, mask_ref, o_ref):
    h = pl.program_id(0)
    bucket = bucket_ref[...]
    acc = jnp.zeros(bucket.shape, F32)
    for b in range(REL_BUCKETS):
        acc = jnp.where(bucket == b, table_ref[b, h], acc)
    o_ref[0] = (acc - table_ref[far_ref[0], h]) * LOG2E + mask_ref[...]


def _bias_tiles(table, bucket, maskadd, far_bucket, n_heads):
    r, c = bucket.shape
    return pl.pallas_call(
        _bias_tile_kernel,
        grid=(n_heads,),
        in_specs=[pl.BlockSpec(memory_space=pltpu.SMEM),
                  pl.BlockSpec(memory_space=pltpu.SMEM),
                  pl.BlockSpec((r, c), lambda h: (0, 0)),
                  pl.BlockSpec((r, c), lambda h: (0, 0))],
        out_specs=pl.BlockSpec((1, r, c), lambda h: (h, 0, 0)),
        out_shape=jax.ShapeDtypeStruct((n_heads, r, c), F32),
        compiler_params=_cparams("arbitrary"),
        name="rel_bias_tiles",
    )(far_bucket.reshape(1).astype(jnp.int32), table.astype(F32), bucket.astype(jnp.int32), maskadd)


def _diff_lambda(lp):
    a = jnp.sum(lp[0:1, :] * lp[1:2, :], axis=-1, keepdims=True)
    b = jnp.sum(lp[2:3, :] * lp[3:4, :], axis=-1, keepdims=True)
    return jnp.exp(a) - jnp.exp(b) + LAM_INIT_L0


def _stack_maps(q):
    lane = lax.broadcasted_iota(jnp.int32, q.shape, 1)
    zero = jnp.zeros_like(q)
    return jnp.concatenate([jnp.where(lane < DIFF_QK, q, zero),
                            jnp.where(lane >= DIFF_QK, q, zero)], axis=0)


def _diff_epilogue(acc, l, lam, g, tq):
    o = acc * (1.0 / l)
    o = o[:tq] - lam * o[tq:]
    ms = jnp.mean(o * o, axis=-1, keepdims=True)
    return o * lax.rsqrt(ms + EPS) * g * (1.0 - LAM_INIT_L0)


def _attn_prompt_kernel(lam_ref, g_ref, q_ref, qn_ref, k_ref, v_ref, bias_ref, o_ref, qs_ref, s_ref, m_ref,
                        acc_ref, *, tile):
    i = pl.program_id(1)
    n_lane_tiles = tile // V7X_LANES
    ones = jnp.ones((tile, V7X_LANES), BF16)
    nt = (((1,), (1,)), ((), ()))

    qs_ref[...] = _stack_maps(q_ref[...])
    m_ref[...] = jnp.full(m_ref.shape, MASK_VALUE, F32)
    acc_ref[...] = jnp.zeros(acc_ref.shape, F32)

    def key_rows(d):
        return pl.ds(pl.multiple_of((i - d) * tile, tile), tile)

    def scores(d, slot):
        s_ref[slot] = lax.dot_general(qs_ref[...], k_ref[key_rows(d), :], nt, preferred_element_type=F32)

    def absorb(d, slot, bias):
        vt = jnp.concatenate([v_ref[key_rows(d), :], ones], axis=1)
        s = s_ref[slot]
        if bias is not None:
            s = (s.reshape(2, tile, tile) + bias[None]).reshape(2 * tile, tile)
        cols = [s[:, c * V7X_LANES:(c + 1) * V7X_LANES] for c in range(n_lane_tiles)]
        mx = cols[0]
        for sc in cols[1:]:
            mx = jnp.maximum(mx, sc)
        m_prev = m_ref[...]
        m_new = jnp.maximum(m_prev, jnp.max(mx, axis=-1, keepdims=True))
        alpha = jnp.exp2(m_prev - m_new)
        p = jnp.concatenate([jnp.exp2(sc - m_new).astype(BF16) for sc in cols], axis=1)
        pv = jnp.dot(p, vt, preferred_element_type=F32)
        acc_ref[...] = jnp.concatenate([alpha, alpha], axis=1) * acc_ref[...] + pv
        m_ref[...] = m_new

    i_even = lax.rem(i, 2) == 0

    @pl.when(i == 0)
    def _():
        scores(0, DIAG_SLOT)

    @pl.when(jnp.logical_and(i_even, i >= 2))
    def _():
        scores(i - 1, 1)
        absorb(i, 0, None)

    d_odd = i - 1 + lax.rem(i, 2)

    def pair_body(u, carry):
        d = d_odd - 2 * u
        scores(d - 1, 0)
        absorb(d, 1, None)
        scores(d - 2, 1)
        absorb(d - 1, 0, None)
        return carry

    lax.fori_loop(0, jnp.where(d_odd >= 3, (d_odd - 1) // 2, 0), pair_body, 0)

    @pl.when(i >= 1)
    def _():
        scores(0, DIAG_SLOT)
        absorb(1, 1, bias_ref[0, :, :tile])

    def diagonal(next_slot):
        if next_slot is not None:
            s_ref[next_slot] = lax.dot_general(_stack_maps(qn_ref[...]), k_ref[pl.ds(0, tile), :], nt,
                                               preferred_element_type=F32)
        absorb(0, DIAG_SLOT, bias_ref[0, :, tile:])

    has_next = i + 1 < pl.num_programs(1)

    @pl.when(jnp.logical_and(has_next, jnp.logical_not(i_even)))
    def _():
        diagonal(0)

    @pl.when(jnp.logical_and(has_next, i_even))
    def _():
        diagonal(1)

    @pl.when(jnp.logical_not(has_next))
    def _():
        diagonal(None)

    acc = acc_ref[...]
    y = _diff_epilogue(acc[:, :HEAD_DIM], acc[:, HEAD_DIM:], _diff_lambda(lam_ref[...]), g_ref[...], tile)
    o_ref[...] = y.astype(o_ref.dtype)


def _attn_prompt(lam_p, g, q, k, v, bias, n_heads, tile=ATTN_TILE):
    t = q.shape[0]
    nq = t // tile
    kern = functools.partial(_attn_prompt_kernel, tile=tile)
    return pl.pallas_call(
        kern,
        grid=(n_heads, nq),
        in_specs=[pl.BlockSpec((4, DIFF_QK), lambda h, i: (0, 0)),
                  pl.BlockSpec((1, HEAD_DIM), lambda h, i: (0, 0)),
                  pl.BlockSpec((tile, HEAD_DIM), lambda h, i: (i, h)),
                  pl.BlockSpec((tile, HEAD_DIM), lambda h, i: (jnp.minimum(i + 1, nq - 1), h)),
                  pl.BlockSpec((t, HEAD_DIM), lambda h, i: (0, h)),
                  pl.BlockSpec((t, HEAD_DIM), lambda h, i: (0, h)),
                  pl.BlockSpec((1, tile, 2 * tile), lambda h, i: (h, 0, 0))],
        out_specs=pl.BlockSpec((tile, HEAD_DIM), lambda h, i: (i, h)),
        out_shape=jax.ShapeDtypeStruct((t, n_heads * HEAD_DIM), BF16),
        scratch_shapes=[pltpu.VMEM((2 * tile, HEAD_DIM), BF16),
                        pltpu.VMEM((DIAG_SLOT + 1, 2 * tile, tile), F32),
                        pltpu.VMEM((2 * tile, V7X_LANES), F32),
                        pltpu.VMEM((2 * tile, 2 * HEAD_DIM), F32)],
        compiler_params=_cparams("parallel", "arbitrary"),
        name="diff_attn_prompt",
    )(lam_p, g, q, q, k, v, bias)


def _attn_sample_step(part, n_parts, lam_ref, g_ref, q_ref, kn_ref, vn_ref, kc_ref, vc_ref, bias_ref, o_ref,
                      m_ref, acc_ref, *, n_heads, keys, ts):
    is_last = part == n_parts - 1
    n_lane_tiles = keys // V7X_LANES
    ones_c = jnp.ones((keys, V7X_LANES), BF16)
    ones_n = jnp.ones((ts, V7X_LANES), BF16)
    nt = (((1,), (1,)), ((), ()))

    @pl.when(part == 0)
    def _():
        m_ref[...] = jnp.full(m_ref.shape, MASK_VALUE, F32)
        acc_ref[...] = jnp.zeros(acc_ref.shape, F32)

    for h in range(n_heads):
        col = slice(h * HEAD_DIM, (h + 1) * HEAD_DIM)
        rows = pl.ds(h, keys, stride=n_heads)
        qs = _stack_maps(q_ref[0, :, col])
        kc = kc_ref[0, rows, :].astype(BF16)
        vc = jnp.concatenate([vc_ref[0, rows, :].astype(BF16), ones_c], axis=1)
        vn = jnp.concatenate([vn_ref[0, :, col], ones_n], axis=1)
        bias = bias_ref[h]
        near_bias = jnp.where(is_last, bias[:, :SAMPLE_NEAR], 0.0)
        new_bias = jnp.where(is_last, bias[:, SAMPLE_NEAR:], MASK_VALUE)

        s = lax.dot_general(qs, kc, nt, preferred_element_type=F32)
        cols = [s[:, c * V7X_LANES:(c + 1) * V7X_LANES] for c in range(n_lane_tiles)]
        cols[-1] = (cols[-1].reshape(2, ts, SAMPLE_NEAR) + near_bias[None]).reshape(2 * ts, SAMPLE_NEAR)
        s_new = lax.dot_general(qs, kn_ref[0, :, col], nt, preferred_element_type=F32)
        s_new = (s_new.reshape(2, ts, ts) + new_bias[None]).reshape(2 * ts, ts)

        mx = cols[0]
        for sc in cols[1:]:
            mx = jnp.maximum(mx, sc)
        m_prev = m_ref[h]
        m_new = jnp.maximum(m_prev, jnp.maximum(jnp.max(mx, axis=-1, keepdims=True),
                                                jnp.max(s_new, axis=-1, keepdims=True)))
        alpha = jnp.exp2(m_prev - m_new)
        p = jnp.concatenate([jnp.exp2(sc - m_new).astype(BF16) for sc in cols], axis=1)
        p_new = jnp.exp2(s_new - m_new[:, :ts]).astype(BF16)
        pv = (jnp.dot(p, vc, preferred_element_type=F32)
              + jnp.dot(p_new, vn, preferred_element_type=F32))
        acc_ref[h] = jnp.concatenate([alpha, alpha], axis=1) * acc_ref[h] + pv
        m_ref[h] = m_new

    @pl.when(is_last)
    def _():
        lam = _diff_lambda(lam_ref[...])
        g = g_ref[...]
        for h in range(n_heads):
            acc = acc_ref[h]
            y = _diff_epilogue(acc[:, :HEAD_DIM], acc[:, HEAD_DIM:], lam, g, ts)
            o_ref[0, :, h * HEAD_DIM:(h + 1) * HEAD_DIM] = y.astype(o_ref.dtype)


def _proj_and_attn_sample_kernel(x_ref, w_ref, lam_ref, g_ref, q_ref, kn_ref, vn_ref, kc_ref, vc_ref, bias_ref,
                                 y_ref, o_ref, m_ref, acc_ref, *, parts, n_heads, keys, ts):
    y_ref[...] = jnp.dot(x_ref[...], w_ref[...], preferred_element_type=F32)
    part = lax.rem(pl.program_id(0), parts)
    _attn_sample_step(part, parts, lam_ref, g_ref, q_ref, kn_ref, vn_ref, kc_ref, vc_ref, bias_ref, o_ref,
                      m_ref, acc_ref, n_heads=n_heads, keys=keys, ts=ts)


def _proj_and_attn_sample(x, w16, lam_p, g, q, kn, vn, kc, vc, bias, n_heads):
    m, kdim = x.shape
    n = w16.shape[1]
    bm, bn = FUSED_PROJ_BLOCK
    nm, nn = m // bm, n // bn
    nb, ts, d = q.shape
    past = kc.shape[1] // n_heads
    parts = SAMPLE_CACHE_PARTS
    keys = past // parts
    assert m % bm == 0 and n % bn == 0 and nm * nn == nb * parts
    assert past % parts == 0 and keys % V7X_LANES == 0 and SAMPLE_NEAR == V7X_LANES
    kern = functools.partial(_proj_and_attn_sample_kernel, parts=parts, n_heads=n_heads, keys=keys, ts=ts)
    new = pl.BlockSpec((1, ts, d), lambda s: (s // parts, 0, 0))
    cache = pl.BlockSpec((1, keys * n_heads, HEAD_DIM), lambda s: (s // parts, s % parts, 0))
    return pl.pallas_call(
        kern,
        grid=(nb * parts,),
        in_specs=[pl.BlockSpec((bm, kdim), lambda s: (s // nn, 0)),
                  pl.BlockSpec((kdim, bn), lambda s: (0, s % nn)),
                  pl.BlockSpec((4, DIFF_QK), lambda s: (0, 0)),
                  pl.BlockSpec((1, HEAD_DIM), lambda s: (0, 0)),
                  new, new, new, cache, cache,
                  pl.BlockSpec((n_heads, ts, SAMPLE_NEAR + ts), lambda s: (0, 0, 0))],
        out_specs=[pl.BlockSpec((bm, bn), lambda s: (s // nn, s % nn)), new],
        out_shape=[jax.ShapeDtypeStruct((m, n), F32),
                   jax.ShapeDtypeStruct((nb, ts, d), BF16)],
        scratch_shapes=[pltpu.VMEM((n_heads, 2 * ts, V7X_LANES), F32),
                        pltpu.VMEM((n_heads, 2 * ts, 2 * HEAD_DIM), F32)],
        compiler_params=_cparams("arbitrary"),
        name="proj_hgrn_prompt_and_diff_attn_sample",
    )(x, w16, lam_p, g, q, kn, vn, kc, vc, bias)


def _split3(x):
    hi = x.astype(BF16)
    r1 = x - hi.astype(F32)
    mid = r1.astype(BF16)
    lo = (r1 - mid.astype(F32)).astype(BF16)
    return hi, mid, lo


def _silu(x):
    h = 0.5 * x
    return h + h * jnp.tanh(h)


def _hgrn_kernel(lbl_ref, g_ref, hq_ref, hf_ref, hi_ref, hg_ref, s0_ref, o_ref, s_out_ref, st_ref, *,
                 n_heads, chain):
    c = pl.program_id(1)
    nc = pl.num_programs(1)
    n_streams = st_ref.shape[0]
    n_chunks = HGRN_CHUNKS_PER_STEP
    rows = n_chunks * CHUNK
    d = n_heads * HEAD_DIM

    @pl.when(c == 0)
    def _():
        for s in range(n_streams):
            for h in range(n_heads):
                st_ref[s, h] = s0_ref[s, h].T

    lg = lbl_ref[...]
    e = jnp.exp(lg - jnp.max(lg, axis=0, keepdims=True))
    lb = e[0:1, :] / jnp.sum(e, axis=0, keepdims=True)

    c_half = 0.5 * (1.0 - lb)
    t = jnp.tanh(0.5 * hf_ref[...].reshape(rows, d))
    ct = c_half * t
    logf = jnp.log2((lb + c_half) + ct)
    kk = c_half - ct
    qq = _silu(hq_ref[...].reshape(rows, d))

    row = lax.broadcasted_iota(jnp.int32, (rows, rows), 0)
    col = lax.broadcasted_iota(jnp.int32, (rows, rows), 1)
    causal = jnp.logical_and(col <= row, col // CHUNK == row // CHUNK)
    tri = jnp.where(causal, 1.0, 0.0).astype(BF16)
    b = None
    for part in _split3(logf):
        term = jnp.dot(tri, part, preferred_element_type=F32)
        b = term if b is None else b + term
    qd = (qq * jnp.exp2(b)).astype(BF16)
    kd32 = kk * jnp.exp2(-b)
    kd = kd32.astype(BF16)
    decays = [jnp.exp2(b[(r + 1) * CHUNK - 1:(r + 1) * CHUNK, :]) for r in range(n_chunks)]
    k_last = jnp.concatenate([kd32[r * CHUNK:(r + 1) * CHUNK] * decays[r] for r in range(n_chunks)],
                             axis=0).astype(BF16)
    v = hi_ref[...].reshape(rows, d).astype(BF16)
    gate = _silu(hg_ref[...].reshape(rows, d))
    g = g_ref[...]

    nt = (((1,), (1,)), ((), ()))
    tn = (((0,), (0,)), ((), ()))
    for h in range(n_heads):
        cols = slice(h * HEAD_DIM, (h + 1) * HEAD_DIM)
        a = lax.dot_general(qd[:, cols], kd[:, cols], nt, preferred_element_type=F32)
        a = jnp.where(causal, a, 0.0)
        o_intra = jnp.dot(a.astype(BF16), v[:, cols], preferred_element_type=F32)
        o_parts = []
        st = st_ref[0, h]
        for r in range(n_chunks):
            rr = slice(r * CHUNK, (r + 1) * CHUNK)
            if not chain:
                st = st_ref[r, h]
            o_parts.append(o_intra[rr] + lax.dot_general(qd[rr, cols], st.astype(BF16), nt,
                                                         preferred_element_type=F32))
            st = decays[r][:, cols] * st + lax.dot_general(v[rr, cols], k_last[rr, cols], tn,
                                                           preferred_element_type=F32)
            if not chain:
                st_ref[r, h] = st
        if chain:
            st_ref[0, h] = st
        o = jnp.concatenate(o_parts, axis=0)
        ms = jnp.mean(o * o, axis=-1, keepdims=True)
        y = (o * lax.rsqrt(ms + EPS) * g * gate[:, cols]).astype(o_ref.dtype)
        if chain:
            o_ref[0, :, cols] = y
        else:
            for r in range(n_chunks):
                o_ref[r, :, cols] = y[r * CHUNK:(r + 1) * CHUNK]

    @pl.when(c == nc - 1)
    def _():
        for s in range(n_streams):
            for h in range(n_heads):
                s_out_ref[s, h] = st_ref[s, h].T


def _hgrn(lb_logits, g, hpart, s0, n_heads):
    nb, t, _ = hpart.shape
    d = n_heads * HEAD_DIM
    per = HGRN_CHUNKS_PER_STEP
    chain = t > CHUNK
    sb, tb = (1, per * CHUNK) if chain else (per, CHUNK)
    assert nb % sb == 0 and t % tb == 0
    kern = functools.partial(_hgrn_kernel, n_heads=n_heads, chain=chain)
    part = lambda p: pl.BlockSpec((sb, tb, d), lambda b, c: (b, c, p))
    state = pl.BlockSpec((sb, n_heads, HEAD_DIM, HEAD_DIM), lambda b, c: (b, 0, 0, 0))
    return pl.pallas_call(
        kern,
        grid=(nb // sb, t // tb),
        in_specs=[pl.BlockSpec(lb_logits.shape, lambda b, c: (0, 0)),
                  pl.BlockSpec((1, HEAD_DIM), lambda b, c: (0, 0)),
                  part(0), part(1), part(2), part(3), state],
        out_specs=[pl.BlockSpec((sb, tb, d), lambda b, c: (b, c, 0)), state],
        out_shape=[jax.ShapeDtypeStruct((nb, t, d), BF16),
                   jax.ShapeDtypeStruct(s0.shape, F32)],
        scratch_shapes=[pltpu.VMEM((sb, n_heads, HEAD_DIM, HEAD_DIM), F32)],
        compiler_params=_cparams("parallel", "arbitrary"),
        name="hgrn2_chunks",
    )(lb_logits, g, hpart, hpart, hpart, hpart, s0)


def _cross_kernel(q_ref, k_ref, v_ref, o_ref, *, heads, hd, split_heads):
    nt = (((1,), (1,)), ((), ()))
    if split_heads:
        tq = q_ref.shape[1]
        n_mem = k_ref.shape[1]
        xk = k_ref[0].reshape(n_mem * heads, hd).astype(BF16)
        xv = v_ref[0].reshape(n_mem * heads, hd).astype(BF16)
        q = jnp.concatenate([q_ref[0, :, h * hd:(h + 1) * hd] for h in range(heads)], axis=0)
        s = lax.dot_general(q, xk, nt, preferred_element_type=F32)
        row_head = lax.broadcasted_iota(jnp.int32, s.shape, 0) // tq
        col_head = lax.rem(lax.broadcasted_iota(jnp.int32, s.shape, 1), heads)
        s = jnp.where(row_head == col_head, s, MASK_VALUE)
        p = jnp.exp(s - jnp.max(s, axis=-1, keepdims=True))
        l = jnp.sum(p, axis=-1, keepdims=True)
        o = jnp.dot(p.astype(BF16), xv, preferred_element_type=F32) * (1.0 / l)
        for h in range(heads):
            o_ref[0, :, h * hd:(h + 1) * hd] = o[h * tq:(h + 1) * tq].astype(o_ref.dtype)
        return
    for h in range(heads):
        cols = slice(h * hd, (h + 1) * hd)
        q = q_ref[0, :, cols]
        k = k_ref[0, :, cols].astype(BF16)
        v = v_ref[0, :, cols].astype(BF16)
        s = lax.dot_general(q, k, nt, preferred_element_type=F32)
        p = jnp.exp(s - jnp.max(s, axis=-1, keepdims=True))
        l = jnp.sum(p, axis=-1, keepdims=True)
        o = jnp.dot(p.astype(BF16), v, preferred_element_type=F32) * (1.0 / l)
        o_ref[0, :, cols] = o.astype(o_ref.dtype)


def _cross_attn(q, mk, mv, tq):
    nb, t, d = q.shape
    split_heads = mk.ndim == 4
    kern = functools.partial(_cross_kernel, heads=MEM_HEADS, hd=d // MEM_HEADS, split_heads=split_heads)
    if split_heads:
        mem_spec = pl.BlockSpec((1,) + mk.shape[1:], lambda b, i: (b, 0, 0, 0))
    else:
        mem_spec = pl.BlockSpec((1, N_MEM, d), lambda b, i: (b, 0, 0))
    return pl.pallas_call(
        kern,
        grid=(nb, t // tq),
        in_specs=[pl.BlockSpec((1, tq, d), lambda b, i: (b, i, 0)),
                  mem_spec,
                  mem_spec],
        out_specs=pl.BlockSpec((1, tq, d), lambda b, i: (b, i, 0)),
        out_shape=jax.ShapeDtypeStruct((nb, t, d), BF16),
        compiler_params=_cparams("parallel", "parallel"),
        name="mem_cross_attn",
    )(q, mk, mv)


def _named_matmul(w, blocks, emit_w16, w16):
    bm, bn = blocks

    def mm(names, xs, n, out_dtypes, **kw):
        pairs = [(xi,) + w[nm] for nm, xi in zip(names, xs)]
        outs = _matmul(pairs, n, out_dtypes, bm=bm, bn=bn, emit_w16=emit_w16, **kw)
        if emit_w16:
            for nm, arr in zip(names, outs[len(out_dtypes):]):
                w16[nm] = (arr, 0, 0)
        return outs[:len(out_dtypes)]

    return mm


def _in_proj(x2, norms, w, widths, *, blocks, emit_w16, with_hgrn):
    w16 = {}
    mm = _named_matmul(w, blocks, emit_w16, w16)
    n = _rmsnorm(x2, norms["mix"], BF16)
    (dq,) = mm(["dq"], [n], widths["diff"], [BF16], scale=DIFF_QK ** -0.5 * LOG2E, name="proj_dq")
    dk32, dk16 = mm(["dk"], [n], widths["diff"], [F32, BF16], name="proj_dk")
    dv32, dv16 = mm(["dv"], [n], widths["diff"], [F32, BF16], name="proj_dv")
    hpart = mm(["h"], [n], widths["hgrn_in"], [F32], name="proj_hgrn")[0] if with_hgrn else None
    return n, dq, dk32, dk16, dv32, dv16, hpart, w16


def _tail(x2, nb, t, cat_diff, cat_hgrn, norms, w, mem_kv_fn, *, blocks, emit_w16):
    m, d = x2.shape
    w16 = {}
    mm = _named_matmul(w, blocks, emit_w16, w16)

    (h1,) = mm(["out_diff", "out_hgrn"], [cat_diff, cat_hgrn], d, [F32], residual=x2, name="proj_out")

    n2 = _rmsnorm(h1, norms["cross"], BF16)
    (cq,) = mm(["mem_q"], [n2], d, [BF16], scale=(d // MEM_HEADS) ** -0.5, name="proj_mem_q")
    mk, mv = mem_kv_fn()
    co = _cross_attn(cq.reshape(nb, t, d), mk, mv, tq=min(t, 512))
    (h2,) = mm(["mem_o"], [co.reshape(m, d)], d, [F32], residual=h1, name="proj_mem_o")

    n3 = _rmsnorm(h2, norms["ffn"], BF16)
    down_f32 = w.get("down_f32")
    gu = _gateup(n3, w["gate"][0], w["up"][0], bm=FFN_GATE_UP_BLOCK[0], bn=FFN_GATE_UP_BLOCK[1],
                 emit_w16=emit_w16, cast_src=down_f32)
    act = gu[0]
    if emit_w16:
        w16["gate"], w16["up"] = (gu[1], 0, 0), (gu[2], 0, 0)
    if down_f32 is not None:
        w16["down"] = (gu[-1], 0, 0)
    down = w16["down"] if down_f32 is not None else w["down"]
    (h3,) = _matmul([(act,) + down], d, [F32], residual=h2, bm=FFN_DOWN_BLOCK[0], bn=FFN_DOWN_BLOCK[1],
                    name="ffn_down")
    y = _rmsnorm(h3, norms["final"], F32)
    return y.reshape(nb, t, d), w16


def kernel(x_prompt, x_sample, mem_prompt, cache_diff_k, cache_diff_v, state_hgrn, cache_mem_k, cache_mem_v, norm_mix, w_in, diff_lambda, diff_subln, rel_bias, hgrn_lb_logits, hgrn_norm, w_out, norm_cross, norm_mem, w_mem_q, w_mem_k, w_mem_v, w_mem_o, norm_ffn, w_ffn_gate, w_ffn_up, w_ffn_down, norm_final):
    bp, tp, d = x_prompt.shape
    bs, ts, _ = x_sample.shape
    depth = w_in.shape[0]
    assert depth == 1 and bp == 1
    past = cache_diff_k.shape[2]
    n_dh = cache_diff_k.shape[3]
    d_diff = n_dh * HEAD_DIM
    n_hh = state_hgrn.shape[2]
    d_hgrn = n_hh * HEAD_DIM
    assert past % CHUNK == 0 and ts == CHUNK and tp % ATTN_TILE == 0

    norms = {"mix": norm_mix[0], "cross": norm_cross[0], "ffn": norm_ffn[0], "final": norm_final}
    widths = {"diff": d_diff, "hgrn_in": w_in.shape[2] - 3 * d_diff}
    w_f32 = {
        "dq": (w_in[0], 0, 0), "dk": (w_in[0], 0, d_diff), "dv": (w_in[0], 0, 2 * d_diff),
        "h": (w_in[0], 0, 3 * d_diff),
        "out_diff": (w_out[0], 0, 0), "out_hgrn": (w_out[0], d_diff, 0),
        "mem_q": (w_mem_q[0], 0, 0), "mem_o": (w_mem_o[0], 0, 0),
        "gate": (w_ffn_gate[0], 0, 0), "up": (w_ffn_up[0], 0, 0),
        "down_f32": w_ffn_down[0],
    }
    lam_p = diff_lambda[0].astype(F32)
    subln = diff_subln[0].reshape(1, HEAD_DIM)
    hnorm = hgrn_norm[0].reshape(1, HEAD_DIM)

    tile = ATTN_TILE
    r = jnp.arange(tile)[:, None]
    c = jnp.arange(2 * tile)[None, :]
    rel_p = (c - tile) - r
    vis_p = jnp.logical_or(c < tile, (c - tile) // CHUNK <= r // CHUNK)
    bias_p = _bias_tiles(rel_bias, _rel_bucket(rel_p), jnp.where(vis_p, 0.0, MASK_VALUE).astype(F32),
                         _rel_bucket(jnp.int32(-2 * tile)), n_dh)
    qpos = past + jnp.arange(ts)[:, None]
    kpos = (past - SAMPLE_NEAR) + jnp.arange(SAMPLE_NEAR + ts)[None, :]
    vis_s = kpos // CHUNK <= qpos // CHUNK
    bias_s = _bias_tiles(rel_bias, _rel_bucket(kpos - qpos), jnp.where(vis_s, 0.0, MASK_VALUE).astype(F32),
                         _rel_bucket(jnp.int32(-(SAMPLE_NEAR + 1))), n_dh)

    mem_out = {}

    def mem_kv_prompt():
        mn = _rmsnorm(mem_prompt.reshape(bp * N_MEM, d), norm_mem[0], BF16)
        bm, bn = MM_BLOCK_F32_WEIGHTS
        (mk,) = _matmul([(mn, w_mem_k[0], 0, 0)], d, [F32], bm=bm, bn=bn, name="proj_mem_k")
        (mv,) = _matmul([(mn, w_mem_v[0], 0, 0)], d, [F32], bm=bm, bn=bn, name="proj_mem_v")
        mem_out["k"], mem_out["v"] = mk, mv
        return mk.reshape(bp, N_MEM, d), mv.reshape(bp, N_MEM, d)

    def mem_kv_sample():
        return cache_mem_k[0], cache_mem_v[0]

    xs2 = x_sample.reshape(bs * ts, d)
    xp2 = x_prompt.reshape(bp * tp, d)
    _, dq_s, dk_s, dk16_s, dv_s, dv16_s, hpart_s, w16_in = _in_proj(
        xs2, norms, w_f32, widths, blocks=MM_BLOCK_F32_WEIGHTS, emit_w16=True, with_hgrn=True)
    n_p, dq_p, dk_p, dk16_p, dv_p, dv16_p, _, _ = _in_proj(
        xp2, norms, w16_in, widths, blocks=MM_BLOCK_BF16_WEIGHTS, emit_w16=False, with_hgrn=False)

    shp = (bs, ts, d_diff)
    hpart_p, cat_diff_s = _proj_and_attn_sample(
        n_p, w16_in["h"][0], lam_p, subln, dq_s.reshape(shp), dk16_s.reshape(shp), dv16_s.reshape(shp),
        cache_diff_k[0].reshape(bs, past * n_dh, HEAD_DIM), cache_diff_v[0].reshape(bs, past * n_dh, HEAD_DIM),
        bias_s, n_dh)

    cat_hgrn_s, s_s = _hgrn(hgrn_lb_logits, hnorm, hpart_s.reshape(bs, ts, 4 * d_hgrn), state_hgrn[0], n_hh)
    y_s, w16_tail = _tail(xs2, bs, ts, cat_diff_s.reshape(bs * ts, d_diff), cat_hgrn_s.reshape(bs * ts, d_hgrn),
                          norms, w_f32, mem_kv_sample, blocks=MM_BLOCK_F32_WEIGHTS, emit_w16=True)

    w_bf16 = {**w16_in, **w16_tail}
    cat_diff_p = _attn_prompt(lam_p, subln, dq_p, dk16_p, dv16_p, bias_p, n_dh)
    s0 = jnp.zeros((bp, n_hh, HEAD_DIM, HEAD_DIM), F32)
    cat_hgrn_p, s_p = _hgrn(hgrn_lb_logits, hnorm, hpart_p.reshape(bp, tp, 4 * d_hgrn), s0, n_hh)
    y_p, _ = _tail(xp2, bp, tp, cat_diff_p, cat_hgrn_p.reshape(bp * tp, d_hgrn), norms, w_bf16, mem_kv_prompt,
                   blocks=MM_BLOCK_BF16_WEIGHTS, emit_w16=False)

    mhd = d // MEM_HEADS
    return (y_p, y_s,
            dk_p.reshape(depth, bp, tp, n_dh, HEAD_DIM), dv_p.reshape(depth, bp, tp, n_dh, HEAD_DIM),
            s_p.reshape(depth, bp, n_hh, HEAD_DIM, HEAD_DIM),
            mem_out["k"].reshape(depth, bp, N_MEM, MEM_HEADS, mhd),
            mem_out["v"].reshape(depth, bp, N_MEM, MEM_HEADS, mhd),
            dk_s.reshape(depth, bs, ts, n_dh, HEAD_DIM), dv_s.reshape(depth, bs, ts, n_dh, HEAD_DIM),
            s_s.reshape(depth, bs, n_hh, HEAD_DIM, HEAD_DIM))
```

```python
import functools
import math

import jax
import jax.numpy as jnp
from jax import lax
from jax.experimental import pallas as pl
from jax.experimental.pallas import tpu as pltpu

F32 = jnp.float32
BF16 = jnp.bfloat16

CHUNK = 64
HEAD_DIM = 128
DIFF_QK = HEAD_DIM // 2
N_MEM = 256
MEM_HEADS = 4
REL_BUCKETS = 32
REL_MAX_DIST = 128
EPS = 1e-6
LAM_INIT_L0 = 0.8 - 0.6 * math.exp(-0.3 * 0)
MASK_VALUE = -1e30
LOG2E = math.log2(math.e)

V7X_LANES = 128
V7X_SCOPED_VMEM_LIMIT_BYTES = 60000 * 1024
MM_BLOCK_F32_WEIGHTS = (2048, 256)
MM_BLOCK_BF16_WEIGHTS = (1024, 1024)
FFN_GATE_UP_BLOCK = (2048, 256)
FFN_DOWN_BLOCK = (512, 512)

DIAG_SLOT = 2
ATTN_TILE = 512
SAMPLE_CACHE_PARTS = 4
FUSED_PROJ_BLOCK = (1024, 512)
HGRN_CHUNKS_PER_STEP = 4
SAMPLE_NEAR = 128


def _cparams(*sem):
    return pltpu.CompilerParams(dimension_semantics=sem,
                                vmem_limit_bytes=V7X_SCOPED_VMEM_LIMIT_BYTES)


def _rmsnorm_kernel(x_ref, g_ref, o_ref):
    x = x_ref[...]
    ms = jnp.mean(x * x, axis=-1, keepdims=True)
    o_ref[...] = (x * lax.rsqrt(ms + EPS) * g_ref[...]).astype(o_ref.dtype)


def _rmsnorm(x, g, out_dtype, rows=512):
    m, d = x.shape
    rows = min(rows, m)
    return pl.pallas_call(
        _rmsnorm_kernel,
        grid=(m // rows,),
        in_specs=[pl.BlockSpec((rows, d), lambda i: (i, 0)),
                  pl.BlockSpec((1, d), lambda i: (0, 0))],
        out_specs=pl.BlockSpec((rows, d), lambda i: (i, 0)),
        out_shape=jax.ShapeDtypeStruct((m, d), out_dtype),
        compiler_params=_cparams("parallel"),
        name="rmsnorm",
    )(x, g.reshape(1, d))


def _mm_kernel(*refs, n_pairs, has_res, n_out, n_w16, has_norm_rider, scale):
    xs = refs[0:2 * n_pairs:2]
    ws = refs[1:2 * n_pairs:2]
    pos = 2 * n_pairs
    res_ref = refs[pos] if has_res else None
    pos += int(has_res)
    if has_norm_rider:
        _rmsnorm_kernel(refs[pos], refs[pos + 1], refs[-1])
        pos += 2
    outs = refs[pos:pos + n_out]
    w16_outs = refs[pos + n_out:pos + n_out + n_w16]

    total = None
    for p, (x_ref, w_ref) in enumerate(zip(xs, ws)):
        wb = w_ref[...].astype(BF16)
        if w16_outs:
            w16_outs[p][...] = wb
        d = jnp.dot(x_ref[...], wb, preferred_element_type=F32)
        total = d if total is None else total + d
    if scale != 1.0:
        total = total * scale
    if has_res:
        total = total + res_ref[...]
    for o_ref in outs:
        o_ref[...] = total.astype(o_ref.dtype)


def _matmul(pairs, n, out_dtypes, *, residual=None, scale=1.0, bm, bn, emit_w16=False, norm_rider=None,
            name="matmul"):
    m = pairs[0][0].shape[0]
    bm = min(bm, m)
    bn = min(bn, n)
    assert m % bm == 0 and n % bn == 0
    in_specs, args = [], []
    for x, w, r0, c0 in pairs:
        kx = x.shape[1]
        assert r0 % kx == 0 and c0 % bn == 0
        mode = dict(pipeline_mode=pl.Buffered(1)) if norm_rider is not None else {}
        in_specs.append(pl.BlockSpec((bm, kx), lambda j, i: (i, 0), **mode))
        in_specs.append(pl.BlockSpec((kx, bn), lambda j, i, rb=r0 // kx, cb=c0 // bn: (rb, cb + j)))
        args += [x, w]
    if residual is not None:
        in_specs.append(pl.BlockSpec((bm, bn), lambda j, i: (i, j)))
        args.append(residual)
    out_specs = [pl.BlockSpec((bm, bn), lambda j, i: (i, j)) for _ in out_dtypes]
    out_shape = [jax.ShapeDtypeStruct((m, n), dt) for dt in out_dtypes]
    if emit_w16:
        for x, _, _, _ in pairs:
            out_specs.append(pl.BlockSpec((x.shape[1], bn), lambda j, i: (0, j)))
            out_shape.append(jax.ShapeDtypeStruct((x.shape[1], n), BF16))
    if norm_rider is not None:
        y, gain = norm_rider
        steps = n // bn
        rows, dy = y.shape
        assert m == bm and rows % steps == 0 and (rows // steps) % 8 == 0
        in_specs += [pl.BlockSpec((rows // steps, dy), lambda j, i: (j, 0)),
                     pl.BlockSpec((1, dy), lambda j, i: (0, 0))]
        args += [y, gain.reshape(1, dy)]
        out_specs.append(pl.BlockSpec((rows // steps, dy), lambda j, i: (j, 0)))
        out_shape.append(jax.ShapeDtypeStruct((rows, dy), BF16))
    kern = functools.partial(_mm_kernel, n_pairs=len(pairs), has_res=residual is not None,
                             n_out=len(out_dtypes), n_w16=len(pairs) if emit_w16 else 0,
                             has_norm_rider=norm_rider is not None, scale=scale)
    return pl.pallas_call(
        kern,
        grid=(n // bn, m // bm),
        in_specs=in_specs,
        out_specs=out_specs,
        out_shape=out_shape,
        compiler_params=_cparams("parallel", "arbitrary"),
        name=name,
    )(*args)


def _gateup_kernel(*refs, emit_w16, has_cast):
    x_ref, wg_ref, wu_ref = refs[:3]
    pos = 3 + int(has_cast)
    o_ref = refs[pos]
    x = x_ref[...]
    wg = wg_ref[...].astype(BF16)
    wu = wu_ref[...].astype(BF16)
    if emit_w16:
        refs[pos + 1][...] = wg
        refs[pos + 2][...] = wu
    if has_cast:
        refs[-1][...] = refs[3][...].astype(BF16)
    g = jnp.dot(x, wg, preferred_element_type=F32)
    u = jnp.dot(x, wu, preferred_element_type=F32)
    o_ref[...] = (_silu(g) * u).astype(o_ref.dtype)


def _gateup(x, wg, wu, *, bm, bn, emit_w16=False, cast_src=None):
    m, kdim = x.shape
    n = wg.shape[1]
    bm = min(bm, m)
    assert m % bm == 0 and n % bn == 0
    steps = n // bn
    w_spec = pl.BlockSpec((kdim, bn), lambda j, i: (0, j))
    x_spec = pl.BlockSpec((bm, kdim), lambda j, i: (i, 0))
    in_specs, args = [x_spec, w_spec, w_spec], [x, wg, wu]
    out_specs = [pl.BlockSpec((bm, bn), lambda j, i: (i, j))]
    out_shape = [jax.ShapeDtypeStruct((m, n), BF16)]
    if emit_w16:
        out_specs += [w_spec, w_spec]
        out_shape += [jax.ShapeDtypeStruct((kdim, n), BF16)] * 2
    if cast_src is not None:
        rows, width = cast_src.shape
        assert m == bm and rows % steps == 0 and (rows // steps) % 8 == 0
        in_specs[0] = pl.BlockSpec((bm, kdim), lambda j, i: (i, 0), pipeline_mode=pl.Buffered(1))
        cast_spec = pl.BlockSpec((rows // steps, width), lambda j, i: (j, 0))
        in_specs.append(cast_spec)
        args.append(cast_src)
        out_specs.append(cast_spec)
        out_shape.append(jax.ShapeDtypeStruct((rows, width), BF16))
    kern = functools.partial(_gateup_kernel, emit_w16=emit_w16, has_cast=cast_src is not None)
    return pl.pallas_call(
        kern,
        grid=(steps, m // bm),
        in_specs=in_specs,
        out_specs=out_specs,
        out_shape=out_shape,
        compiler_params=_cparams("parallel", "arbitrary"),
        name="ffn_gate_up",
    )(*args)


def _rel_bucket(rel):
    nb = REL_BUCKETS // 2
    max_exact = nb // 2
    ret = jnp.where(rel > 0, nb, 0)
    n = jnp.abs(rel)
    nf = jnp.maximum(n, 1).astype(F32)
    large = max_exact + (jnp.log(nf / max_exact) / math.log(REL_MAX_DIST / max_exact)
                         * (nb - max_exact)).astype(jnp.int32)
    large = jnp.minimum(large, nb - 1)
    return ret + jnp.where(n < max_exact, n, large)


def _bias_tile_kernel(far_ref, table_ref, bucket_ref, mask_ref, o_ref):
    h = pl.program_id(0)
    bucket = bucket_ref[...]
    acc = jnp.zeros(bucket.shape, F32)
    for b in range(REL_BUCKETS):
        acc = jnp.where(bucket == b, table_ref[b, h], acc)
    o_ref[0] = (acc - table_ref[far_ref[0], h]) * LOG2E + mask_ref[...]


def _bias_tiles(table, bucket, maskadd, far_bucket, n_heads):
    r, c = bucket.shape
    return pl.pallas_call(
        _bias_tile_kernel,
        grid=(n_heads,),
        in_specs=[pl.BlockSpec(memory_space=pltpu.SMEM),
                  pl.BlockSpec(memory_space=pltpu.SMEM),
                  pl.BlockSpec((r, c), lambda h: (0, 0)),
                  pl.BlockSpec((r, c), lambda h: (0, 0))],
        out_specs=pl.BlockSpec((1, r, c), lambda h: (h, 0, 0)),
        out_shape=jax.ShapeDtypeStruct((n_heads, r, c), F32),
        compiler_params=_cparams("arbitrary"),
        name="rel_bias_tiles",
    )(far_bucket.reshape(1).astype(jnp.int32), table.astype(F32), bucket.astype(jnp.int32), maskadd)


def _diff_lambda(lp):
    a = jnp.sum(lp[0:1, :] * lp[1:2, :], axis=-1, keepdims=True)
    b = jnp.sum(lp[2:3, :] * lp[3:4, :], axis=-1, keepdims=True)
    return jnp.exp(a) - jnp.exp(b) + LAM_INIT_L0


def _stack_maps(q):
    lane = lax.broadcasted_iota(jnp.int32, q.shape, 1)
    zero = jnp.zeros_like(q)
    return jnp.concatenate([jnp.where(lane < DIFF_QK, q, zero),
                            jnp.where(lane >= DIFF_QK, q, zero)], axis=0)


def _diff_epilogue(acc, l, lam, g, tq):
    o = acc * (1.0 / l)
    o = o[:tq] - lam * o[tq:]
    ms = jnp.mean(o * o, axis=-1, keepdims=True)
    return o * lax.rsqrt(ms + EPS) * g * (1.0 - LAM_INIT_L0)


def _attn_prompt_kernel(lam_ref, g_ref, q_ref, qn_ref, k_ref, v_ref, bias_ref, o_ref, qs_ref, s_ref, m_ref,
                        acc_ref, *, tile):
    i = pl.program_id(1)
    n_lane_tiles = tile // V7X_LANES
    ones = jnp.ones((tile, V7X_LANES), BF16)
    nt = (((1,), (1,)), ((), ()))

    qs_ref[...] = _stack_maps(q_ref[...])
    m_ref[...] = jnp.full(m_ref.shape, MASK_VALUE, F32)
    acc_ref[...] = jnp.zeros(acc_ref.shape, F32)

    def key_rows(d):
        return pl.ds(pl.multiple_of((i - d) * tile, tile), tile)

    def scores(d, slot):
        s_ref[slot] = lax.dot_general(qs_ref[...], k_ref[key_rows(d), :], nt, preferred_element_type=F32)

    def absorb(d, slot, bias):
        vt = jnp.concatenate([v_ref[key_rows(d), :], ones], axis=1)
        s = s_ref[slot]
        if bias is not None:
            s = (s.reshape(2, tile, tile) + bias[None]).reshape(2 * tile, tile)
        cols = [s[:, c * V7X_LANES:(c + 1) * V7X_LANES] for c in range(n_lane_tiles)]
        mx = cols[0]
        for sc in cols[1:]:
            mx = jnp.maximum(mx, sc)
        m_prev = m_ref[...]
        m_new = jnp.maximum(m_prev, jnp.max(mx, axis=-1, keepdims=True))
        alpha = jnp.exp2(m_prev - m_new)
        p = jnp.concatenate([jnp.exp2(sc - m_new).astype(BF16) for sc in cols], axis=1)
        pv = jnp.dot(p, vt, preferred_element_type=F32)
        acc_ref[...] = jnp.concatenate([alpha, alpha], axis=1) * acc_ref[...] + pv
        m_ref[...] = m_new

    i_even = lax.rem(i, 2) == 0

    @pl.when(i == 0)
    def _():
        scores(0, DIAG_SLOT)

    @pl.when(jnp.logical_and(i_even, i >= 2))
    def _():
        scores(i - 1, 1)
        absorb(i, 0, None)

    d_odd = i - 1 + lax.rem(i, 2)

    def pair_body(u, carry):
        d = d_odd - 2 * u
        scores(d - 1, 0)
        absorb(d, 1, None)
        scores(d - 2, 1)
        absorb(d - 1, 0, None)
        return carry

    lax.fori_loop(0, jnp.where(d_odd >= 3, (d_odd - 1) // 2, 0), pair_body, 0)

    @pl.when(i >= 1)
    def _():
        scores(0, DIAG_SLOT)
        absorb(1, 1, bias_ref[0, :, :tile])

    def diagonal(next_slot):
        if next_slot is not None:
            s_ref[next_slot] = lax.dot_general(_stack_maps(qn_ref[...]), k_ref[pl.ds(0, tile), :], nt,
                                               preferred_element_type=F32)
        absorb(0, DIAG_SLOT, bias_ref[0, :, tile:])

    has_next = i + 1 < pl.num_programs(1)

    @pl.when(jnp.logical_and(has_next, jnp.logical_not(i_even)))
    def _():
        diagonal(0)

    @pl.when(jnp.logical_and(has_next, i_even))
    def _():
        diagonal(1)

    @pl.when(jnp.logical_not(has_next))
    def _():
        diagonal(None)

    acc = acc_ref[...]
    y = _diff_epilogue(acc[:, :HEAD_DIM], acc[:, HEAD_DIM:], _diff_lambda(lam_ref[...]), g_ref[...], tile)
    o_ref[...] = y.astype(o_ref.dtype)


def _attn_prompt(lam_p, g, q, k, v, bias, n_heads, tile=ATTN_TILE):
    t = q.shape[0]
    nq = t // tile
    kern = functools.partial(_attn_prompt_kernel, tile=tile)
    return pl.pallas_call(
        kern,
        grid=(n_heads, nq),
        in_specs=[pl.BlockSpec((4, DIFF_QK), lambda h, i: (0, 0)),
                  pl.BlockSpec((1, HEAD_DIM), lambda h, i: (0, 0)),
                  pl.BlockSpec((tile, HEAD_DIM), lambda h, i: (i, h)),
                  pl.BlockSpec((tile, HEAD_DIM), lambda h, i: (jnp.minimum(i + 1, nq - 1), h)),
                  pl.BlockSpec((t, HEAD_DIM), lambda h, i: (0, h)),
                  pl.BlockSpec((t, HEAD_DIM), lambda h, i: (0, h)),
                  pl.BlockSpec((1, tile, 2 * tile), lambda h, i: (h, 0, 0))],
        out_specs=pl.BlockSpec((tile, HEAD_DIM), lambda h, i: (i, h)),
        out_shape=jax.ShapeDtypeStruct((t, n_heads * HEAD_DIM), BF16),
        scratch_shapes=[pltpu.VMEM((2 * tile, HEAD_DIM), BF16),
                        pltpu.VMEM((DIAG_SLOT + 1, 2 * tile, tile), F32),
                        pltpu.VMEM((2 * tile, V7X_LANES), F32),
                        pltpu.VMEM((2 * tile, 2 * HEAD_DIM), F32)],
        compiler_params=_cparams("parallel", "arbitrary"),
        name="diff_attn_prompt",
    )(lam_p, g, q, q, k, v, bias)


def _attn_sample_step(part, n_parts, lam_ref, g_ref, q_ref, kn_ref, vn_ref, kc_ref, vc_ref, bias_ref, o_ref,
                      m_ref, acc_ref, *, n_heads, keys, ts):
    is_last = part == n_parts - 1
    n_lane_tiles = keys // V7X_LANES
    ones_c = jnp.ones((keys, V7X_LANES), BF16)
    ones_n = jnp.ones((ts, V7X_LANES), BF16)
    nt = (((1,), (1,)), ((), ()))

    @pl.when(part == 0)
    def _():
        m_ref[...] = jnp.full(m_ref.shape, MASK_VALUE, F32)
        acc_ref[...] = jnp.zeros(acc_ref.shape, F32)

    for h in range(n_heads):
        col = slice(h * HEAD_DIM, (h + 1) * HEAD_DIM)
        rows = pl.ds(h, keys, stride=n_heads)
        qs = _stack_maps(q_ref[0, :, col])
        kc = kc_ref[0, rows, :].astype(BF16)
        vc = jnp.concatenate([vc_ref[0, rows, :].astype(BF16), ones_c], axis=1)
        vn = jnp.concatenate([vn_ref[0, :, col], ones_n], axis=1)
        bias = bias_ref[h]
        near_bias = jnp.where(is_last, bias[:, :SAMPLE_NEAR], 0.0)
        new_bias = jnp.where(is_last, bias[:, SAMPLE_NEAR:], MASK_VALUE)

        s = lax.dot_general(qs, kc, nt, preferred_element_type=F32)
        cols = [s[:, c * V7X_LANES:(c + 1) * V7X_LANES] for c in range(n_lane_tiles)]
        cols[-1] = (cols[-1].reshape(2, ts, SAMPLE_NEAR) + near_bias[None]).reshape(2 * ts, SAMPLE_NEAR)
        s_new = lax.dot_general(qs, kn_ref[0, :, col], nt, preferred_element_type=F32)
        s_new = (s_new.reshape(2, ts, ts) + new_bias[None]).reshape(2 * ts, ts)

        mx = cols[0]
        for sc in cols[1:]:
            mx = jnp.maximum(mx, sc)
        m_prev = m_ref[h]
        m_new = jnp.maximum(m_prev, jnp.maximum(jnp.max(mx, axis=-1, keepdims=True),
                                                jnp.max(s_new, axis=-1, keepdims=True)))
        alpha = jnp.exp2(m_prev - m_new)
        p = jnp.concatenate([jnp.exp2(sc - m_new).astype(BF16) for sc in cols], axis=1)
        p_new = jnp.exp2(s_new - m_new[:, :ts]).astype(BF16)
        pv = (jnp.dot(p, vc, preferred_element_type=F32)
              + jnp.dot(p_new, vn, preferred_element_type=F32))
        acc_ref[h] = jnp.concatenate([alpha, alpha], axis=1) * acc_ref[h] + pv
        m_ref[h] = m_new

    @pl.when(is_last)
    def _():
        lam = _diff_lambda(lam_ref[...])
        g = g_ref[...]
        for h in range(n_heads):
            acc = acc_ref[h]
            y = _diff_epilogue(acc[:, :HEAD_DIM], acc[:, HEAD_DIM:], lam, g, ts)
            o_ref[0, :, h * HEAD_DIM:(h + 1) * HEAD_DIM] = y.astype(o_ref.dtype)


def _proj_and_attn_sample_kernel(x_ref, w_ref, lam_ref, g_ref, q_ref, kn_ref, vn_ref, kc_ref, vc_ref, bias_ref,
                                 y_ref, o_ref, m_ref, acc_ref, *, parts, n_heads, keys, ts):
    y_ref[...] = jnp.dot(x_ref[...], w_ref[...], preferred_element_type=F32)
    part = lax.rem(pl.program_id(0), parts)
    _attn_sample_step(part, parts, lam_ref, g_ref, q_ref, kn_ref, vn_ref, kc_ref, vc_ref, bias_ref, o_ref,
                      m_ref, acc_ref, n_heads=n_heads, keys=keys, ts=ts)


def _proj_and_attn_sample(x, w16, lam_p, g, q, kn, vn, kc, vc, bias, n_heads):
    m, kdim = x.shape
    n = w16.shape[1]
    bm, bn = FUSED_PROJ_BLOCK
    nm, nn = m // bm, n // bn
    nb, ts, d = q.shape
    past = kc.shape[1] // n_heads
    parts = SAMPLE_CACHE_PARTS
    keys = past // parts
    assert m % bm == 0 and n % bn == 0 and nm * nn == nb * parts
    assert past % parts == 0 and keys % V7X_LANES == 0 and SAMPLE_NEAR == V7X_LANES
    kern = functools.partial(_proj_and_attn_sample_kernel, parts=parts, n_heads=n_heads, keys=keys, ts=ts)
    new = pl.BlockSpec((1, ts, d), lambda s: (s // parts, 0, 0))
    cache = pl.BlockSpec((1, keys * n_heads, HEAD_DIM), lambda s: (s // parts, s % parts, 0))
    return pl.pallas_call(
        kern,
        grid=(nb * parts,),
        in_specs=[pl.BlockSpec((bm, kdim), lambda s: (s // nn, 0)),
                  pl.BlockSpec((kdim, bn), lambda s: (0, s % nn)),
                  pl.BlockSpec((4, DIFF_QK), lambda s: (0, 0)),
                  pl.BlockSpec((1, HEAD_DIM), lambda s: (0, 0)),
                  new, new, new, cache, cache,
                  pl.BlockSpec((n_heads, ts, SAMPLE_NEAR + ts), lambda s: (0, 0, 0))],
        out_specs=[pl.BlockSpec((bm, bn), lambda s: (s // nn, s % nn)), new],
        out_shape=[jax.ShapeDtypeStruct((m, n), F32),
                   jax.ShapeDtypeStruct((nb, ts, d), BF16)],
        scratch_shapes=[pltpu.VMEM((n_heads, 2 * ts, V7X_LANES), F32),
                        pltpu.VMEM((n_heads, 2 * ts, 2 * HEAD_DIM), F32)],
        compiler_params=_cparams("arbitrary"),
        name="proj_hgrn_prompt_and_diff_attn_sample",
    )(x, w16, lam_p, g, q, kn, vn, kc, vc, bias)


def _split3(x):
    hi = x.astype(BF16)
    r1 = x - hi.astype(F32)
    mid = r1.astype(BF16)
    lo = (r1 - mid.astype(F32)).astype(BF16)
    return hi, mid, lo


def _silu(x):
    h = 0.5 * x
    return h + h * jnp.tanh(h)


def _hgrn_kernel(lbl_ref, g_ref, hq_ref, hf_ref, hi_ref, hg_ref, s0_ref, o_ref, s_out_ref, st_ref, *,
                 n_heads, chain):
    c = pl.program_id(1)
    nc = pl.num_programs(1)
    n_streams = st_ref.shape[0]
    n_chunks = HGRN_CHUNKS_PER_STEP
    rows = n_chunks * CHUNK
    d = n_heads * HEAD_DIM

    @pl.when(c == 0)
    def _():
        for s in range(n_streams):
            for h in range(n_heads):
                st_ref[s, h] = s0_ref[s, h].T

    lg = lbl_ref[...]
    e = jnp.exp(lg - jnp.max(lg, axis=0, keepdims=True))
    lb = e[0:1, :] / jnp.sum(e, axis=0, keepdims=True)

    c_half = 0.5 * (1.0 - lb)
    t = jnp.tanh(0.5 * hf_ref[...].reshape(rows, d))
    ct = c_half * t
    logf = jnp.log2((lb + c_half) + ct)
    kk = c_half - ct
    qq = _silu(hq_ref[...].reshape(rows, d))

    row = lax.broadcasted_iota(jnp.int32, (rows, rows), 0)
    col = lax.broadcasted_iota(jnp.int32, (rows, rows), 1)
    causal = jnp.logical_and(col <= row, col // CHUNK == row // CHUNK)
    tri = jnp.where(causal, 1.0, 0.0).astype(BF16)
    b = None
    for part in _split3(logf):
        term = jnp.dot(tri, part, preferred_element_type=F32)
        b = term if b is None else b + term
    qd = (qq * jnp.exp2(b)).astype(BF16)
    kd32 = kk * jnp.exp2(-b)
    kd = kd32.astype(BF16)
    decays = [jnp.exp2(b[(r + 1) * CHUNK - 1:(r + 1) * CHUNK, :]) for r in range(n_chunks)]
    k_last = jnp.concatenate([kd32[r * CHUNK:(r + 1) * CHUNK] * decays[r] for r in range(n_chunks)],
                             axis=0).astype(BF16)
    v = hi_ref[...].reshape(rows, d).astype(BF16)
    gate = _silu(hg_ref[...].reshape(rows, d))
    g = g_ref[...]

    nt = (((1,), (1,)), ((), ()))
    tn = (((0,), (0,)), ((), ()))
    for h in range(n_heads):
        cols = slice(h * HEAD_DIM, (h + 1) * HEAD_DIM)
        a = lax.dot_general(qd[:, cols], kd[:, cols], nt, preferred_element_type=F32)
        a = jnp.where(causal, a, 0.0)
        o_intra = jnp.dot(a.astype(BF16), v[:, cols], preferred_element_type=F32)
        o_parts = []
        st = st_ref[0, h]
        for r in range(n_chunks):
            rr = slice(r * CHUNK, (r + 1) * CHUNK)
            if not chain:
                st = st_ref[r, h]
            o_parts.append(o_intra[rr] + lax.dot_general(qd[rr, cols], st.astype(BF16), nt,
                                                         preferred_element_type=F32))
            st = decays[r][:, cols] * st + lax.dot_general(v[rr, cols], k_last[rr, cols], tn,
                                                           preferred_element_type=F32)
            if not chain:
                st_ref[r, h] = st
        if chain:
            st_ref[0, h] = st
        o = jnp.concatenate(o_parts, axis=0)
        ms = jnp.mean(o * o, axis=-1, keepdims=True)
        y = (o * lax.rsqrt(ms + EPS) * g * gate[:, cols]).astype(o_ref.dtype)
        if chain:
            o_ref[0, :, cols] = y
        else:
            for r in range(n_chunks):
                o_ref[r, :, cols] = y[r * CHUNK:(r + 1) * CHUNK]

    @pl.when(c == nc - 1)
    def _():
        for s in range(n_streams):
            for h in range(n_heads):
                s_out_ref[s, h] = st_ref[s, h].T


def _hgrn(lb_logits, g, hpart, s0, n_heads):
    nb, t, _ = hpart.shape
    d = n_heads * HEAD_DIM
    per = HGRN_CHUNKS_PER_STEP
    chain = t > CHUNK
    sb, tb = (1, per * CHUNK) if chain else (per, CHUNK)
    assert nb % sb == 0 and t % tb == 0
    kern = functools.partial(_hgrn_kernel, n_heads=n_heads, chain=chain)
    part = lambda p: pl.BlockSpec((sb, tb, d), lambda b, c: (b, c, p))
    state = pl.BlockSpec((sb, n_heads, HEAD_DIM, HEAD_DIM), lambda b, c: (b, 0, 0, 0))
    return pl.pallas_call(
        kern,
        grid=(nb // sb, t // tb),
        in_specs=[pl.BlockSpec(lb_logits.shape, lambda b, c: (0, 0)),
                  pl.BlockSpec((1, HEAD_DIM), lambda b, c: (0, 0)),
                  part(0), part(1), part(2), part(3), state],
        out_specs=[pl.BlockSpec((sb, tb, d), lambda b, c: (b, c, 0)), state],
        out_shape=[jax.ShapeDtypeStruct((nb, t, d), BF16),
                   jax.ShapeDtypeStruct(s0.shape, F32)],
        scratch_shapes=[pltpu.VMEM((sb, n_heads, HEAD_DIM, HEAD_DIM), F32)],
        compiler_params=_cparams("parallel", "arbitrary"),
        name="hgrn2_chunks",
    )(lb_logits, g, hpart, hpart, hpart, hpart, s0)


def _cross_kernel(q_ref, k_ref, v_ref, o_ref, *, heads, hd, split_heads):
    nt = (((1,), (1,)), ((), ()))
    if split_heads:
        tq = q_ref.shape[1]
        n_mem = k_ref.shape[1]
        xk = k_ref[0].reshape(n_mem * heads, hd).astype(BF16)
        xv = v_ref[0].reshape(n_mem * heads, hd).astype(BF16)
        q = jnp.concatenate([q_ref[0, :, h * hd:(h + 1) * hd] for h in range(heads)], axis=0)
        s = lax.dot_general(q, xk, nt, preferred_element_type=F32)
        row_head = lax.broadcasted_iota(jnp.int32, s.shape, 0) // tq
        col_head = lax.rem(lax.broadcasted_iota(jnp.int32, s.shape, 1), heads)
        s = jnp.where(row_head == col_head, s, MASK_VALUE)
        p = jnp.exp(s - jnp.max(s, axis=-1, keepdims=True))
        l = jnp.sum(p, axis=-1, keepdims=True)
        o = jnp.dot(p.astype(BF16), xv, preferred_element_type=F32) * (1.0 / l)
        for h in range(heads):
            o_ref[0, :, h * hd:(h + 1) * hd] = o[h * tq:(h + 1) * tq].astype(o_ref.dtype)
        return
    for h in range(heads):
        cols = slice(h * hd, (h + 1) * hd)
        q = q_ref[0, :, cols]
        k = k_ref[0, :, cols].astype(BF16)
        v = v_ref[0, :, cols].astype(BF16)
        s = lax.dot_general(q, k, nt, preferred_element_type=F32)
        p = jnp.exp(s - jnp.max(s, axis=-1, keepdims=True))
        l = jnp.sum(p, axis=-1, keepdims=True)
        o = jnp.dot(p.astype(BF16), v, preferred_element_type=F32) * (1.0 / l)
        o_ref[0, :, cols] = o.astype(o_ref.dtype)


def _cross_attn(q, mk, mv, tq):
    nb, t, d = q.shape
    split_heads = mk.ndim == 4
    kern = functools.partial(_cross_kernel, heads=MEM_HEADS, hd=d // MEM_HEADS, split_heads=split_heads)
    if split_heads:
        mem_spec = pl.BlockSpec((1,) + mk.shape[1:], lambda b, i: (b, 0, 0, 0))
    else:
        mem_spec = pl.BlockSpec((1, N_MEM, d), lambda b, i: (b, 0, 0))
    return pl.pallas_call(
        kern,
        grid=(nb, t // tq),
        in_specs=[pl.BlockSpec((1, tq, d), lambda b, i: (b, i, 0)),
                  mem_spec,
                  mem_spec],
        out_specs=pl.BlockSpec((1, tq, d), lambda b, i: (b, i, 0)),
        out_shape=jax.ShapeDtypeStruct((nb, t, d), BF16),
        compiler_params=_cparams("parallel", "parallel"),
        name="mem_cross_attn",
    )(q, mk, mv)


def _named_matmul(w, blocks, emit_w16, w16):
    bm, bn = blocks

    def mm(names, xs, n, out_dtypes, **kw):
        pairs = [(xi,) + w[nm] for nm, xi in zip(names, xs)]
        outs = _matmul(pairs, n, out_dtypes, bm=bm, bn=bn, emit_w16=emit_w16, **kw)
        n_w16 = len(names) if emit_w16 else 0
        for nm, arr in zip(names[:n_w16], outs[len(out_dtypes):]):
            w16[nm] = (arr, 0, 0)
        return tuple(outs[:len(out_dtypes)]) + tuple(outs[len(out_dtypes) + n_w16:])

    return mm


def _in_proj(x2, norms, w, widths, *, blocks, emit_w16, with_hgrn, normed=None, other_x2=None):
    w16 = {}
    mm = _named_matmul(w, blocks, emit_w16, w16)
    n = _rmsnorm(x2, norms["mix"], BF16) if normed is None else normed
    (dq,) = mm(["dq"], [n], widths["diff"], [BF16], scale=DIFF_QK ** -0.5 * LOG2E, name="proj_dq")
    dk32, dk16 = mm(["dk"], [n], widths["diff"], [F32, BF16], name="proj_dk")
    dv32, dv16 = mm(["dv"], [n], widths["diff"], [F32, BF16], name="proj_dv")
    hpart, other_n = None, None
    if with_hgrn:
        rider = None if other_x2 is None else (other_x2, norms["mix"])
        hpart, *rest = mm(["h"], [n], widths["hgrn_in"], [F32], norm_rider=rider, name="proj_hgrn")
        other_n = rest[0] if rest else None
    return n, dq, dk32, dk16, dv32, dv16, hpart, w16, other_n


def _tail(x2, nb, t, cat_diff, cat_hgrn, norms, w, mem_kv_fn, *, blocks, emit_w16):
    m, d = x2.shape
    w16 = {}
    mm = _named_matmul(w, blocks, emit_w16, w16)

    (h1,) = mm(["out_diff", "out_hgrn"], [cat_diff, cat_hgrn], d, [F32], residual=x2, name="proj_out")

    n2 = _rmsnorm(h1, norms["cross"], BF16)
    (cq,) = mm(["mem_q"], [n2], d, [BF16], scale=(d // MEM_HEADS) ** -0.5, name="proj_mem_q")
    mk, mv = mem_kv_fn()
    co = _cross_attn(cq.reshape(nb, t, d), mk, mv, tq=min(t, 512))
    (h2,) = mm(["mem_o"], [co.reshape(m, d)], d, [F32], residual=h1, name="proj_mem_o")

    n3 = _rmsnorm(h2, norms["ffn"], BF16)
    down_f32 = w.get("down_f32")
    gu = _gateup(n3, w["gate"][0], w["up"][0], bm=FFN_GATE_UP_BLOCK[0], bn=FFN_GATE_UP_BLOCK[1],
                 emit_w16=emit_w16, cast_src=down_f32)
    act = gu[0]
    if emit_w16:
        w16["gate"], w16["up"] = (gu[1], 0, 0), (gu[2], 0, 0)
    if down_f32 is not None:
        w16["down"] = (gu[-1], 0, 0)
    down = w16["down"] if down_f32 is not None else w["down"]
    (h3,) = _matmul([(act,) + down], d, [F32], residual=h2, bm=FFN_DOWN_BLOCK[0], bn=FFN_DOWN_BLOCK[1],
                    name="ffn_down")
    y = _rmsnorm(h3, norms["final"], F32)
    return y.reshape(nb, t, d), w16


def kernel(x_prompt, x_sample, mem_prompt, cache_diff_k, cache_diff_v, state_hgrn, cache_mem_k, cache_mem_v, norm_mix, w_in, diff_lambda, diff_subln, rel_bias, hgrn_lb_logits, hgrn_norm, w_out, norm_cross, norm_mem, w_mem_q, w_mem_k, w_mem_v, w_mem_o, norm_ffn, w_ffn_gate, w_ffn_up, w_ffn_down, norm_final):
    bp, tp, d = x_prompt.shape
    bs, ts, _ = x_sample.shape
    depth = w_in.shape[0]
    assert depth == 1 and bp == 1
    past = cache_diff_k.shape[2]
    n_dh = cache_diff_k.shape[3]
    d_diff = n_dh * HEAD_DIM
    n_hh = state_hgrn.shape[2]
    d_hgrn = n_hh * HEAD_DIM
    assert past % CHUNK == 0 and ts == CHUNK and tp % ATTN_TILE == 0

    norms = {"mix": norm_mix[0], "cross": norm_cross[0], "ffn": norm_ffn[0], "final": norm_final}
    widths = {"diff": d_diff, "hgrn_in": w_in.shape[2] - 3 * d_diff}
    w_f32 = {
        "dq": (w_in[0], 0, 0), "dk": (w_in[0], 0, d_diff), "dv": (w_in[0], 0, 2 * d_diff),
        "h": (w_in[0], 0, 3 * d_diff),
        "out_diff": (w_out[0], 0, 0), "out_hgrn": (w_out[0], d_diff, 0),
        "mem_q": (w_mem_q[0], 0, 0), "mem_o": (w_mem_o[0], 0, 0),
        "gate": (w_ffn_gate[0], 0, 0), "up": (w_ffn_up[0], 0, 0),
        "down_f32": w_ffn_down[0],
    }
    lam_p = diff_lambda[0].astype(F32)
    subln = diff_subln[0].reshape(1, HEAD_DIM)
    hnorm = hgrn_norm[0].reshape(1, HEAD_DIM)

    tile = ATTN_TILE
    r = jnp.arange(tile)[:, None]
    c = jnp.arange(2 * tile)[None, :]
    rel_p = (c - tile) - r
    vis_p = jnp.logical_or(c < tile, (c - tile) // CHUNK <= r // CHUNK)
    bias_p = _bias_tiles(rel_bias, _rel_bucket(rel_p), jnp.where(vis_p, 0.0, MASK_VALUE).astype(F32),
                         _rel_bucket(jnp.int32(-2 * tile)), n_dh)
    qpos = past + jnp.arange(ts)[:, None]
    kpos = (past - SAMPLE_NEAR) + jnp.arange(SAMPLE_NEAR + ts)[None, :]
    vis_s = kpos // CHUNK <= qpos // CHUNK
    bias_s = _bias_tiles(rel_bias, _rel_bucket(kpos - qpos), jnp.where(vis_s, 0.0, MASK_VALUE).astype(F32),
                         _rel_bucket(jnp.int32(-(SAMPLE_NEAR + 1))), n_dh)

    mem_out = {}

    def mem_kv_prompt():
        mn = _rmsnorm(mem_prompt.reshape(bp * N_MEM, d), norm_mem[0], BF16)
        bm, bn = MM_BLOCK_F32_WEIGHTS
        (mk,) = _matmul([(mn, w_mem_k[0], 0, 0)], d, [F32], bm=bm, bn=bn, name="proj_mem_k")
        (mv,) = _matmul([(mn, w_mem_v[0], 0, 0)], d, [F32], bm=bm, bn=bn, name="proj_mem_v")
        mem_out["k"], mem_out["v"] = mk, mv
        return mk.reshape(bp, N_MEM, d), mv.reshape(bp, N_MEM, d)

    def mem_kv_sample():
        return cache_mem_k[0], cache_mem_v[0]

    xs2 = x_sample.reshape(bs * ts, d)
    xp2 = x_prompt.reshape(bp * tp, d)
    _, dq_s, dk_s, dk16_s, dv_s, dv16_s, hpart_s, w16_in, n_p = _in_proj(
        xs2, norms, w_f32, widths, blocks=MM_BLOCK_F32_WEIGHTS, emit_w16=True, with_hgrn=True, other_x2=xp2)
    _, dq_p, dk_p, dk16_p, dv_p, dv16_p, _, _, _ = _in_proj(
        xp2, norms, w16_in, widths, blocks=MM_BLOCK_BF16_WEIGHTS, emit_w16=False, with_hgrn=False, normed=n_p)

    shp = (bs, ts, d_diff)
    hpart_p, cat_diff_s = _proj_and_attn_sample(
        n_p, w16_in["h"][0], lam_p, subln, dq_s.reshape(shp), dk16_s.reshape(shp), dv16_s.reshape(shp),
        cache_diff_k[0].reshape(bs, past * n_dh, HEAD_DIM), cache_diff_v[0].reshape(bs, past * n_dh, HEAD_DIM),
        bias_s, n_dh)

    cat_hgrn_s, s_s = _hgrn(hgrn_lb_logits, hnorm, hpart_s.reshape(bs, ts, 4 * d_hgrn), state_hgrn[0], n_hh)
    y_s, w16_tail = _tail(xs2, bs, ts, cat_diff_s.reshape(bs * ts, d_diff), cat_hgrn_s.reshape(bs * ts, d_hgrn),
                          norms, w_f32, mem_kv_sample, blocks=MM_BLOCK_F32_WEIGHTS, emit_w16=True)

    w_bf16 = {**w16_in, **w16_tail}
    cat_diff_p = _attn_prompt(lam_p, subln, dq_p, dk16_p, dv16_p, bias_p, n_dh)
    s0 = jnp.zeros((bp, n_hh, HEAD_DIM, HEAD_DIM), F32)
    cat_hgrn_p, s_p = _hgrn(hgrn_lb_logits, hnorm, hpart_p.reshape(bp, tp, 4 * d_hgrn), s0, n_hh)
    y_p, _ = _tail(xp2, bp, tp, cat_diff_p, cat_hgrn_p.reshape(bp * tp, d_hgrn), norms, w_bf16, mem_kv_prompt,
                   blocks=MM_BLOCK_BF16_WEIGHTS, emit_w16=False)

    mhd = d // MEM_HEADS
    return (y_p, y_s,
            dk_p.reshape(depth, bp, tp, n_dh, HEAD_DIM), dv_p.reshape(depth, bp, tp, n_dh, HEAD_DIM),
            s_p.reshape(depth, bp, n_hh, HEAD_DIM, HEAD_DIM),
            mem_out["k"].reshape(depth, bp, N_MEM, MEM_HEADS, mhd),
            mem_out["v"].reshape(depth, bp, N_MEM, MEM_HEADS, mhd),
            dk_s.reshape(depth, bs, ts, n_dh, HEAD_DIM), dv_s.reshape(depth, bs, ts, n_dh, HEAD_DIM),
            s_s.reshape(depth, bs, n_hh, HEAD_DIM, HEAD_DIM))
```

```python
import functools
import math

import jax
import jax.numpy as jnp
from jax import lax
from jax.experimental import pallas as pl
from jax.experimental.pallas import tpu as pltpu

F32 = jnp.float32
BF16 = jnp.bfloat16

CHUNK = 64
HEAD_DIM = 128
DIFF_QK = HEAD_DIM // 2
N_MEM = 256
MEM_HEADS = 4
REL_BUCKETS = 32
REL_MAX_DIST = 128
EPS = 1e-6
LAM_INIT_L0 = 0.8 - 0.6 * math.exp(-0.3 * 0)
MASK_VALUE = -1e30
LOG2E = math.log2(math.e)

V7X_LANES = 128
V7X_SCOPED_VMEM_LIMIT_BYTES = 60000 * 1024
MM_BLOCK_F32_WEIGHTS = (2048, 256)
MM_BLOCK_F32_WEIGHTS_WIDE = (2048, 512)
MM_BLOCK_BF16_WEIGHTS = (1024, 1024)
FFN_GATE_UP_BLOCK = (2048, 256)
FFN_DOWN_BLOCK = (512, 512)

DIAG_SLOT = 2
ATTN_TILE = 512
SAMPLE_CACHE_PARTS = 4
FUSED_PROJ_BLOCK = (1024, 512)
HGRN_CHUNKS_PER_STEP = 4
SAMPLE_NEAR = 128


def _cparams(*sem):
    return pltpu.CompilerParams(dimension_semantics=sem,
                                vmem_limit_bytes=V7X_SCOPED_VMEM_LIMIT_BYTES)


def _rmsnorm_kernel(x_ref, g_ref, o_ref):
    x = x_ref[...]
    ms = jnp.mean(x * x, axis=-1, keepdims=True)
    o_ref[...] = (x * lax.rsqrt(ms + EPS) * g_ref[...]).astype(o_ref.dtype)


def _rmsnorm(x, g, out_dtype, rows=512):
    m, d = x.shape
    rows = min(rows, m)
    return pl.pallas_call(
        _rmsnorm_kernel,
        grid=(m // rows,),
        in_specs=[pl.BlockSpec((rows, d), lambda i: (i, 0)),
                  pl.BlockSpec((1, d), lambda i: (0, 0))],
        out_specs=pl.BlockSpec((rows, d), lambda i: (i, 0)),
        out_shape=jax.ShapeDtypeStruct((m, d), out_dtype),
        compiler_params=_cparams("parallel"),
        name="rmsnorm",
    )(x, g.reshape(1, d))


def _mm_kernel(*refs, n_pairs, has_res, n_out, scale):
    xs = refs[0:2 * n_pairs:2]
    ws = refs[1:2 * n_pairs:2]
    pos = 2 * n_pairs
    res_ref = refs[pos] if has_res else None
    pos += int(has_res)
    outs = refs[pos:pos + n_out]
    w16_outs = refs[pos + n_out:]

    total = None
    for p, (x_ref, w_ref) in enumerate(zip(xs, ws)):
        wb = w_ref[...].astype(BF16)
        if w16_outs:
            w16_outs[p][...] = wb
        d = jnp.dot(x_ref[...], wb, preferred_element_type=F32)
        total = d if total is None else total + d
    if scale != 1.0:
        total = total * scale
    if has_res:
        total = total + res_ref[...]
    for o_ref in outs:
        o_ref[...] = total.astype(o_ref.dtype)


def _matmul(pairs, n, out_dtypes, *, residual=None, scale=1.0, bm, bn, emit_w16=False, name="matmul"):
    m = pairs[0][0].shape[0]
    bm = min(bm, m)
    bn = min(bn, n)
    assert m % bm == 0 and n % bn == 0
    in_specs, args = [], []
    for x, w, r0, c0 in pairs:
        kx = x.shape[1]
        assert r0 % kx == 0 and c0 % bn == 0
        mode = dict(pipeline_mode=pl.Buffered(1)) if m == bm else {}
        in_specs.append(pl.BlockSpec((bm, kx), lambda j, i: (i, 0), **mode))
        in_specs.append(pl.BlockSpec((kx, bn), lambda j, i, rb=r0 // kx, cb=c0 // bn: (rb, cb + j)))
        args += [x, w]
    if residual is not None:
        in_specs.append(pl.BlockSpec((bm, bn), lambda j, i: (i, j)))
        args.append(residual)
    out_specs = [pl.BlockSpec((bm, bn), lambda j, i: (i, j)) for _ in out_dtypes]
    out_shape = [jax.ShapeDtypeStruct((m, n), dt) for dt in out_dtypes]
    if emit_w16:
        for x, _, _, _ in pairs:
            out_specs.append(pl.BlockSpec((x.shape[1], bn), lambda j, i: (0, j)))
            out_shape.append(jax.ShapeDtypeStruct((x.shape[1], n), BF16))
    kern = functools.partial(_mm_kernel, n_pairs=len(pairs), has_res=residual is not None,
                             n_out=len(out_dtypes), scale=scale)
    return pl.pallas_call(
        kern,
        grid=(n // bn, m // bm),
        in_specs=in_specs,
        out_specs=out_specs,
        out_shape=out_shape,
        compiler_params=_cparams("parallel", "arbitrary"),
        name=name,
    )(*args)


def _gateup_kernel(*refs, emit_w16, has_cast):
    x_ref, wg_ref, wu_ref = refs[:3]
    pos = 3 + int(has_cast)
    o_ref = refs[pos]
    x = x_ref[...]
    wg = wg_ref[...].astype(BF16)
    wu = wu_ref[...].astype(BF16)
    if emit_w16:
        refs[pos + 1][...] = wg
        refs[pos + 2][...] = wu
    if has_cast:
        refs[-1][...] = refs[3][...].astype(BF16)
    g = jnp.dot(x, wg, preferred_element_type=F32)
    u = jnp.dot(x, wu, preferred_element_type=F32)
    o_ref[...] = (_silu(g) * u).astype(o_ref.dtype)


def _gateup(x, wg, wu, *, bm, bn, emit_w16=False, cast_src=None):
    m, kdim = x.shape
    n = wg.shape[1]
    bm = min(bm, m)
    assert m % bm == 0 and n % bn == 0
    steps = n // bn
    w_spec = pl.BlockSpec((kdim, bn), lambda j, i: (0, j))
    x_spec = pl.BlockSpec((bm, kdim), lambda j, i: (i, 0))
    in_specs, args = [x_spec, w_spec, w_spec], [x, wg, wu]
    out_specs = [pl.BlockSpec((bm, bn), lambda j, i: (i, j))]
    out_shape = [jax.ShapeDtypeStruct((m, n), BF16)]
    if emit_w16:
        out_specs += [w_spec, w_spec]
        out_shape += [jax.ShapeDtypeStruct((kdim, n), BF16)] * 2
    if cast_src is not None:
        rows, width = cast_src.shape
        assert m == bm and rows % steps == 0 and (rows // steps) % 8 == 0
        in_specs[0] = pl.BlockSpec((bm, kdim), lambda j, i: (i, 0), pipeline_mode=pl.Buffered(1))
        cast_spec = pl.BlockSpec((rows // steps, width), lambda j, i: (j, 0))
        in_specs.append(cast_spec)
        args.append(cast_src)
        out_specs.append(cast_spec)
        out_shape.append(jax.ShapeDtypeStruct((rows, width), BF16))
    kern = functools.partial(_gateup_kernel, emit_w16=emit_w16, has_cast=cast_src is not None)
    return pl.pallas_call(
        kern,
        grid=(steps, m // bm),
        in_specs=in_specs,
        out_specs=out_specs,
        out_shape=out_shape,
        compiler_params=_cparams("parallel", "arbitrary"),
        name="ffn_gate_up",
    )(*args)


def _rel_bucket(rel):
    nb = REL_BUCKETS // 2
    max_exact = nb // 2
    ret = jnp.where(rel > 0, nb, 0)
    n = jnp.abs(rel)
    nf = jnp.maximum(n, 1).astype(F32)
    large = max_exact + (jnp.log(nf / max_exact) / math.log(REL_MAX_DIST / max_exact)
                         * (nb - max_exact)).astype(jnp.int32)
    large = jnp.minimum(large, nb - 1)
    return ret + jnp.where(n < max_exact, n, large)


def _bias_tile_kernel(far_ref, table_ref, bucket_ref, mask_ref, o_ref):
    h = pl.program_id(0)
    bucket = bucket_ref[...]
    acc = jnp.zeros(bucket.shape, F32)
    for b in range(REL_BUCKETS):
        acc = jnp.where(bucket == b, table_ref[b, h], acc)
    o_ref[0] = (acc - table_ref[far_ref[0], h]) * LOG2E + mask_ref[...]


def _bias_tiles(table, bucket, maskadd, far_bucket, n_heads):
    r, c = bucket.shape
    return pl.pallas_call(
        _bias_tile_kernel,
        grid=(n_heads,),
        in_specs=[pl.BlockSpec(memory_space=pltpu.SMEM),
                  pl.BlockSpec(memory_space=pltpu.SMEM),
                  pl.BlockSpec((r, c), lambda h: (0, 0)),
                  pl.BlockSpec((r, c), lambda h: (0, 0))],
        out_specs=pl.BlockSpec((1, r, c), lambda h: (h, 0, 0)),
        out_shape=jax.ShapeDtypeStruct((n_heads, r, c), F32),
        compiler_params=_cparams("arbitrary"),
        name="rel_bias_tiles",
    )(far_bucket.reshape(1).astype(jnp.int32), table.astype(F32), bucket.astype(jnp.int32), maskadd)


def _diff_lambda(lp):
    a = jnp.sum(lp[0:1, :] * lp[1:2, :], axis=-1, keepdims=True)
    b = jnp.sum(lp[2:3, :] * lp[3:4, :], axis=-1, keepdims=True)
    return jnp.exp(a) - jnp.exp(b) + LAM_INIT_L0


def _stack_maps(q):
    lane = lax.broadcasted_iota(jnp.int32, q.shape, 1)
    zero = jnp.zeros_like(q)
    return jnp.concatenate([jnp.where(lane < DIFF_QK, q, zero),
                            jnp.where(lane >= DIFF_QK, q, zero)], axis=0)


def _diff_epilogue(acc, l, lam, g, tq):
    o = acc * (1.0 / l)
    o = o[:tq] - lam * o[tq:]
    ms = jnp.mean(o * o, axis=-1, keepdims=True)
    return o * lax.rsqrt(ms + EPS) * g * (1.0 - LAM_INIT_L0)


def _attn_prompt_kernel(lam_ref, g_ref, q_ref, qn_ref, k_ref, v_ref, bias_ref, o_ref, qs_ref, s_ref, m_ref,
                        acc_ref, *, tile):
    i = pl.program_id(1)
    n_lane_tiles = tile // V7X_LANES
    ones = jnp.ones((tile, V7X_LANES), BF16)
    nt = (((1,), (1,)), ((), ()))

    qs_ref[...] = _stack_maps(q_ref[...])
    m_ref[...] = jnp.full(m_ref.shape, MASK_VALUE, F32)
    acc_ref[...] = jnp.zeros(acc_ref.shape, F32)

    def key_rows(d):
        return pl.ds(pl.multiple_of((i - d) * tile, tile), tile)

    def scores(d, slot):
        s_ref[slot] = lax.dot_general(qs_ref[...], k_ref[key_rows(d), :], nt, preferred_element_type=F32)

    def absorb(d, slot, bias):
        vt = jnp.concatenate([v_ref[key_rows(d), :], ones], axis=1)
        s = s_ref[slot]
        if bias is not None:
            s = (s.reshape(2, tile, tile) + bias[None]).reshape(2 * tile, tile)
        cols = [s[:, c * V7X_LANES:(c + 1) * V7X_LANES] for c in range(n_lane_tiles)]
        mx = cols[0]
        for sc in cols[1:]:
            mx = jnp.maximum(mx, sc)
        m_prev = m_ref[...]
        m_new = jnp.maximum(m_prev, jnp.max(mx, axis=-1, keepdims=True))
        alpha = jnp.exp2(m_prev - m_new)
        p = jnp.concatenate([jnp.exp2(sc - m_new).astype(BF16) for sc in cols], axis=1)
        pv = jnp.dot(p, vt, preferred_element_type=F32)
        acc_ref[...] = jnp.concatenate([alpha, alpha], axis=1) * acc_ref[...] + pv
        m_ref[...] = m_new

    i_even = lax.rem(i, 2) == 0

    @pl.when(i == 0)
    def _():
        scores(0, DIAG_SLOT)

    @pl.when(jnp.logical_and(i_even, i >= 2))
    def _():
        scores(i - 1, 1)
        absorb(i, 0, None)

    d_odd = i - 1 + lax.rem(i, 2)

    def pair_body(u, carry):
        d = d_odd - 2 * u
        scores(d - 1, 0)
        absorb(d, 1, None)
        scores(d - 2, 1)
        absorb(d - 1, 0, None)
        return carry

    lax.fori_loop(0, jnp.where(d_odd >= 3, (d_odd - 1) // 2, 0), pair_body, 0)

    @pl.when(i >= 1)
    def _():
        scores(0, DIAG_SLOT)
        absorb(1, 1, bias_ref[0, :, :tile])

    def diagonal(next_slot):
        if next_slot is not None:
            s_ref[next_slot] = lax.dot_general(_stack_maps(qn_ref[...]), k_ref[pl.ds(0, tile), :], nt,
                                               preferred_element_type=F32)
        absorb(0, DIAG_SLOT, bias_ref[0, :, tile:])

    has_next = i + 1 < pl.num_programs(1)

    @pl.when(jnp.logical_and(has_next, jnp.logical_not(i_even)))
    def _():
        diagonal(0)

    @pl.when(jnp.logical_and(has_next, i_even))
    def _():
        diagonal(1)

    @pl.when(jnp.logical_not(has_next))
    def _():
        diagonal(None)

    acc = acc_ref[...]
    y = _diff_epilogue(acc[:, :HEAD_DIM], acc[:, HEAD_DIM:], _diff_lambda(lam_ref[...]), g_ref[...], tile)
    o_ref[...] = y.astype(o_ref.dtype)


def _attn_prompt(lam_p, g, q, k, v, bias, n_heads, tile=ATTN_TILE):
    t = q.shape[0]
    nq = t // tile
    kern = functools.partial(_attn_prompt_kernel, tile=tile)
    return pl.pallas_call(
        kern,
        grid=(n_heads, nq),
        in_specs=[pl.BlockSpec((4, DIFF_QK), lambda h, i: (0, 0)),
                  pl.BlockSpec((1, HEAD_DIM), lambda h, i: (0, 0)),
                  pl.BlockSpec((tile, HEAD_DIM), lambda h, i: (i, h)),
                  pl.BlockSpec((tile, HEAD_DIM), lambda h, i: (jnp.minimum(i + 1, nq - 1), h)),
                  pl.BlockSpec((t, HEAD_DIM), lambda h, i: (0, h)),
                  pl.BlockSpec((t, HEAD_DIM), lambda h, i: (0, h)),
                  pl.BlockSpec((1, tile, 2 * tile), lambda h, i: (h, 0, 0))],
        out_specs=pl.BlockSpec((tile, HEAD_DIM), lambda h, i: (i, h)),
        out_shape=jax.ShapeDtypeStruct((t, n_heads * HEAD_DIM), BF16),
        scratch_shapes=[pltpu.VMEM((2 * tile, HEAD_DIM), BF16),
                        pltpu.VMEM((DIAG_SLOT + 1, 2 * tile, tile), F32),
                        pltpu.VMEM((2 * tile, V7X_LANES), F32),
                        pltpu.VMEM((2 * tile, 2 * HEAD_DIM), F32)],
        compiler_params=_cparams("parallel", "arbitrary"),
        name="diff_attn_prompt",
    )(lam_p, g, q, q, k, v, bias)


def _attn_sample_step(part, n_parts, lam_ref, g_ref, q_ref, kn_ref, vn_ref, kc_ref, vc_ref, bias_ref, o_ref,
                      m_ref, acc_ref, *, n_heads, keys, ts):
    is_last = part == n_parts - 1
    n_lane_tiles = keys // V7X_LANES
    ones_c = jnp.ones((keys, V7X_LANES), BF16)
    ones_n = jnp.ones((ts, V7X_LANES), BF16)
    nt = (((1,), (1,)), ((), ()))

    @pl.when(part == 0)
    def _():
        m_ref[...] = jnp.full(m_ref.shape, MASK_VALUE, F32)
        acc_ref[...] = jnp.zeros(acc_ref.shape, F32)

    for h in range(n_heads):
        col = slice(h * HEAD_DIM, (h + 1) * HEAD_DIM)
        rows = pl.ds(h, keys, stride=n_heads)
        qs = _stack_maps(q_ref[0, :, col])
        kc = kc_ref[0, rows, :].astype(BF16)
        vc = jnp.concatenate([vc_ref[0, rows, :].astype(BF16), ones_c], axis=1)
        vn = jnp.concatenate([vn_ref[0, :, col], ones_n], axis=1)
        bias = bias_ref[h]
        near_bias = jnp.where(is_last, bias[:, :SAMPLE_NEAR], 0.0)
        new_bias = jnp.where(is_last, bias[:, SAMPLE_NEAR:], MASK_VALUE)

        s = lax.dot_general(qs, kc, nt, preferred_element_type=F32)
        cols = [s[:, c * V7X_LANES:(c + 1) * V7X_LANES] for c in range(n_lane_tiles)]
        cols[-1] = (cols[-1].reshape(2, ts, SAMPLE_NEAR) + near_bias[None]).reshape(2 * ts, SAMPLE_NEAR)
        s_new = lax.dot_general(qs, kn_ref[0, :, col], nt, preferred_element_type=F32)
        s_new = (s_new.reshape(2, ts, ts) + new_bias[None]).reshape(2 * ts, ts)

        mx = cols[0]
        for sc in cols[1:]:
            mx = jnp.maximum(mx, sc)
        m_prev = m_ref[h]
        m_new = jnp.maximum(m_prev, jnp.maximum(jnp.max(mx, axis=-1, keepdims=True),
                                                jnp.max(s_new, axis=-1, keepdims=True)))
        alpha = jnp.exp2(m_prev - m_new)
        p = jnp.concatenate([jnp.exp2(sc - m_new).astype(BF16) for sc in cols], axis=1)
        p_new = jnp.exp2(s_new - m_new[:, :ts]).astype(BF16)
        pv = (jnp.dot(p, vc, preferred_element_type=F32)
              + jnp.dot(p_new, vn, preferred_element_type=F32))
        acc_ref[h] = jnp.concatenate([alpha, alpha], axis=1) * acc_ref[h] + pv
        m_ref[h] = m_new

    @pl.when(is_last)
    def _():
        lam = _diff_lambda(lam_ref[...])
        g = g_ref[...]
        for h in range(n_heads):
            acc = acc_ref[h]
            y = _diff_epilogue(acc[:, :HEAD_DIM], acc[:, HEAD_DIM:], lam, g, ts)
            o_ref[0, :, h * HEAD_DIM:(h + 1) * HEAD_DIM] = y.astype(o_ref.dtype)


def _proj_and_attn_sample_kernel(x_ref, w_ref, lam_ref, g_ref, q_ref, kn_ref, vn_ref, kc_ref, vc_ref, bias_ref,
                                 y_ref, o_ref, m_ref, acc_ref, *, parts, n_heads, keys, ts):
    y_ref[...] = jnp.dot(x_ref[...], w_ref[...], preferred_element_type=F32)
    part = lax.rem(pl.program_id(0), parts)
    _attn_sample_step(part, parts, lam_ref, g_ref, q_ref, kn_ref, vn_ref, kc_ref, vc_ref, bias_ref, o_ref,
                      m_ref, acc_ref, n_heads=n_heads, keys=keys, ts=ts)


def _proj_and_attn_sample(x, w16, lam_p, g, q, kn, vn, kc, vc, bias, n_heads):
    m, kdim = x.shape
    n = w16.shape[1]
    bm, bn = FUSED_PROJ_BLOCK
    nm, nn = m // bm, n // bn
    nb, ts, d = q.shape
    past = kc.shape[1] // n_heads
    parts = SAMPLE_CACHE_PARTS
    keys = past // parts
    assert m % bm == 0 and n % bn == 0 and nm * nn == nb * parts
    assert past % parts == 0 and keys % V7X_LANES == 0 and SAMPLE_NEAR == V7X_LANES
    kern = functools.partial(_proj_and_attn_sample_kernel, parts=parts, n_heads=n_heads, keys=keys, ts=ts)
    new = pl.BlockSpec((1, ts, d), lambda s: (s // parts, 0, 0))
    cache = pl.BlockSpec((1, keys * n_heads, HEAD_DIM), lambda s: (s // parts, s % parts, 0))
    return pl.pallas_call(
        kern,
        grid=(nb * parts,),
        in_specs=[pl.BlockSpec((bm, kdim), lambda s: (s // nn, 0)),
                  pl.BlockSpec((kdim, bn), lambda s: (0, s % nn)),
                  pl.BlockSpec((4, DIFF_QK), lambda s: (0, 0)),
                  pl.BlockSpec((1, HEAD_DIM), lambda s: (0, 0)),
                  new, new, new, cache, cache,
                  pl.BlockSpec((n_heads, ts, SAMPLE_NEAR + ts), lambda s: (0, 0, 0))],
        out_specs=[pl.BlockSpec((bm, bn), lambda s: (s // nn, s % nn)), new],
        out_shape=[jax.ShapeDtypeStruct((m, n), F32),
                   jax.ShapeDtypeStruct((nb, ts, d), BF16)],
        scratch_shapes=[pltpu.VMEM((n_heads, 2 * ts, V7X_LANES), F32),
                        pltpu.VMEM((n_heads, 2 * ts, 2 * HEAD_DIM), F32)],
        compiler_params=_cparams("arbitrary"),
        name="proj_hgrn_prompt_and_diff_attn_sample",
    )(x, w16, lam_p, g, q, kn, vn, kc, vc, bias)


def _split3(x):
    hi = x.astype(BF16)
    r1 = x - hi.astype(F32)
    mid = r1.astype(BF16)
    lo = (r1 - mid.astype(F32)).astype(BF16)
    return hi, mid, lo


def _silu(x):
    h = 0.5 * x
    return h + h * jnp.tanh(h)


def _hgrn_kernel(lbl_ref, g_ref, hq_ref, hf_ref, hi_ref, hg_ref, s0_ref, o_ref, s_out_ref, st_ref, *,
                 n_heads, chain):
    c = pl.program_id(1)
    nc = pl.num_programs(1)
    n_streams = st_ref.shape[0]
    n_chunks = HGRN_CHUNKS_PER_STEP
    rows = n_chunks * CHUNK
    d = n_heads * HEAD_DIM

    @pl.when(c == 0)
    def _():
        for s in range(n_streams):
            for h in range(n_heads):
                st_ref[s, h] = s0_ref[s, h].T

    lg = lbl_ref[...]
    e = jnp.exp(lg - jnp.max(lg, axis=0, keepdims=True))
    lb = e[0:1, :] / jnp.sum(e, axis=0, keepdims=True)

    c_half = 0.5 * (1.0 - lb)
    t = jnp.tanh(0.5 * hf_ref[...].reshape(rows, d))
    ct = c_half * t
    logf = jnp.log2((lb + c_half) + ct)
    kk = c_half - ct
    qq = _silu(hq_ref[...].reshape(rows, d))

    row = lax.broadcasted_iota(jnp.int32, (rows, rows), 0)
    col = lax.broadcasted_iota(jnp.int32, (rows, rows), 1)
    causal = jnp.logical_and(col <= row, col // CHUNK == row // CHUNK)
    tri = jnp.where(causal, 1.0, 0.0).astype(BF16)
    b = None
    for part in _split3(logf):
        term = jnp.dot(tri, part, preferred_element_type=F32)
        b = term if b is None else b + term
    qd = (qq * jnp.exp2(b)).astype(BF16)
    kd32 = kk * jnp.exp2(-b)
    kd = kd32.astype(BF16)
    decays = [jnp.exp2(b[(r + 1) * CHUNK - 1:(r + 1) * CHUNK, :]) for r in range(n_chunks)]
    k_last = jnp.concatenate([kd32[r * CHUNK:(r + 1) * CHUNK] * decays[r] for r in range(n_chunks)],
                             axis=0).astype(BF16)
    v = hi_ref[...].reshape(rows, d).astype(BF16)
    gate = _silu(hg_ref[...].reshape(rows, d))
    g = g_ref[...]

    nt = (((1,), (1,)), ((), ()))
    tn = (((0,), (0,)), ((), ()))
    for h in range(n_heads):
        cols = slice(h * HEAD_DIM, (h + 1) * HEAD_DIM)
        a = lax.dot_general(qd[:, cols], kd[:, cols], nt, preferred_element_type=F32)
        a = jnp.where(causal, a, 0.0)
        o_intra = jnp.dot(a.astype(BF16), v[:, cols], preferred_element_type=F32)
        o_parts = []
        st = st_ref[0, h]
        for r in range(n_chunks):
            rr = slice(r * CHUNK, (r + 1) * CHUNK)
            if not chain:
                st = st_ref[r, h]
            o_parts.append(o_intra[rr] + lax.dot_general(qd[rr, cols], st.astype(BF16), nt,
                                                         preferred_element_type=F32))
            st = decays[r][:, cols] * st + lax.dot_general(v[rr, cols], k_last[rr, cols], tn,
                                                           preferred_element_type=F32)
            if not chain:
                st_ref[r, h] = st
        if chain:
            st_ref[0, h] = st
        o = jnp.concatenate(o_parts, axis=0)
        ms = jnp.mean(o * o, axis=-1, keepdims=True)
        y = (o * lax.rsqrt(ms + EPS) * g * gate[:, cols]).astype(o_ref.dtype)
        if chain:
            o_ref[0, :, cols] = y
        else:
            for r in range(n_chunks):
                o_ref[r, :, cols] = y[r * CHUNK:(r + 1) * CHUNK]

    @pl.when(c == nc - 1)
    def _():
        for s in range(n_streams):
            for h in range(n_heads):
                s_out_ref[s, h] = st_ref[s, h].T


def _hgrn(lb_logits, g, hpart, s0, n_heads):
    nb, t, _ = hpart.shape
    d = n_heads * HEAD_DIM
    per = HGRN_CHUNKS_PER_STEP
    chain = t > CHUNK
    sb, tb = (1, per * CHUNK) if chain else (per, CHUNK)
    assert nb % sb == 0 and t % tb == 0
    kern = functools.partial(_hgrn_kernel, n_heads=n_heads, chain=chain)
    part = lambda p: pl.BlockSpec((sb, tb, d), lambda b, c: (b, c, p))
    state = pl.BlockSpec((sb, n_heads, HEAD_DIM, HEAD_DIM), lambda b, c: (b, 0, 0, 0))
    return pl.pallas_call(
        kern,
        grid=(nb // sb, t // tb),
        in_specs=[pl.BlockSpec(lb_logits.shape, lambda b, c: (0, 0)),
                  pl.BlockSpec((1, HEAD_DIM), lambda b, c: (0, 0)),
                  part(0), part(1), part(2), part(3), state],
        out_specs=[pl.BlockSpec((sb, tb, d), lambda b, c: (b, c, 0)), state],
        out_shape=[jax.ShapeDtypeStruct((nb, t, d), BF16),
                   jax.ShapeDtypeStruct(s0.shape, F32)],
        scratch_shapes=[pltpu.VMEM((sb, n_heads, HEAD_DIM, HEAD_DIM), F32)],
        compiler_params=_cparams("parallel", "arbitrary"),
        name="hgrn2_chunks",
    )(lb_logits, g, hpart, hpart, hpart, hpart, s0)


def _cross_kernel(q_ref, k_ref, v_ref, o_ref, *, heads, hd, split_heads):
    nt = (((1,), (1,)), ((), ()))
    if split_heads:
        tq = q_ref.shape[1]
        n_mem = k_ref.shape[1]
        xk = k_ref[0].reshape(n_mem * heads, hd).astype(BF16)
        xv = v_ref[0].reshape(n_mem * heads, hd).astype(BF16)
        q = jnp.concatenate([q_ref[0, :, h * hd:(h + 1) * hd] for h in range(heads)], axis=0)
        s = lax.dot_general(q, xk, nt, preferred_element_type=F32)
        row_head = lax.broadcasted_iota(jnp.int32, s.shape, 0) // tq
        col_head = lax.rem(lax.broadcasted_iota(jnp.int32, s.shape, 1), heads)
        s = jnp.where(row_head == col_head, s, MASK_VALUE)
        p = jnp.exp(s - jnp.max(s, axis=-1, keepdims=True))
        l = jnp.sum(p, axis=-1, keepdims=True)
        o = jnp.dot(p.astype(BF16), xv, preferred_element_type=F32) * (1.0 / l)
        for h in range(heads):
            o_ref[0, :, h * hd:(h + 1) * hd] = o[h * tq:(h + 1) * tq].astype(o_ref.dtype)
        return
    for h in range(heads):
        cols = slice(h * hd, (h + 1) * hd)
        q = q_ref[0, :, cols]
        k = k_ref[0, :, cols].astype(BF16)
        v = v_ref[0, :, cols].astype(BF16)
        s = lax.dot_general(q, k, nt, preferred_element_type=F32)
        p = jnp.exp(s - jnp.max(s, axis=-1, keepdims=True))
        l = jnp.sum(p, axis=-1, keepdims=True)
        o = jnp.dot(p.astype(BF16), v, preferred_element_type=F32) * (1.0 / l)
        o_ref[0, :, cols] = o.astype(o_ref.dtype)


def _cross_attn(q, mk, mv, tq):
    nb, t, d = q.shape
    split_heads = mk.ndim == 4
    kern = functools.partial(_cross_kernel, heads=MEM_HEADS, hd=d // MEM_HEADS, split_heads=split_heads)
    if split_heads:
        mem_spec = pl.BlockSpec((1,) + mk.shape[1:], lambda b, i: (b, 0, 0, 0))
    else:
        mem_spec = pl.BlockSpec((1, N_MEM, d), lambda b, i: (b, 0, 0))
    return pl.pallas_call(
        kern,
        grid=(nb, t // tq),
        in_specs=[pl.BlockSpec((1, tq, d), lambda b, i: (b, i, 0)),
                  mem_spec,
                  mem_spec],
        out_specs=pl.BlockSpec((1, tq, d), lambda b, i: (b, i, 0)),
        out_shape=jax.ShapeDtypeStruct((nb, t, d), BF16),
        compiler_params=_cparams("parallel", "parallel"),
        name="mem_cross_attn",
    )(q, mk, mv)


def _named_matmul(w, blocks, emit_w16, w16):
    def mm(names, xs, n, out_dtypes, wide=False, **kw):
        bm, bn = MM_BLOCK_F32_WEIGHTS_WIDE if (wide and emit_w16) else blocks
        pairs = [(xi,) + w[nm] for nm, xi in zip(names, xs)]
        outs = _matmul(pairs, n, out_dtypes, bm=bm, bn=bn, emit_w16=emit_w16, **kw)
        if emit_w16:
            for nm, arr in zip(names, outs[len(out_dtypes):]):
                w16[nm] = (arr, 0, 0)
        return outs[:len(out_dtypes)]

    return mm


def _in_proj(x2, norms, w, widths, *, blocks, emit_w16, with_hgrn):
    w16 = {}
    mm = _named_matmul(w, blocks, emit_w16, w16)
    n = _rmsnorm(x2, norms["mix"], BF16)
    (dq,) = mm(["dq"], [n], widths["diff"], [BF16], scale=DIFF_QK ** -0.5 * LOG2E, wide=True, name="proj_dq")
    dk32, dk16 = mm(["dk"], [n], widths["diff"], [F32, BF16], name="proj_dk")
    dv32, dv16 = mm(["dv"], [n], widths["diff"], [F32, BF16], name="proj_dv")
    hpart = mm(["h"], [n], widths["hgrn_in"], [F32], wide=True, name="proj_hgrn")[0] if with_hgrn else None
    return n, dq, dk32, dk16, dv32, dv16, hpart, w16


def _tail(x2, nb, t, cat_diff, cat_hgrn, norms, w, mem_kv_fn, *, blocks, emit_w16):
    m, d = x2.shape
    w16 = {}
    mm = _named_matmul(w, blocks, emit_w16, w16)

    (h1,) = mm(["out_diff", "out_hgrn"], [cat_diff, cat_hgrn], d, [F32], residual=x2, name="proj_out")

    n2 = _rmsnorm(h1, norms["cross"], BF16)
    (cq,) = mm(["mem_q"], [n2], d, [BF16], scale=(d // MEM_HEADS) ** -0.5, wide=True, name="proj_mem_q")
    mk, mv = mem_kv_fn()
    co = _cross_attn(cq.reshape(nb, t, d), mk, mv, tq=min(t, 512))
    (h2,) = mm(["mem_o"], [co.reshape(m, d)], d, [F32], residual=h1, name="proj_mem_o")

    n3 = _rmsnorm(h2, norms["ffn"], BF16)
    down_f32 = w.get("down_f32")
    gu = _gateup(n3, w["gate"][0], w["up"][0], bm=FFN_GATE_UP_BLOCK[0], bn=FFN_GATE_UP_BLOCK[1],
                 emit_w16=emit_w16, cast_src=down_f32)
    act = gu[0]
    if emit_w16:
        w16["gate"], w16["up"] = (gu[1], 0, 0), (gu[2], 0, 0)
    if down_f32 is not None:
        w16["down"] = (gu[-1], 0, 0)
    down = w16["down"] if down_f32 is not None else w["down"]
    (h3,) = _matmul([(act,) + down], d, [F32], residual=h2, bm=FFN_DOWN_BLOCK[0], bn=FFN_DOWN_BLOCK[1],
                    name="ffn_down")
    y = _rmsnorm(h3, norms["final"], F32)
    return y.reshape(nb, t, d), w16


def kernel(x_prompt, x_sample, mem_prompt, cache_diff_k, cache_diff_v, state_hgrn, cache_mem_k, cache_mem_v, norm_mix, w_in, diff_lambda, diff_subln, rel_bias, hgrn_lb_logits, hgrn_norm, w_out, norm_cross, norm_mem, w_mem_q, w_mem_k, w_mem_v, w_mem_o, norm_ffn, w_ffn_gate, w_ffn_up, w_ffn_down, norm_final):
    bp, tp, d = x_prompt.shape
    bs, ts, _ = x_sample.shape
    depth = w_in.shape[0]
    assert depth == 1 and bp == 1
    past = cache_diff_k.shape[2]
    n_dh = cache_diff_k.shape[3]
    d_diff = n_dh * HEAD_DIM
    n_hh = state_hgrn.shape[2]
    d_hgrn = n_hh * HEAD_DIM
    assert past % CHUNK == 0 and ts == CHUNK and tp % ATTN_TILE == 0

    norms = {"mix": norm_mix[0], "cross": norm_cross[0], "ffn": norm_ffn[0], "final": norm_final}
    widths = {"diff": d_diff, "hgrn_in": w_in.shape[2] - 3 * d_diff}
    w_f32 = {
        "dq": (w_in[0], 0, 0), "dk": (w_in[0], 0, d_diff), "dv": (w_in[0], 0, 2 * d_diff),
        "h": (w_in[0], 0, 3 * d_diff),
        "out_diff": (w_out[0], 0, 0), "out_hgrn": (w_out[0], d_diff, 0),
        "mem_q": (w_mem_q[0], 0, 0), "mem_o": (w_mem_o[0], 0, 0),
        "gate": (w_ffn_gate[0], 0, 0), "up": (w_ffn_up[0], 0, 0),
        "down_f32": w_ffn_down[0],
    }
    lam_p = diff_lambda[0].astype(F32)
    subln = diff_subln[0].reshape(1, HEAD_DIM)
    hnorm = hgrn_norm[0].reshape(1, HEAD_DIM)

    tile = ATTN_TILE
    r = jnp.arange(tile)[:, None]
    c = jnp.arange(2 * tile)[None, :]
    rel_p = (c - tile) - r
    vis_p = jnp.logical_or(c < tile, (c - tile) // CHUNK <= r // CHUNK)
    bias_p = _bias_tiles(rel_bias, _rel_bucket(rel_p), jnp.where(vis_p, 0.0, MASK_VALUE).astype(F32),
                         _rel_bucket(jnp.int32(-2 * tile)), n_dh)
    qpos = past + jnp.arange(ts)[:, None]
    kpos = (past - SAMPLE_NEAR) + jnp.arange(SAMPLE_NEAR + ts)[None, :]
    vis_s = kpos // CHUNK <= qpos // CHUNK
    bias_s = _bias_tiles(rel_bias, _rel_bucket(kpos - qpos), jnp.where(vis_s, 0.0, MASK_VALUE).astype(F32),
                         _rel_bucket(jnp.int32(-(SAMPLE_NEAR + 1))), n_dh)

    mem_out = {}

    def mem_kv_prompt():
        mn = _rmsnorm(mem_prompt.reshape(bp * N_MEM, d), norm_mem[0], BF16)
        bm, bn = MM_BLOCK_F32_WEIGHTS
        (mk,) = _matmul([(mn, w_mem_k[0], 0, 0)], d, [F32], bm=bm, bn=bn, name="proj_mem_k")
        (mv,) = _matmul([(mn, w_mem_v[0], 0, 0)], d, [F32], bm=bm, bn=bn, name="proj_mem_v")
        mem_out["k"], mem_out["v"] = mk, mv
        return mk.reshape(bp, N_MEM, d), mv.reshape(bp, N_MEM, d)

    def mem_kv_sample():
        return cache_mem_k[0], cache_mem_v[0]

    xs2 = x_sample.reshape(bs * ts, d)
    xp2 = x_prompt.reshape(bp * tp, d)
    _, dq_s, dk_s, dk16_s, dv_s, dv16_s, hpart_s, w16_in = _in_proj(
        xs2, norms, w_f32, widths, blocks=MM_BLOCK_F32_WEIGHTS, emit_w16=True, with_hgrn=True)
    n_p, dq_p, dk_p, dk16_p, dv_p, dv16_p, _, _ = _in_proj(
        xp2, norms, w16_in, widths, blocks=MM_BLOCK_BF16_WEIGHTS, emit_w16=False, with_hgrn=False)

    shp = (bs, ts, d_diff)
    hpart_p, cat_diff_s = _proj_and_attn_sample(
        n_p, w16_in["h"][0], lam_p, subln, dq_s.reshape(shp), dk16_s.reshape(shp), dv16_s.reshape(shp),
        cache_diff_k[0].reshape(bs, past * n_dh, HEAD_DIM), cache_diff_v[0].reshape(bs, past * n_dh, HEAD_DIM),
        bias_s, n_dh)

    cat_hgrn_s, s_s = _hgrn(hgrn_lb_logits, hnorm, hpart_s.reshape(bs, ts, 4 * d_hgrn), state_hgrn[0], n_hh)
    y_s, w16_tail = _tail(xs2, bs, ts, cat_diff_s.reshape(bs * ts, d_diff), cat_hgrn_s.reshape(bs * ts, d_hgrn),
                          norms, w_f32, mem_kv_sample, blocks=MM_BLOCK_F32_WEIGHTS, emit_w16=True)

    w_bf16 = {**w16_in, **w16_tail}
    cat_diff_p = _attn_prompt(lam_p, subln, dq_p, dk16_p, dv16_p, bias_p, n_dh)
    s0 = jnp.zeros((bp, n_hh, HEAD_DIM, HEAD_DIM), F32)
    cat_hgrn_p, s_p = _hgrn(hgrn_lb_logits, hnorm, hpart_p.reshape(bp, tp, 4 * d_hgrn), s0, n_hh)
    y_p, _ = _tail(xp2, bp, tp, cat_diff_p, cat_hgrn_p.reshape(bp * tp, d_hgrn), norms, w_bf16, mem_kv_prompt,
                   blocks=MM_BLOCK_BF16_WEIGHTS, emit_w16=False)

    mhd = d // MEM_HEADS
    return (y_p, y_s,
            dk_p.reshape(depth, bp, tp, n_dh, HEAD_DIM), dv_p.reshape(depth, bp, tp, n_dh, HEAD_DIM),
            s_p.reshape(depth, bp, n_hh, HEAD_DIM, HEAD_DIM),
            mem_out["k"].reshape(depth, bp, N_MEM, MEM_HEADS, mhd),
            mem_out["v"].reshape(depth, bp, N_MEM, MEM_HEADS, mhd),
            dk_s.reshape(depth, bs, ts, n_dh, HEAD_DIM), dv_s.reshape(depth, bs, ts, n_dh, HEAD_DIM),
            s_s.reshape(depth, bs, n_hh, HEAD_DIM, HEAD_DIM))
```
